```python
import math
import jax, jax.numpy as jnp
from jax import lax
import numpy as np

D_MODEL = 1024
BATCH = 8
SEQ = 2048
DEPTH = 1

GRID_W = 64
CTX_LEN = 256
MIX_W = D_MODEL
F_GROUPS = 4
F_DIM = D_MODEL // 8
FOURIER_W = F_GROUPS * F_DIM
GDN_HEADS = 4
GDN_DK = D_MODEL // 8
GDN_DV = D_MODEL // 8
QK_W = GDN_HEADS * GDN_DK
V_W = GDN_HEADS * GDN_DV
IN_W = FOURIER_W + 2 * QK_W + 2 * V_W + 4 * GDN_HEADS
SHORT_CONV = 3
CHUNK = 64
D_FF = ((8 * D_MODEL // 3 + 127) // 128) * 128
EPS = 1e-6

kernel_name = "hybrid_fourier_gdn_convglu_dit"


def rmsnorm(t, g):
    tf = t.astype(jnp.float32)
    y = tf * lax.rsqrt(jnp.mean(tf * tf, axis=-1, keepdims=True) + EPS)
    return y.astype(t.dtype) * g


def l2norm(t):
    tf = t.astype(jnp.float32)
    return (tf * lax.rsqrt(jnp.sum(tf * tf, axis=-1, keepdims=True) + EPS)).astype(t.dtype)


def modulate(h, shift, scale):
    return h * (1 + scale) + shift


def dwconv1d(t, w):
    k_w = w.shape[0]
    pad = k_w // 2
    length = t.shape[1]
    tp = jnp.pad(t, ((0, 0), (pad, pad), (0, 0)))
    out = tp[:, 0:length] * w[0]
    for j in range(1, k_w):
        out = out + tp[:, j:j + length] * w[j]
    return out


def gated_delta_chunked(q, k, v, beta, g, s0):
    f32 = jnp.float32
    bsz, length, heads, dk = q.shape
    dv = v.shape[-1]
    n_chunks = length // CHUNK

    def chunks(t):
        t = t.astype(f32).reshape(bsz, n_chunks, CHUNK, heads, t.shape[-1])
        return jnp.transpose(t, (1, 0, 3, 2, 4))

    def chunks_s(t):
        t = t.astype(f32).reshape(bsz, n_chunks, CHUNK, heads)
        return jnp.transpose(t, (1, 0, 3, 2))

    qc = chunks(q) * (dk ** -0.5)
    kc = chunks(k)
    vc = chunks(v)
    bc = chunks_s(beta)
    gc = jnp.cumsum(chunks_s(g), axis=-1)
    lower = jnp.tril(jnp.ones((CHUNK, CHUNK), dtype=bool))
    strict = jnp.tril(jnp.ones((CHUNK, CHUNK), dtype=bool), -1)
    diff = gc[..., :, None] - gc[..., None, :]
    decay = jnp.where(lower, jnp.exp(jnp.where(lower, diff, 0.0)), 0.0)

    kb = kc * bc[..., None]
    vb = vc * bc[..., None]
    a_mat = jnp.where(strict, jnp.einsum('nbhid,nbhjd->nbhij', kb, kc) * decay, 0.0)
    eye = jnp.eye(CHUNK, dtype=f32)
    t_inv = lax.linalg.triangular_solve(eye + a_mat, jnp.broadcast_to(eye, a_mat.shape),
                                        left_side=True, lower=True, unit_diagonal=True)
    u_val = jnp.einsum('nbhij,nbhjd->nbhid', t_inv, vb)
    w_key = jnp.einsum('nbhij,nbhjd->nbhid', t_inv, kb * jnp.exp(gc)[..., None])
    attn_intra = jnp.where(lower, jnp.einsum('nbhid,nbhjd->nbhij', qc, kc) * decay, 0.0)
    g_last = gc[..., -1]
    k_dec = kc * jnp.exp(g_last[..., None] - gc)[..., None]
    q_dec = qc * jnp.exp(gc)[..., None]

    def step(s, xs):
        u_i, w_i, a_i, qd_i, kd_i, gl_i = xs
        v_new = u_i - jnp.einsum('bhcd,bhde->bhce', w_i, s)
        o_i = jnp.einsum('bhcd,bhde->bhce', qd_i, s) + jnp.einsum('bhij,bhje->bhie', a_i, v_new)
        s = s * jnp.exp(gl_i)[..., None, None] + jnp.einsum('bhcd,bhce->bhde', kd_i, v_new)
        return s, o_i

    s_final, o = lax.scan(step, s0.astype(f32), (u_val, w_key, attn_intra, q_dec, k_dec, g_last))
    o = jnp.transpose(o, (1, 0, 3, 2, 4)).reshape(bsz, length, heads, dv)
    return o.astype(v.dtype), s_final


def bidir_gdn(q, k, v, beta, g, s0_f, s0_b):
    flip = lambda t: jnp.flip(t, axis=1)
    o_f, s_f = gated_delta_chunked(q, k, v, beta[:, :, 0], g[:, :, 0], s0_f)
    o_b, s_b = gated_delta_chunked(flip(q), flip(k), flip(v), flip(beta[:, :, 1]), flip(g[:, :, 1]), s0_b)
    return o_f + flip(o_b), s_f, s_b


def mixer_inputs(h, w_in_l, conv_l, a_log_l, dt_bias_l):
    bsz, length, _ = h.shape
    p = h @ w_in_l
    o1 = FOURIER_W
    o2 = o1 + 2 * QK_W + V_W
    o3 = o2 + V_W
    u = p[..., :o1]
    qkv = jax.nn.silu(dwconv1d(p[..., o1:o2], conv_l))
    z = p[..., o2:o3]
    ab = p[..., o3:].astype(jnp.float32).reshape(bsz, length, 4, GDN_HEADS)
    q = l2norm(qkv[..., :QK_W].reshape(bsz, length, GDN_HEADS, GDN_DK))
    k = l2norm(qkv[..., QK_W:2 * QK_W].reshape(bsz, length, GDN_HEADS, GDN_DK))
    v = qkv[..., 2 * QK_W:].reshape(bsz, length, GDN_HEADS, GDN_DV)
    beta = jax.nn.sigmoid(ab[:, :, 0:2])
    g = -jnp.exp(a_log_l.astype(jnp.float32)) * jax.nn.softplus(ab[:, :, 2:4] + dt_bias_l.astype(jnp.float32))
    return u, z, q, k, v, beta, g


def fourier_mix(u, w_f):
    bsz, length, _ = u.shape
    ug = u.astype(jnp.float32).reshape(bsz, length, F_GROUPS, F_DIM)
    y = jnp.fft.fft2(ug, axes=(1, 3), norm='ortho').real.astype(u.dtype)
    return jnp.einsum('blgc,gcd->blgd', y, w_f).reshape(bsz, length, FOURIER_W)


def mixer_out(u, o, z, w_f, g_gdn_l, w_out_l):
    bsz, length, _ = z.shape
    y_f = fourier_mix(u, w_f)
    y_d = rmsnorm(o, g_gdn_l) * jax.nn.silu(z).reshape(bsz, length, GDN_HEADS, GDN_DV)
    return jnp.concatenate([y_f, y_d.reshape(bsz, length, V_W)], axis=-1) @ w_out_l


def conv_ffn(h, w_up_l, w_dwc_l, w_down_l, on_grid):
    bsz, length, _ = h.shape
    p = h @ w_up_l
    val, gate = p[..., :D_FF], p[..., D_FF:]
    if on_grid:
        rows = length // GRID_W
        gate = lax.conv_general_dilated(
            gate.reshape(bsz, rows, GRID_W, D_FF), w_dwc_l[:, :, None, :],
            window_strides=(1, 1), padding='SAME',
            dimension_numbers=('NHWC', 'HWIO', 'NHWC'),
            feature_group_count=D_FF).reshape(bsz, length, D_FF)
    else:
        gate = dwconv1d(gate, w_dwc_l[1])
    return (jax.nn.silu(gate) * val) @ w_down_l


def setup_inputs(seed: int = 0) -> dict:
    key = jax.random.key(seed)
    ks = jax.random.split(key, 20)
    f32 = jnp.float32

    def nrm(k, shape, s):
        return jax.random.normal(k, shape, f32) * s

    L = DEPTH
    x = nrm(ks[0], (BATCH, SEQ, D_MODEL), 1.0)
    c = nrm(ks[1], (BATCH, D_MODEL), 1.0)
    ctx = nrm(ks[2], (BATCH, CTX_LEN, D_MODEL), 1.0)
    c_ctx = nrm(ks[3], (D_MODEL,), 1.0)
    w_ada = nrm(ks[4], (L, D_MODEL, 6 * D_MODEL), 0.5 * D_MODEL ** -0.5)
    b_ada = nrm(ks[5], (L, 6 * D_MODEL), 0.02)
    g_pre_mix = 1.0 + nrm(ks[6], (L, D_MODEL), 0.1)
    g_post_mix = 1.0 + nrm(ks[7], (L, D_MODEL), 0.1)
    g_pre_ffn = 1.0 + nrm(ks[8], (L, D_MODEL), 0.1)
    g_post_ffn = 1.0 + nrm(ks[9], (L, D_MODEL), 0.1)
    w_in = nrm(ks[10], (L, D_MODEL, IN_W), D_MODEL ** -0.5)
    w_qkv_conv = nrm(ks[11], (L, SHORT_CONV, 2 * QK_W + V_W), SHORT_CONV ** -0.5)
    a_log = jnp.log(jax.random.uniform(ks[12], (L, 2, GDN_HEADS), f32, minval=1.0, maxval=16.0))
    dt = jnp.exp(jax.random.uniform(ks[13], (L, 2, GDN_HEADS), f32,
                                    minval=math.log(1e-3), maxval=math.log(1e-1)))
    dt_bias = dt + jnp.log(-jnp.expm1(-dt))
    g_gdn = 1.0 + nrm(ks[14], (L, GDN_DV), 0.1)
    w_fourier = nrm(ks[15], (L, F_GROUPS, F_DIM, F_DIM), F_DIM ** -0.5)
    w_out = nrm(ks[16], (L, MIX_W, D_MODEL), MIX_W ** -0.5)
    w_up = nrm(ks[17], (L, D_MODEL, 2 * D_FF), D_MODEL ** -0.5)
    w_dwc = nrm(ks[18], (L, 3, 3, D_FF), 1.0 / 3.0)
    w_down = nrm(ks[19], (L, D_FF, D_MODEL), D_FF ** -0.5)
    return {"x": x, "c": c, "ctx": ctx, "c_ctx": c_ctx, "w_ada": w_ada, "b_ada": b_ada,
            "g_pre_mix": g_pre_mix, "g_post_mix": g_post_mix, "g_pre_ffn": g_pre_ffn,
            "g_post_ffn": g_post_ffn, "w_in": w_in, "w_qkv_conv": w_qkv_conv, "a_log": a_log,
            "dt_bias": dt_bias, "g_gdn": g_gdn, "w_fourier": w_fourier, "w_out": w_out,
            "w_up": w_up, "w_dwc": w_dwc, "w_down": w_down}


def reference(x, c, ctx, c_ctx, w_ada, b_ada, g_pre_mix, g_post_mix, g_pre_ffn, g_post_ffn,
              w_in, w_qkv_conv, a_log, dt_bias, g_gdn, w_fourier, w_out, w_up, w_dwc, w_down):
    xl, xc = x, ctx
    bsz = x.shape[0]
    silu_c = jax.nn.silu(c)
    silu_cc = jax.nn.silu(c_ctx)
    for i in range(DEPTH):
        update_ctx = i < DEPTH - 1
        mod_l = (silu_c @ w_ada[i] + b_ada[i])[:, None, :]
        mod_c = silu_cc @ w_ada[i] + b_ada[i]
        sh1l, sc1l, gt1l, sh2l, sc2l, gt2l = jnp.split(mod_l, 6, axis=-1)
        sh1c, sc1c, gt1c, sh2c, sc2c, gt2c = jnp.split(mod_c, 6, axis=-1)

        hl = modulate(rmsnorm(xl, g_pre_mix[i]), sh1l, sc1l)
        hc = modulate(rmsnorm(xc, g_pre_mix[i]), sh1c, sc1c)
        ul, zl, ql, kl, vl, bl, gl = mixer_inputs(hl, w_in[i], w_qkv_conv[i], a_log[i], dt_bias[i])
        uc, zc, qc, kc, vc, bc, gc = mixer_inputs(hc, w_in[i], w_qkv_conv[i], a_log[i], dt_bias[i])
        s_zero = jnp.zeros((bsz, GDN_HEADS, GDN_DK, GDN_DV), jnp.float32)
        oc, s_f, s_b = bidir_gdn(qc, kc, vc, bc, gc, s_zero, s_zero)
        ol, _, _ = bidir_gdn(ql, kl, vl, bl, gl, s_f, s_b)
        yl = mixer_out(ul, ol, zl, w_fourier[i], g_gdn[i], w_out[i])
        xl = xl + gt1l * rmsnorm(yl, g_post_mix[i])
        if update_ctx:
            yc = mixer_out(uc, oc, zc, w_fourier[i], g_gdn[i], w_out[i])
            xc = xc + gt1c * rmsnorm(yc, g_post_mix[i])

        hl = modulate(rmsnorm(xl, g_pre_ffn[i]), sh2l, sc2l)
        xl = xl + gt2l * rmsnorm(conv_ffn(hl, w_up[i], w_dwc[i], w_down[i], True), g_post_ffn[i])
        if update_ctx:
            hc = modulate(rmsnorm(xc, g_pre_ffn[i]), sh2c, sc2c)
            xc = xc + gt2c * rmsnorm(conv_ffn(hc, w_up[i], w_dwc[i], w_down[i], False), g_post_ffn[i])
    return xl
```

```python
import functools
import math

import numpy as np
import jax
import jax.numpy as jnp
from jax import lax
from jax.experimental import pallas as pl
from jax.experimental.pallas import tpu as pltpu

F32 = jnp.float32
BF16 = jnp.bfloat16

GRID_W = 64
HEADS = 4
HEAD_D = 128
CHUNK = 64
PAIR = 2 * CHUNK
GROUP_W = HEADS * HEAD_D
N_GROUPS = 5
N_GATES = 16
EPS = 1e-6
ROW_BLOCK = 256
FFN_TILE = 256
MIX_TILE = 256
MIB = 1024 * 1024

_NT = (((1,), (1,)), ((), ()))


def _dot(a, b):
    return jnp.dot(a, b, preferred_element_type=F32)


def _silu(x):
    return x / (1.0 + jnp.exp(-x))


def _rms(x):
    return x * lax.rsqrt(jnp.mean(x * x, axis=-1, keepdims=True) + EPS)


def _params(vmem_mib, semantics):
    return pltpu.CompilerParams(dimension_semantics=semantics,
                                vmem_limit_bytes=vmem_mib * MIB)


def _ada_kernel(c_ref, w_ref, b_ref, o_ref):
    s = _silu(c_ref[...]).astype(BF16)
    o_ref[...] = _dot(s, w_ref[...].astype(BF16)) + b_ref[...]


def _ada(cc, w, b):
    rows, d = cc.shape
    n = w.shape[1]
    tn = 1024
    return pl.pallas_call(
        _ada_kernel,
        grid=(n // tn,),
        in_specs=[pl.BlockSpec((rows, d), lambda j: (0, 0)),
                  pl.BlockSpec((d, tn), lambda j: (0, j)),
                  pl.BlockSpec((1, tn), lambda j: (0, j))],
        out_specs=pl.BlockSpec((rows, tn), lambda j: (0, j)),
        out_shape=jax.ShapeDtypeStruct((rows, n), F32),
        compiler_params=_params(32, ("arbitrary",)),
        name="ada",
    )(cc, w, b)


def _inproj_kernel(x_ref, g_ref, sh_ref, sc_ref, w_ref, wabt_ref, conv_ref, alog_ref, dtb_ref,
                   pg_ref, kt_ref, gat_ref, h_s, w_s, p_s):
    j = pl.program_id(1)
    lx = x_ref.shape[0]
    n_rb = lx // ROW_BLOCK

    w_s[...] = w_ref[...].astype(BF16)

    @pl.when(j == 0)
    def _prologue():
        p_s[0:8, :] = jnp.zeros((8, GROUP_W), F32)
        p_s[lx + 8:lx + 16, :] = jnp.zeros((8, GROUP_W), F32)
        wabt = wabt_ref[...].astype(BF16)
        row = lax.broadcasted_iota(jnp.int32, (N_GATES, PAIR), 0)

        def body(rb, carry):
            r0 = pl.multiple_of(rb * ROW_BLOCK, ROW_BLOCK)
            x = x_ref[pl.ds(r0, ROW_BLOCK), :]
            h = _rms(x) * g_ref[...]
            h = h * (1.0 + sc_ref[...]) + sh_ref[...]
            hb = h.astype(BF16)
            h_s[pl.ds(r0, ROW_BLOCK), :] = hb
            for s in range(ROW_BLOCK // PAIR):
                ab = lax.dot_general(wabt, hb[s * PAIR:(s + 1) * PAIR], _NT,
                                     preferred_element_type=F32)
                beta = 1.0 / (1.0 + jnp.exp(-ab))
                xs = ab + dtb_ref[...]
                softplus = jnp.maximum(xs, 0.0) + jnp.log1p(jnp.exp(-jnp.abs(xs)))
                gate = -jnp.exp(alog_ref[...]) * softplus
                gat_ref[rb * (ROW_BLOCK // PAIR) + s] = jnp.where(row < 8, beta, gate)
            return carry

        lax.fori_loop(0, n_rb, body, 0)

    def matmul_loop(store):
        def body(rb, carry):
            r0 = pl.multiple_of(rb * ROW_BLOCK, ROW_BLOCK)
            store(r0, _dot(h_s[pl.ds(r0, ROW_BLOCK), :], w_s[...]))
            return carry
        lax.fori_loop(0, n_rb, body, 0)

    @pl.when(j == 0)
    def _u():
        def store(r0, p):
            pg_ref[pl.ds(r0, ROW_BLOCK), :] = p.astype(BF16)
        matmul_loop(store)

    @pl.when(j == N_GROUPS - 1)
    def _z():
        def store(r0, p):
            pg_ref[pl.ds(r0, ROW_BLOCK), :] = _silu(p).astype(BF16)
        matmul_loop(store)

    def conv_group(normalise, transpose):
        def store(r0, p):
            p_s[pl.ds(r0 + 8, ROW_BLOCK), :] = p
        matmul_loop(store)
        cw = conv_ref[...]
        n_ext = ROW_BLOCK + 16

        def body(rb, carry):
            r0 = pl.multiple_of(rb * ROW_BLOCK, ROW_BLOCK)
            ext = p_s[pl.ds(r0, n_ext), :]
            prev = pltpu.roll(ext, 1, 0)[8:8 + ROW_BLOCK]
            nxt = pltpu.roll(ext, n_ext - 1, 0)[8:8 + ROW_BLOCK]
            cur = ext[8:8 + ROW_BLOCK]
            a = _silu(prev * cw[0:1] + cur * cw[1:2] + nxt * cw[2:3])
            if normalise:
                segs = []
                for hh in range(HEADS):
                    seg = a[:, hh * HEAD_D:(hh + 1) * HEAD_D]
                    segs.append(seg * lax.rsqrt(jnp.sum(seg * seg, axis=-1, keepdims=True) + EPS))
                a = jnp.concatenate(segs, axis=1)
            pg_ref[pl.ds(r0, ROW_BLOCK), :] = a.astype(BF16)
            if transpose:
                at = a.T.astype(BF16)
                for s in range(ROW_BLOCK // PAIR):
                    kt_ref[rb * (ROW_BLOCK // PAIR) + s] = at[:, s * PAIR:(s + 1) * PAIR]
            return carry

        lax.fori_loop(0, n_rb, body, 0)

    @pl.when(j == 1)
    def _q():
        conv_group(True, False)

    @pl.when(j == 2)
    def _k():
        conv_group(True, True)

    @pl.when(j == 3)
    def _v():
        conv_group(False, False)


def _inproj(x, g, shift, scale, w_in, wabt, conv, alog, dtb):
    b, lx, d = x.shape
    n_pairs = lx // PAIR
    return pl.pallas_call(
        _inproj_kernel,
        grid=(b, N_GROUPS),
        in_specs=[pl.BlockSpec((None, lx, d), lambda i, j: (i, 0, 0)),
                  pl.BlockSpec((1, d), lambda i, j: (0, 0)),
                  pl.BlockSpec((None, 1, d), lambda i, j: (i, 0, 0)),
                  pl.BlockSpec((None, 1, d), lambda i, j: (i, 0, 0)),
                  pl.BlockSpec((d, GROUP_W), lambda i, j: (0, j)),
                  pl.BlockSpec((N_GATES, d), lambda i, j: (0, 0)),
                  pl.BlockSpec((3, GROUP_W), lambda i, j: (0, jnp.clip(j - 1, 0, 2))),
                  pl.BlockSpec((N_GATES, 1), lambda i, j: (0, 0)),
                  pl.BlockSpec((N_GATES, 1), lambda i, j: (0, 0))],
        out_specs=[pl.BlockSpec((None, lx, GROUP_W), lambda i, j: (i, 0, j)),
                   pl.BlockSpec((None, n_pairs, GROUP_W, PAIR), lambda i, j: (i, 0, 0, 0)),
                   pl.BlockSpec((None, n_pairs, N_GATES, PAIR), lambda i, j: (i, 0, 0, 0))],
        out_shape=[jax.ShapeDtypeStruct((b, lx, N_GROUPS * GROUP_W), BF16),
                   jax.ShapeDtypeStruct((b, n_pairs, GROUP_W, PAIR), BF16),
                   jax.ShapeDtypeStruct((b, n_pairs, N_GATES, PAIR), F32)],
        scratch_shapes=[pltpu.VMEM((lx, d), BF16),
                        pltpu.VMEM((d, GROUP_W), BF16),
                        pltpu.VMEM((lx + 16, GROUP_W), F32)],
        compiler_params=_params(48, ("arbitrary", "arbitrary")),
        name="inproj",
    )(x, g, shift, scale, w_in, wabt, conv, alog, dtb)


def _bd2(a, b):
    z = jnp.zeros_like(a)
    return jnp.concatenate([jnp.concatenate([a, z], axis=1),
                            jnp.concatenate([z, b], axis=1)], axis=0)


def _gdn_scan(q_ref, k_ref, v_ref, kt_ref, gat_ref, n_pairs, write_o,
              u_s, wq_s, at_s, kdt_s, egl_s, s_s, o_s):
    ri = lax.broadcasted_iota(jnp.int32, (PAIR, PAIR), 0)
    ci = lax.broadcasted_iota(jnp.int32, (PAIR, PAIR), 1)
    same = (ri // CHUNK) == (ci // CHUNK)
    incl = (jnp.where(same & (ri >= ci), 1.0, 0.0), jnp.where(same & (ri <= ci), 1.0, 0.0))
    strict = (jnp.where(same & (ri > ci), 1.0, 0.0), jnp.where(same & (ri < ci), 1.0, 0.0))
    tri_u = incl[1]
    tri_l = incl[0]
    eye = jnp.where(ri == ci, 1.0, 0.0)
    eye2 = jnp.concatenate([eye, eye], axis=1)
    lane = lax.broadcasted_iota(jnp.int32, (1, PAIR), 1)
    first_half = lane < CHUNK
    scale = HEAD_D ** -0.5
    hi = lax.Precision.HIGHEST

    def phase_a(p, carry):
        t0 = pl.multiple_of(p * PAIR, PAIR)
        gates = gat_ref[p]
        gcs = (jnp.dot(gates, tri_u, precision=hi, preferred_element_type=F32),
               jnp.dot(gates, tri_l, precision=hi, preferred_element_type=F32))
        gcs_t = (gcs[0].T, gcs[1].T)
        kt_all = kt_ref[p]
        for hp in range(HEADS // 2):
            lanes = slice(hp * 2 * HEAD_D, (hp + 1) * 2 * HEAD_D)
            k2, q2, v2, kt2, kk2, qk2 = [], [], [], [], [], []
            for hh in range(2):
                h = 2 * hp + hh
                hl = slice(h * HEAD_D, (h + 1) * HEAD_D)
                k = k_ref[pl.ds(t0, PAIR), hl]
                q = q_ref[pl.ds(t0, PAIR), hl]
                kt = kt_all[hl, :]
                k2.append(k)
                q2.append(q)
                v2.append(v_ref[pl.ds(t0, PAIR), hl])
                kt2.append(kt)
                kk2.append(_dot(k, kt))
                qk2.append(_dot(q, kt))
            bd_k = _bd2(k2[0], k2[1])
            bd_v = _bd2(v2[0], v2[1])
            for d in range(2):
                ps, attn, qd, kdt, erow = [], [], [], [], []
                for hh in range(2):
                    h = 2 * hp + hh
                    brow = gates[4 * d + h:4 * d + h + 1, :]
                    gcr = gcs[d][8 + 4 * d + h:9 + 4 * d + h, :]
                    gcc = gcs_t[d][:, 8 + 4 * d + h:9 + 4 * d + h]
                    diff = (gcc - gcr) * incl[d]
                    decay = jnp.exp(diff) * incl[d]
                    ps.append(-(kk2[hh] * decay * strict[d] * brow))
                    attn.append(qk2[hh] * decay * (brow * scale))
                    qd.append(q2[hh].astype(F32) * (jnp.exp(gcc) * scale))
                    if d == 0:
                        gl0, gl1 = gcr[:, CHUNK - 1:CHUNK], gcr[:, PAIR - 1:PAIR]
                    else:
                        gl0, gl1 = gcr[:, 0:1], gcr[:, CHUNK:CHUNK + 1]
                    glr = jnp.where(first_half, gl0, gl1)
                    kdt.append(kt2[hh].astype(F32) * (jnp.exp(glr - gcr) * brow))
                    erow.append(jnp.exp(gcr))
                    egl_s[d, 2 * p, h:h + 1, :] = jnp.broadcast_to(jnp.exp(gl0), (1, PAIR))
                    egl_s[d, 2 * p + 1, h:h + 1, :] = jnp.broadcast_to(jnp.exp(gl1), (1, PAIR))
                pw = jnp.concatenate(ps, axis=1)
                r = pw
                for _ in range(5):
                    pb = pw.astype(BF16)
                    pw = _dot(pb, _bd2(pb[:, :HEAD_D], pb[:, HEAD_D:]))
                    pb = pw.astype(BF16)
                    r = r + pw + _dot(r.astype(BF16), _bd2(pb[:, :HEAD_D], pb[:, HEAD_D:]))
                t = eye2 + r
                u = _dot(t.astype(BF16), bd_v)
                w = _dot((t * jnp.concatenate(erow, axis=1)).astype(BF16), bd_k)
                u_s[d, pl.ds(t0, PAIR), lanes] = u.astype(BF16)
                wb = w.astype(BF16)
                qdb = jnp.concatenate(qd, axis=1).astype(BF16)
                base = pl.multiple_of(2 * t0, 2 * PAIR)
                wq_s[d, pl.ds(base, 2 * PAIR), lanes] = jnp.concatenate(
                    [wb[:CHUNK], qdb[:CHUNK], wb[CHUNK:], qdb[CHUNK:]], axis=0)
                at_s[d, pl.ds(t0, PAIR), lanes] = jnp.concatenate(attn, axis=1).astype(BF16)
                kdt_s[d, 2 * p + hp] = jnp.concatenate(kdt, axis=1).astype(BF16)
        return carry

    lax.fori_loop(0, n_pairs, phase_a, 0)

    zeros_half = jnp.zeros((CHUNK, 2 * HEAD_D), BF16)

    def phase_b(ip, carry):
        for hp in range(HEADS // 2):
            lanes = slice(hp * 2 * HEAD_D, (hp + 1) * 2 * HEAD_D)
            for d in range(2):
                pp = ip if d == 0 else n_pairs - 1 - ip
                t0 = pl.multiple_of(pp * PAIR, PAIR)
                s = s_s[d, :, lanes]
                for e in ((0, 1) if d == 0 else (1, 0)):
                    r0 = pl.multiple_of(t0 + e * CHUNK, CHUNK)
                    base = pl.multiple_of(2 * t0 + e * PAIR, PAIR)
                    sb = s.astype(BF16)
                    res = _dot(wq_s[d, pl.ds(base, PAIR), lanes],
                               _bd2(sb[:, :HEAD_D], sb[:, HEAD_D:]))
                    vn = u_s[d, pl.ds(r0, CHUNK), lanes].astype(F32) - res[:CHUNK]
                    vnb = vn.astype(BF16)
                    x = jnp.concatenate([vnb, zeros_half] if e == 0 else [zeros_half, vnb], axis=0)
                    bd_x = _bd2(x[:, :HEAD_D], x[:, HEAD_D:])
                    if write_o:
                        o = res[CHUNK:] + _dot(at_s[d, pl.ds(r0, CHUNK), lanes], bd_x)
                        o_s[pl.ds(r0, CHUNK), lanes] += o
                    c = 2 * pp + e
                    eg = jnp.concatenate([egl_s[d, c, 2 * hp:2 * hp + 1, :],
                                          egl_s[d, c, 2 * hp + 1:2 * hp + 2, :]], axis=1)
                    s = s * eg + _dot(kdt_s[d, 2 * pp + hp], bd_x)
                s_s[d, :, lanes] = s
        return carry

    lax.fori_loop(0, n_pairs, phase_b, 0)


def _gdn_kernel(q_ref, k_ref, v_ref, z_ref, kt_ref, gat_ref,
                qc_ref, kc_ref, vc_ref, ktc_ref, gatc_ref, g_ref,
                y_ref, u_s, wq_s, at_s, kdt_s, egl_s, s_s, o_s):
    l = q_ref.shape[0]
    lc = qc_ref.shape[0]
    s_s[...] = jnp.zeros(s_s.shape, F32)
    o_s[...] = jnp.zeros(o_s.shape, F32)
    scratch = (u_s, wq_s, at_s, kdt_s, egl_s, s_s, o_s)
    _gdn_scan(qc_ref, kc_ref, vc_ref, ktc_ref, gatc_ref, lc // PAIR, False, *scratch)
    _gdn_scan(q_ref, k_ref, v_ref, kt_ref, gat_ref, l // PAIR, True, *scratch)

    def epilogue(rb, carry):
        r0 = pl.multiple_of(rb * ROW_BLOCK, ROW_BLOCK)
        o = o_s[pl.ds(r0, ROW_BLOCK), :]
        segs = [_rms(o[:, h * HEAD_D:(h + 1) * HEAD_D]) * g_ref[...] for h in range(HEADS)]
        y = jnp.concatenate(segs, axis=1) * z_ref[pl.ds(r0, ROW_BLOCK), :].astype(F32)
        y_ref[pl.ds(r0, ROW_BLOCK), :] = y.astype(BF16)
        return carry

    lax.fori_loop(0, l // ROW_BLOCK, epilogue, 0)


def _gdn(pg, kt, gat, pgc, ktc, gatc, g_gdn):
    b, l, _ = pg.shape
    lc = pgc.shape[1]
    n_pairs, n_pairs_c = l // PAIR, lc // PAIR

    def group(length, g):
        return pl.BlockSpec((None, length, GROUP_W), lambda i, g=g: (i, 0, g))

    return pl.pallas_call(
        _gdn_kernel,
        grid=(b,),
        in_specs=[group(l, 1), group(l, 2), group(l, 3), group(l, 4),
                  pl.BlockSpec((None, n_pairs, GROUP_W, PAIR), lambda i: (i, 0, 0, 0)),
                  pl.BlockSpec((None, n_pairs, N_GATES, PAIR), lambda i: (i, 0, 0, 0)),
                  group(lc, 1), group(lc, 2), group(lc, 3),
                  pl.BlockSpec((None, n_pairs_c, GROUP_W, PAIR), lambda i: (i, 0, 0, 0)),
                  pl.BlockSpec((None, n_pairs_c, N_GATES, PAIR), lambda i: (i, 0, 0, 0)),
                  pl.BlockSpec((1, HEAD_D), lambda i: (0, 0))],
        out_specs=pl.BlockSpec((None, l, GROUP_W), lambda i: (i, 0, 0)),
        out_shape=jax.ShapeDtypeStruct((b, l, GROUP_W), BF16),
        scratch_shapes=[pltpu.VMEM((2, l, GROUP_W), BF16),
                        pltpu.VMEM((2, 2 * l, GROUP_W), BF16),
                        pltpu.VMEM((2, l, GROUP_W), BF16),
                        pltpu.VMEM((2, 2 * n_pairs, HEAD_D, 2 * PAIR), BF16),
                        pltpu.VMEM((2, 2 * n_pairs, 8, PAIR), F32),
                        pltpu.VMEM((2, HEAD_D, GROUP_W), F32),
                        pltpu.VMEM((l, GROUP_W), F32)],
        compiler_params=_params(58, ("arbitrary",)),
        name="gdn",
    )(pg, pg, pg, pg, kt, gat, pgc, pgc, pgc, ktc, gatc, g_gdn)


def _fourier_w_kernel(cc_ref, sc_ref, w_ref, o_ref):
    hi = lax.Precision.HIGHEST
    for g in range(w_ref.shape[0]):
        w = w_ref[g]
        o_ref[g] = jnp.concatenate(
            [jnp.dot(cc_ref[...], w, precision=hi, preferred_element_type=F32),
             -jnp.dot(sc_ref[...], w, precision=hi, preferred_element_type=F32)], axis=1)


def _fourier_w(cc, sc, w_f):
    groups, c, d = w_f.shape
    return pl.pallas_call(
        _fourier_w_kernel,
        out_shape=jax.ShapeDtypeStruct((groups, c, 2 * d), F32),
        name="fourier_w",
    )(cc, sc, w_f)


@functools.lru_cache(maxsize=None)
def _dft_constants(length, channels):
    def cos_sin(n):
        idx = np.arange(n, dtype=np.int64)
        ang = (np.outer(idx, idx) % n).astype(np.float64) * (2.0 * np.pi / n)
        return np.cos(ang), np.sin(ang)
    c_l, s_l = cos_sin(length)
    c_c, s_c = cos_sin(channels)
    norm = 1.0 / math.sqrt(length * channels)
    cs = np.concatenate([c_l, s_l], axis=1).astype(np.float32)
    return (jnp.asarray(cs),
            jnp.asarray((c_c * norm).astype(np.float32)),
            jnp.asarray((s_c * norm).astype(np.float32)))


def _mix_kernel(u_ref, yd_ref, cs_ref, mcs_ref, wout_ref, x_ref, gpost_ref, gt_ref,
                gpre_ref, sh_ref, sc_ref, x1_ref, h2_ref, v_s, wout_s):
    m = pl.program_id(1)
    l = u_ref.shape[0]

    @pl.when(m == 0)
    def _():
        wout_s[...] = wout_ref[...].astype(BF16)
        for g in range(HEADS):
            mg = mcs_ref[g].astype(BF16)

            def body(rb, carry, g=g, mg=mg):
                r0 = pl.multiple_of(rb * ROW_BLOCK, ROW_BLOCK)
                vv = _dot(u_ref[pl.ds(r0, ROW_BLOCK), g * HEAD_D:(g + 1) * HEAD_D], mg)
                v_s[pl.ds(r0, ROW_BLOCK), g * HEAD_D:(g + 1) * HEAD_D] = vv[:, :HEAD_D].astype(BF16)
                v_s[pl.ds(r0 + l, ROW_BLOCK), g * HEAD_D:(g + 1) * HEAD_D] = vv[:, HEAD_D:].astype(BF16)
                return carry

            lax.fori_loop(0, l // ROW_BLOCK, body, 0)

    yf = _dot(cs_ref[...].astype(BF16), v_s[...])
    half = wout_s.shape[0] // 2
    y = _dot(yf.astype(BF16), wout_s[:half, :]) + _dot(yd_ref[...], wout_s[half:, :])
    x1 = x_ref[...] + gt_ref[...] * (_rms(y) * gpost_ref[...])
    x1_ref[...] = x1
    h2 = _rms(x1) * gpre_ref[...]
    h2_ref[...] = (h2 * (1.0 + sc_ref[...]) + sh_ref[...]).astype(BF16)


def _mix(pg, yd, cs, mcs, w_out, x, g_post, gt, g_pre, shift, scale):
    b, l, d = x.shape
    tm = MIX_TILE
    vec = pl.BlockSpec((1, d), lambda i, m: (0, 0))
    mod = pl.BlockSpec((None, 1, d), lambda i, m: (i, 0, 0))
    return pl.pallas_call(
        _mix_kernel,
        grid=(b, l // tm),
        in_specs=[pl.BlockSpec((None, l, GROUP_W), lambda i, m: (i, 0, 0)),
                  pl.BlockSpec((None, tm, GROUP_W), lambda i, m: (i, m, 0)),
                  pl.BlockSpec((tm, 2 * l), lambda i, m: (m, 0)),
                  pl.BlockSpec(mcs.shape, lambda i, m: (0, 0, 0)),
                  pl.BlockSpec(w_out.shape, lambda i, m: (0, 0)),
                  pl.BlockSpec((None, tm, d), lambda i, m: (i, m, 0)),
                  vec, mod, vec, mod, mod],
        out_specs=[pl.BlockSpec((None, tm, d), lambda i, m: (i, m, 0)),
                   pl.BlockSpec((None, tm, d), lambda i, m: (i, m, 0))],
        out_shape=[jax.ShapeDtypeStruct((b, l, d), F32),
                   jax.ShapeDtypeStruct((b, l, d), BF16)],
        scratch_shapes=[pltpu.VMEM((2 * l, GROUP_W), BF16),
                        pltpu.VMEM(w_out.shape, BF16)],
        compiler_params=_params(48, ("arbitrary", "arbitrary")),
        name="mix",
    )(pg, yd, cs, mcs, w_out, x, g_post, gt, g_pre, shift, scale)


def _ffn_kernel(h_ref, x1_ref, wv_ref, wg_ref, dwc_ref, wd_ref, gpost_ref, gt_ref,
                o_ref, gate_s, wv_s, wg_s, wd_s):
    j = pl.program_id(1)
    l = h_ref.shape[0]
    tf = wv_ref.shape[1]
    n_rb = l // ROW_BLOCK
    pad = GRID_W

    wv_s[...] = wv_ref[...].astype(BF16)
    wg_s[...] = wg_ref[...].astype(BF16)
    wd_s[...] = wd_ref[...].astype(BF16)
    gate_s[0:pad, :] = jnp.zeros((pad, tf), F32)
    gate_s[l + pad:l + 2 * pad, :] = jnp.zeros((pad, tf), F32)

    @pl.when(j == 0)
    def _():
        o_ref[...] = jnp.zeros(o_ref.shape, F32)

    def gate_body(rb, carry):
        r0 = pl.multiple_of(rb * ROW_BLOCK, ROW_BLOCK)
        gate_s[pl.ds(r0 + pad, ROW_BLOCK), :] = _dot(h_ref[pl.ds(r0, ROW_BLOCK), :], wg_s[...])
        return carry

    lax.fori_loop(0, n_rb, gate_body, 0)

    n_ext = ROW_BLOCK + 2 * pad
    col = lax.broadcasted_iota(jnp.int32, (n_ext, tf), 0) % GRID_W
    not_first = jnp.where(col == 0, 0.0, 1.0)
    not_last = jnp.where(col == GRID_W - 1, 0.0, 1.0)
    dw = dwc_ref[...]

    def body(rb, carry):
        r0 = pl.multiple_of(rb * ROW_BLOCK, ROW_BLOCK)
        ext = gate_s[pl.ds(r0, n_ext), :]
        shifted = (pltpu.roll(ext, 1, 0) * not_first, ext, pltpu.roll(ext, n_ext - 1, 0) * not_last)
        conv = None
        for dy in range(3):
            for dx in range(3):
                term = shifted[dx][dy * pad:dy * pad + ROW_BLOCK] * dw[3 * dy + dx:3 * dy + dx + 1]
                conv = term if conv is None else conv + term
        val = _dot(h_ref[pl.ds(r0, ROW_BLOCK), :], wv_s[...])
        act = (_silu(conv) * val).astype(BF16)
        o_ref[pl.ds(r0, ROW_BLOCK), :] += _dot(act, wd_s[...])
        return carry

    lax.fori_loop(0, n_rb, body, 0)

    @pl.when(j == pl.num_programs(1) - 1)
    def _():
        def fin(rb, carry):
            r0 = pl.multiple_of(rb * ROW_BLOCK, ROW_BLOCK)
            y = _rms(o_ref[pl.ds(r0, ROW_BLOCK), :]) * gpost_ref[...]
            o_ref[pl.ds(r0, ROW_BLOCK), :] = x1_ref[pl.ds(r0, ROW_BLOCK), :] + gt_ref[...] * y
            return carry
        lax.fori_loop(0, n_rb, fin, 0)


def _ffn(h2, x1, w_up, dwc, w_down, g_post, gt):
    b, l, d = x1.shape
    d_ff = w_down.shape[0]
    tf = FFN_TILE
    n_f = d_ff // tf
    return pl.pallas_call(
        _ffn_kernel,
        grid=(b, n_f),
        in_specs=[pl.BlockSpec((None, l, d), lambda i, j: (i, 0, 0)),
                  pl.BlockSpec((None, l, d), lambda i, j: (i, 0, 0)),
                  pl.BlockSpec((d, tf), lambda i, j: (0, j)),
                  pl.BlockSpec((d, tf), lambda i, j: (0, n_f + j)),
                  pl.BlockSpec((9, tf), lambda i, j: (0, j)),
                  pl.BlockSpec((tf, d), lambda i, j: (j, 0)),
                  pl.BlockSpec((1, d), lambda i, j: (0, 0)),
                  pl.BlockSpec((None, 1, d), lambda i, j: (i, 0, 0))],
        out_specs=pl.BlockSpec((None, l, d), lambda i, j: (i, 0, 0)),
        out_shape=jax.ShapeDtypeStruct((b, l, d), F32),
        scratch_shapes=[pltpu.VMEM((l + 2 * GRID_W, tf), F32),
                        pltpu.VMEM((d, tf), BF16),
                        pltpu.VMEM((d, tf), BF16),
                        pltpu.VMEM((tf, d), BF16)],
        compiler_params=_params(58, ("arbitrary", "arbitrary")),
        name="ffn",
    )(h2, x1, w_up, w_up, dwc, w_down, g_post, gt)


def kernel(x, c, ctx, c_ctx, w_ada, b_ada, g_pre_mix, g_post_mix, g_pre_ffn, g_post_ffn,
           w_in, w_qkv_conv, a_log, dt_bias, g_gdn, w_fourier, w_out, w_up, w_dwc, w_down):
    b, l, d = x.shape
    assert w_ada.shape[0] == 1, "single-layer stack"
    assert b < 16 and l % ROW_BLOCK == 0 and ctx.shape[1] % ROW_BLOCK == 0
    assert w_in.shape[2] == N_GROUPS * GROUP_W + N_GATES and d == 2 * GROUP_W

    cc = jnp.zeros((16, d), F32).at[:b].set(c).at[b].set(c_ctx)
    mod = _ada(cc, w_ada[0], b_ada)
    sh1, sc1, gt1, sh2, sc2, gt2 = [m[:, None, :] for m in jnp.split(mod[:b], 6, axis=-1)]
    sh1c, sc1c = [jnp.broadcast_to(m[None, None, :], (b, 1, d))
                  for m in jnp.split(mod[b], 6, axis=-1)[:2]]

    wabt = w_in[0][:, N_GROUPS * GROUP_W:].T
    zeros8 = jnp.zeros((8,), F32)
    alog = jnp.concatenate([zeros8, a_log[0].reshape(-1)])[:, None]
    dtb = jnp.concatenate([zeros8, dt_bias[0].reshape(-1)])[:, None]

    pg, kt, gat = _inproj(x, g_pre_mix, sh1, sc1, w_in[0], wabt, w_qkv_conv[0], alog, dtb)
    pgc, ktc, gatc = _inproj(ctx, g_pre_mix, sh1c, sc1c, w_in[0], wabt, w_qkv_conv[0], alog, dtb)
    yd = _gdn(pg, kt, gat, pgc, ktc, gatc, g_gdn)

    cs, c_c, s_c = _dft_constants(l, HEAD_D)
    mcs = _fourier_w(c_c, s_c, w_fourier[0])
    x1, h2 = _mix(pg, yd, cs, mcs, w_out[0], x, g_post_mix, gt1, g_pre_ffn, sh2, sc2)
    d_ff = w_down.shape[1]
    return _ffn(h2, x1, w_up[0], w_dwc[0].reshape(9, d_ff), w_down[0], g_post_ffn, gt2)
```

```python
import functools
import math

import numpy as np
import jax
import jax.numpy as jnp
from jax import lax
from jax.experimental import pallas as pl
from jax.experimental.pallas import tpu as pltpu

F32 = jnp.float32
BF16 = jnp.bfloat16

GRID_W = 64
HEADS = 4
HEAD_D = 128
CHUNK = 64
PAIR = 2 * CHUNK
GROUP_W = HEADS * HEAD_D
N_GROUPS = 5
N_GATES = 16
EPS = 1e-6
ROW_BLOCK = 256
FFN_TILE = 256
MIX_TILE = 256
PAIRS_PER_STEP = 2
MIB = 1024 * 1024

_NT = (((1,), (1,)), ((), ()))


def _dot(a, b):
    return jnp.dot(a, b, preferred_element_type=F32)


def _silu(x):
    return x / (1.0 + jnp.exp(-x))


def _rms(x):
    return x * lax.rsqrt(jnp.mean(x * x, axis=-1, keepdims=True) + EPS)


def _params(vmem_mib, semantics):
    return pltpu.CompilerParams(dimension_semantics=semantics,
                                vmem_limit_bytes=vmem_mib * MIB)


def _ada_kernel(c_ref, w_ref, b_ref, o_ref):
    s = _silu(c_ref[...]).astype(BF16)
    o_ref[...] = _dot(s, w_ref[...].astype(BF16)) + b_ref[...]


def _ada(cc, w, b):
    rows, d = cc.shape
    n = w.shape[1]
    tn = 1024
    return pl.pallas_call(
        _ada_kernel,
        grid=(n // tn,),
        in_specs=[pl.BlockSpec((rows, d), lambda j: (0, 0)),
                  pl.BlockSpec((d, tn), lambda j: (0, j)),
                  pl.BlockSpec((1, tn), lambda j: (0, j))],
        out_specs=pl.BlockSpec((rows, tn), lambda j: (0, j)),
        out_shape=jax.ShapeDtypeStruct((rows, n), F32),
        compiler_params=_params(32, ("arbitrary",)),
        name="ada",
    )(cc, w, b)


def _inproj_kernel(x_ref, g_ref, sh_ref, sc_ref, w_ref, wabt_ref, conv_ref, alog_ref, dtb_ref,
                   pg_ref, kt_ref, gat_ref, h_s, w_s, p_s):
    j = pl.program_id(1)
    lx = x_ref.shape[0]
    n_rb = lx // ROW_BLOCK

    w_s[...] = w_ref[...].astype(BF16)

    @pl.when(j == 0)
    def _prologue():
        p_s[0:8, :] = jnp.zeros((8, GROUP_W), F32)
        p_s[lx + 8:lx + 16, :] = jnp.zeros((8, GROUP_W), F32)
        wabt = wabt_ref[...].astype(BF16)
        row = lax.broadcasted_iota(jnp.int32, (N_GATES, PAIR), 0)

        def body(rb, carry):
            r0 = pl.multiple_of(rb * ROW_BLOCK, ROW_BLOCK)
            x = x_ref[pl.ds(r0, ROW_BLOCK), :]
            h = _rms(x) * g_ref[...]
            h = h * (1.0 + sc_ref[...]) + sh_ref[...]
            hb = h.astype(BF16)
            h_s[pl.ds(r0, ROW_BLOCK), :] = hb
            for s in range(ROW_BLOCK // PAIR):
                ab = lax.dot_general(wabt, hb[s * PAIR:(s + 1) * PAIR], _NT,
                                     preferred_element_type=F32)
                beta = 1.0 / (1.0 + jnp.exp(-ab))
                xs = ab + dtb_ref[...]
                softplus = jnp.maximum(xs, 0.0) + jnp.log1p(jnp.exp(-jnp.abs(xs)))
                gate = -jnp.exp(alog_ref[...]) * softplus
                gat_ref[rb * (ROW_BLOCK // PAIR) + s] = jnp.where(row < 8, beta, gate)
            return carry

        lax.fori_loop(0, n_rb, body, 0)

    def matmul_loop(store):
        def body(rb, carry):
            r0 = pl.multiple_of(rb * ROW_BLOCK, ROW_BLOCK)
            store(r0, _dot(h_s[pl.ds(r0, ROW_BLOCK), :], w_s[...]))
            return carry
        lax.fori_loop(0, n_rb, body, 0)

    @pl.when(j == 0)
    def _u():
        def store(r0, p):
            pg_ref[pl.ds(r0, ROW_BLOCK), :] = p.astype(BF16)
        matmul_loop(store)

    @pl.when(j == N_GROUPS - 1)
    def _z():
        def store(r0, p):
            pg_ref[pl.ds(r0, ROW_BLOCK), :] = _silu(p).astype(BF16)
        matmul_loop(store)

    def conv_group(normalise, transpose):
        def store(r0, p):
            p_s[pl.ds(r0 + 8, ROW_BLOCK), :] = p
        matmul_loop(store)
        cw = conv_ref[...]
        n_ext = ROW_BLOCK + 16

        def body(rb, carry):
            r0 = pl.multiple_of(rb * ROW_BLOCK, ROW_BLOCK)
            ext = p_s[pl.ds(r0, n_ext), :]
            prev = pltpu.roll(ext, 1, 0)[8:8 + ROW_BLOCK]
            nxt = pltpu.roll(ext, n_ext - 1, 0)[8:8 + ROW_BLOCK]
            cur = ext[8:8 + ROW_BLOCK]
            a = _silu(prev * cw[0:1] + cur * cw[1:2] + nxt * cw[2:3])
            if normalise:
                segs = []
                for hh in range(HEADS):
                    seg = a[:, hh * HEAD_D:(hh + 1) * HEAD_D]
                    segs.append(seg * lax.rsqrt(jnp.sum(seg * seg, axis=-1, keepdims=True) + EPS))
                a = jnp.concatenate(segs, axis=1)
            pg_ref[pl.ds(r0, ROW_BLOCK), :] = a.astype(BF16)
            if transpose:
                at = a.T.astype(BF16)
                for s in range(ROW_BLOCK // PAIR):
                    kt_ref[rb * (ROW_BLOCK // PAIR) + s] = at[:, s * PAIR:(s + 1) * PAIR]
            return carry

        lax.fori_loop(0, n_rb, body, 0)

    @pl.when(j == 1)
    def _q():
        conv_group(True, False)

    @pl.when(j == 2)
    def _k():
        conv_group(True, True)

    @pl.when(j == 3)
    def _v():
        conv_group(False, False)


def _inproj(x, g, shift, scale, w_in, wabt, conv, alog, dtb):
    b, lx, d = x.shape
    n_pairs = lx // PAIR
    return pl.pallas_call(
        _inproj_kernel,
        grid=(b, N_GROUPS),
        in_specs=[pl.BlockSpec((None, lx, d), lambda i, j: (i, 0, 0)),
                  pl.BlockSpec((1, d), lambda i, j: (0, 0)),
                  pl.BlockSpec((None, 1, d), lambda i, j: (i, 0, 0)),
                  pl.BlockSpec((None, 1, d), lambda i, j: (i, 0, 0)),
                  pl.BlockSpec((d, GROUP_W), lambda i, j: (0, j)),
                  pl.BlockSpec((N_GATES, d), lambda i, j: (0, 0)),
                  pl.BlockSpec((3, GROUP_W), lambda i, j: (0, jnp.clip(j - 1, 0, 2))),
                  pl.BlockSpec((N_GATES, 1), lambda i, j: (0, 0)),
                  pl.BlockSpec((N_GATES, 1), lambda i, j: (0, 0))],
        out_specs=[pl.BlockSpec((None, lx, GROUP_W), lambda i, j: (i, 0, j)),
                   pl.BlockSpec((None, n_pairs, GROUP_W, PAIR), lambda i, j: (i, 0, 0, 0)),
                   pl.BlockSpec((None, n_pairs, N_GATES, PAIR), lambda i, j: (i, 0, 0, 0))],
        out_shape=[jax.ShapeDtypeStruct((b, lx, N_GROUPS * GROUP_W), BF16),
                   jax.ShapeDtypeStruct((b, n_pairs, GROUP_W, PAIR), BF16),
                   jax.ShapeDtypeStruct((b, n_pairs, N_GATES, PAIR), F32)],
        scratch_shapes=[pltpu.VMEM((lx, d), BF16),
                        pltpu.VMEM((d, GROUP_W), BF16),
                        pltpu.VMEM((lx + 16, GROUP_W), F32)],
        compiler_params=_params(48, ("arbitrary", "arbitrary")),
        name="inproj",
    )(x, g, shift, scale, w_in, wabt, conv, alog, dtb)


def _bd2(a, b):
    z = jnp.zeros_like(a)
    return jnp.concatenate([jnp.concatenate([a, z], axis=1),
                            jnp.concatenate([z, b], axis=1)], axis=0)


def _gdn_scan(q_ref, k_ref, v_ref, kt_ref, gat_ref, n_pairs, write_o,
              u_s, wq_s, at_s, kdt_s, egl_s, s_s, o_s):
    ri = lax.broadcasted_iota(jnp.int32, (PAIR, PAIR), 0)
    ci = lax.broadcasted_iota(jnp.int32, (PAIR, PAIR), 1)
    same = (ri // CHUNK) == (ci // CHUNK)
    incl = (jnp.where(same & (ri >= ci), 1.0, 0.0), jnp.where(same & (ri <= ci), 1.0, 0.0))
    strict = (jnp.where(same & (ri > ci), 1.0, 0.0), jnp.where(same & (ri < ci), 1.0, 0.0))
    tri_u = incl[1]
    tri_l = incl[0]
    eye = jnp.where(ri == ci, 1.0, 0.0)
    eye2 = jnp.concatenate([eye, eye], axis=1)
    off = ([], [])
    for lvl in range(6):
        b = 2 ** lvl
        joined = (ri // (2 * b)) == (ci // (2 * b))
        off[0].append(jnp.where(joined & (ri % (2 * b) >= b) & (ci % (2 * b) < b), 1.0, 0.0))
        off[1].append(jnp.where(joined & (ri % (2 * b) < b) & (ci % (2 * b) >= b), 1.0, 0.0))

    def joining_blocks(a, d, lvl):
        m = off[d][lvl]
        return jnp.concatenate([a[:, :HEAD_D] * m, a[:, HEAD_D:] * m], axis=1)
    lane = lax.broadcasted_iota(jnp.int32, (1, PAIR), 1)
    first_half = lane < CHUNK
    scale = HEAD_D ** -0.5
    hi = lax.Precision.HIGHEST

    def md_and_friends(p, gates, gcs, gcs_t, kk, qk, q4, kt4, hp, d):
        ps, attn, qd, kdt, erow = [], [], [], [], []
        for hh in range(2):
            h = 2 * hp + hh
            brow = gates[4 * d + h:4 * d + h + 1, :]
            gcr = gcs[d][8 + 4 * d + h:9 + 4 * d + h, :]
            gcc = gcs_t[d][:, 8 + 4 * d + h:9 + 4 * d + h]
            diff = (gcc - gcr) * incl[d]
            decay = jnp.exp(diff) * incl[d]
            ps.append(kk[h] * decay * strict[d] * brow)
            attn.append(qk[h] * decay * (brow * scale))
            qd.append(q4[h].astype(F32) * (jnp.exp(gcc) * scale))
            if d == 0:
                gl0, gl1 = gcr[:, CHUNK - 1:CHUNK], gcr[:, PAIR - 1:PAIR]
            else:
                gl0, gl1 = gcr[:, 0:1], gcr[:, CHUNK:CHUNK + 1]
            glr = jnp.where(first_half, gl0, gl1)
            kdt.append(kt4[h].astype(F32) * (jnp.exp(glr - gcr) * brow))
            erow.append(jnp.exp(gcr))
            egl_s[d, 2 * p, h:h + 1, :] = jnp.broadcast_to(jnp.exp(gl0), (1, PAIR))
            egl_s[d, 2 * p + 1, h:h + 1, :] = jnp.broadcast_to(jnp.exp(gl1), (1, PAIR))
        return (jnp.concatenate(ps, axis=1), jnp.concatenate(attn, axis=1).astype(BF16),
                jnp.concatenate(qd, axis=1).astype(BF16), jnp.concatenate(kdt, axis=1).astype(BF16),
                jnp.concatenate(erow, axis=1))

    def bd_of(x):
        xb = x.astype(BF16)
        return xb, _bd2(xb[:, :HEAD_D], xb[:, HEAD_D:])

    def phase_a(it, carry):
        loaded = []
        for pi in range(PAIRS_PER_STEP):
            p = it * PAIRS_PER_STEP + pi
            t0 = pl.multiple_of(p * PAIR, PAIR)
            gates = gat_ref[p]
            kt_all = kt_ref[p]
            heads = [slice(h * HEAD_D, (h + 1) * HEAD_D) for h in range(HEADS)]
            k4 = [k_ref[pl.ds(t0, PAIR), hl] for hl in heads]
            q4 = [q_ref[pl.ds(t0, PAIR), hl] for hl in heads]
            v4 = [v_ref[pl.ds(t0, PAIR), hl] for hl in heads]
            kt4 = [kt_all[hl, :] for hl in heads]
            kk = [_dot(k4[h], kt4[h]) for h in range(HEADS)]
            qk = [_dot(q4[h], kt4[h]) for h in range(HEADS)]
            gcs = (jnp.dot(gates, tri_u, precision=hi, preferred_element_type=F32),
                   jnp.dot(gates, tri_l, precision=hi, preferred_element_type=F32))
            loaded.append((p, t0, gates, gcs, (gcs[0].T, gcs[1].T), k4, q4, v4, kt4, kk, qk))

        streams = []
        for (p, t0, gates, gcs, gcs_t, k4, q4, v4, kt4, kk, qk) in loaded:
            for hp in range(HEADS // 2):
                bd_k = _bd2(k4[2 * hp], k4[2 * hp + 1])
                bd_v = _bd2(v4[2 * hp], v4[2 * hp + 1])
                for d in range(2):
                    a, attn, qdb, kdtb, erow = md_and_friends(
                        p, gates, gcs, gcs_t, kk, qk, q4, kt4, hp, d)
                    streams.append(dict(p=p, t0=t0, hp=hp, d=d, a=a, attn=attn, qd=qdb,
                                        kdt=kdtb, erow=erow, bd_k=bd_k, bd_v=bd_v))

        for st in streams:
            st["t"] = eye2 - joining_blocks(st["a"], st["d"], 0)
        for lvl in range(1, 6):
            for st in streams:
                _, bd_t = bd_of(st["t"])
                st["y"] = _dot(joining_blocks(st["a"], st["d"], lvl).astype(BF16), bd_t)
            for st in streams:
                _, bd_y = bd_of(st["y"])
                st["t"] = st["t"] - _dot(st["t"].astype(BF16), bd_y)
        for st in streams:
            t = st["t"]
            st["u"] = _dot(t.astype(BF16), st["bd_v"])
            st["w"] = _dot((t * st["erow"]).astype(BF16), st["bd_k"])
        for st in streams:
            d, hp, t0, p = st["d"], st["hp"], st["t0"], st["p"]
            lanes = slice(hp * 2 * HEAD_D, (hp + 1) * 2 * HEAD_D)
            u_s[d, pl.ds(t0, PAIR), lanes] = st["u"].astype(BF16)
            wb, qdb = st["w"].astype(BF16), st["qd"]
            base = pl.multiple_of(2 * t0, 2 * PAIR)
            wq_s[d, pl.ds(base, 2 * PAIR), lanes] = jnp.concatenate(
                [wb[:CHUNK], qdb[:CHUNK], wb[CHUNK:], qdb[CHUNK:]], axis=0)
            at_s[d, pl.ds(t0, PAIR), lanes] = st["attn"]
            kdt_s[d, 2 * p + hp] = st["kdt"]
        return carry

    lax.fori_loop(0, n_pairs // PAIRS_PER_STEP, phase_a, 0)

    zeros_half = jnp.zeros((CHUNK, 2 * HEAD_D), BF16)
    stream_ids = [(hp, d) for hp in range(HEADS // 2) for d in range(2)]

    def phase_b(ip, carry):
        lanes = {hp: slice(hp * 2 * HEAD_D, (hp + 1) * 2 * HEAD_D) for hp in range(HEADS // 2)}
        pp = {0: ip, 1: n_pairs - 1 - ip}
        state = {sid: s_s[sid[1], :, lanes[sid[0]]] for sid in stream_ids}
        for step in range(2):
            e = {0: step, 1: 1 - step}
            res, bd_x = {}, {}
            for (hp, d) in stream_ids:
                base = pl.multiple_of(2 * pp[d] * PAIR + e[d] * PAIR, PAIR)
                _, bd_s = bd_of(state[(hp, d)])
                res[(hp, d)] = _dot(wq_s[d, pl.ds(base, PAIR), lanes[hp]], bd_s)
            for (hp, d) in stream_ids:
                r0 = pl.multiple_of(pp[d] * PAIR + e[d] * CHUNK, CHUNK)
                vn = u_s[d, pl.ds(r0, CHUNK), lanes[hp]].astype(F32) - res[(hp, d)][:CHUNK]
                vnb = vn.astype(BF16)
                x = jnp.concatenate([vnb, zeros_half] if e[d] == 0 else [zeros_half, vnb], axis=0)
                bd_x[(hp, d)] = _bd2(x[:, :HEAD_D], x[:, HEAD_D:])
            for (hp, d) in stream_ids:
                c = 2 * pp[d] + e[d]
                eg = jnp.concatenate([egl_s[d, c, 2 * hp:2 * hp + 1, :],
                                      egl_s[d, c, 2 * hp + 1:2 * hp + 2, :]], axis=1)
                state[(hp, d)] = state[(hp, d)] * eg + _dot(kdt_s[d, 2 * pp[d] + hp], bd_x[(hp, d)])
            if write_o:
                for (hp, d) in stream_ids:
                    r0 = pl.multiple_of(pp[d] * PAIR + e[d] * CHUNK, CHUNK)
                    o = res[(hp, d)][CHUNK:] + _dot(at_s[d, pl.ds(r0, CHUNK), lanes[hp]], bd_x[(hp, d)])
                    o_s[pl.ds(r0, CHUNK), lanes[hp]] += o
        for (hp, d) in stream_ids:
            s_s[d, :, lanes[hp]] = state[(hp, d)]
        return carry

    lax.fori_loop(0, n_pairs, phase_b, 0)


def _gdn_kernel(q_ref, k_ref, v_ref, kt_ref, gat_ref,
                qc_ref, kc_ref, vc_ref, ktc_ref, gatc_ref, g_ref,
                y_ref, u_s, wq_s, at_s, kdt_s, egl_s, s_s, o_s):
    l = q_ref.shape[0]
    lc = qc_ref.shape[0]
    s_s[...] = jnp.zeros(s_s.shape, F32)
    o_s[...] = jnp.zeros(o_s.shape, F32)
    scratch = (u_s, wq_s, at_s, kdt_s, egl_s, s_s, o_s)
    _gdn_scan(qc_ref, kc_ref, vc_ref, ktc_ref, gatc_ref, lc // PAIR, False, *scratch)
    _gdn_scan(q_ref, k_ref, v_ref, kt_ref, gat_ref, l // PAIR, True, *scratch)

    def epilogue(rb, carry):
        r0 = pl.multiple_of(rb * ROW_BLOCK, ROW_BLOCK)
        o = o_s[pl.ds(r0, ROW_BLOCK), :]
        segs = [_rms(o[:, h * HEAD_D:(h + 1) * HEAD_D]) * g_ref[...] for h in range(HEADS)]
        y_ref[pl.ds(r0, ROW_BLOCK), :] = jnp.concatenate(segs, axis=1).astype(BF16)
        return carry

    lax.fori_loop(0, l // ROW_BLOCK, epilogue, 0)


def _gdn(pg, kt, gat, pgc, ktc, gatc, g_gdn):
    b, l, _ = pg.shape
    lc = pgc.shape[1]
    n_pairs, n_pairs_c = l // PAIR, lc // PAIR

    def group(length, g):
        return pl.BlockSpec((None, length, GROUP_W), lambda i, g=g: (i, 0, g))

    return pl.pallas_call(
        _gdn_kernel,
        grid=(b,),
        in_specs=[group(l, 1), group(l, 2), group(l, 3),
                  pl.BlockSpec((None, n_pairs, GROUP_W, PAIR), lambda i: (i, 0, 0, 0)),
                  pl.BlockSpec((None, n_pairs, N_GATES, PAIR), lambda i: (i, 0, 0, 0)),
                  group(lc, 1), group(lc, 2), group(lc, 3),
                  pl.BlockSpec((None, n_pairs_c, GROUP_W, PAIR), lambda i: (i, 0, 0, 0)),
                  pl.BlockSpec((None, n_pairs_c, N_GATES, PAIR), lambda i: (i, 0, 0, 0)),
                  pl.BlockSpec((1, HEAD_D), lambda i: (0, 0))],
        out_specs=pl.BlockSpec((None, l, GROUP_W), lambda i: (i, 0, 0)),
        out_shape=jax.ShapeDtypeStruct((b, l, GROUP_W), BF16),
        scratch_shapes=[pltpu.VMEM((2, l, GROUP_W), BF16),
                        pltpu.VMEM((2, 2 * l, GROUP_W), BF16),
                        pltpu.VMEM((2, l, GROUP_W), BF16),
                        pltpu.VMEM((2, 2 * n_pairs, HEAD_D, 2 * PAIR), BF16),
                        pltpu.VMEM((2, 2 * n_pairs, 8, PAIR), F32),
                        pltpu.VMEM((2, HEAD_D, GROUP_W), F32),
                        pltpu.VMEM((l, GROUP_W), F32)],
        compiler_params=_params(58, ("arbitrary",)),
        name="gdn",
    )(pg, pg, pg, kt, gat, pgc, pgc, pgc, ktc, gatc, g_gdn)


def _fourier_w_kernel(cc_ref, sc_ref, w_ref, o_ref):
    hi = lax.Precision.HIGHEST
    for g in range(w_ref.shape[0]):
        w = w_ref[g]
        o_ref[g] = jnp.concatenate(
            [jnp.dot(cc_ref[...], w, precision=hi, preferred_element_type=F32),
             -jnp.dot(sc_ref[...], w, precision=hi, preferred_element_type=F32)], axis=1)


def _fourier_w(cc, sc, w_f):
    groups, c, d = w_f.shape
    return pl.pallas_call(
        _fourier_w_kernel,
        out_shape=jax.ShapeDtypeStruct((groups, c, 2 * d), F32),
        name="fourier_w",
    )(cc, sc, w_f)


@functools.lru_cache(maxsize=None)
def _dft_constants(length, channels):
    def cos_sin(n, keep):
        idx = np.arange(keep, dtype=np.int64)
        ang = (np.outer(idx, idx) % n).astype(np.float64) * (2.0 * np.pi / n)
        return np.cos(ang), np.sin(ang)
    c_l, s_l = cos_sin(length, length // 2)
    c_c, s_c = cos_sin(channels, channels)
    norm = 1.0 / math.sqrt(length * channels)
    return (jnp.asarray(c_l.astype(np.float32)), jnp.asarray(s_l.astype(np.float32)),
            jnp.asarray((c_c * norm).astype(np.float32)),
            jnp.asarray((s_c * norm).astype(np.float32)))


def _mix_kernel(u_ref, on_ref, z_ref, ch_ref, sh_ref, mcs_ref, wout_ref, x_ref, gpost_ref, gt_ref,
                gpre_ref, shift_ref, scale_ref, x1_ref, h2_ref,
                yf_s, ch_s, sh_s, mc_s, ms_s, wout_s, ve_s, vo_s, d_s):
    i, m = pl.program_id(0), pl.program_id(1)
    l = u_ref.shape[0]
    h = l // 2
    blk = PAIR
    n_blk = h // blk

    @pl.when((i == 0) & (m == 0))
    def _():
        ch_s[...] = ch_ref[...].astype(BF16)
        sh_s[...] = sh_ref[...].astype(BF16)
        wout_s[...] = wout_ref[...].astype(BF16)
        mc_s[...] = jnp.zeros(mc_s.shape, BF16)
        ms_s[...] = jnp.zeros(ms_s.shape, BF16)
        for g in range(HEADS):
            gs = slice(g * HEAD_D, (g + 1) * HEAD_D)
            mc_s[gs, gs] = mcs_ref[g][:, :HEAD_D].astype(BF16)
            ms_s[gs, gs] = mcs_ref[g][:, HEAD_D:].astype(BF16)

    @pl.when(m == 0)
    def _():
        ri = lax.broadcasted_iota(jnp.int32, (blk, 2 * blk), 0)
        ci = lax.broadcasted_iota(jnp.int32, (blk, 2 * blk), 1)
        rev = jnp.where(((ci < blk) & (ri + ci == blk)) | ((ri == 0) & (ci == blk)), 1.0, 0.0).astype(BF16)
        zero_blk = jnp.zeros((blk, GROUP_W), BF16)

        def reversed_block(ref, base, k, last):
            top = ref[base + (n_blk - 1 - k) * blk:base + (n_blk - k) * blk, :]
            bottom = last if k == 0 else ref[base + (n_blk - k) * blk:base + (n_blk - k + 1) * blk, :]
            return _dot(rev, jnp.concatenate([top, bottom], axis=0))

        for k in range(n_blk):
            low = u_ref[k * blk:(k + 1) * blk, :].astype(F32)
            mirrored = reversed_block(u_ref, h, k, zero_blk)
            ve_s[k * blk:(k + 1) * blk, :] = _dot((low + mirrored).astype(BF16), mc_s[...]).astype(BF16)
            vo_s[k * blk:(k + 1) * blk, :] = _dot((low - mirrored).astype(BF16), ms_s[...]).astype(BF16)
        sign_row = jnp.where(lax.broadcasted_iota(jnp.int32, (16, l), 1) % 2 == 0, 1.0, -1.0)
        sign_row = jnp.where(lax.broadcasted_iota(jnp.int32, (16, l), 0) == 0, sign_row, 0.0).astype(BF16)
        alt_u = _dot(sign_row, u_ref[...])
        y_nyq = _dot(alt_u.astype(BF16), mc_s[...])[0:1, :]
        v_nyq = _dot(u_ref[h:h + 16, :], mc_s[...])[0:1, :]
        sign_col = jnp.where(lax.broadcasted_iota(jnp.int32, (blk, GROUP_W), 0) % 2 == 0, 1.0, -1.0)
        first_row = lax.broadcasted_iota(jnp.int32, (blk, GROUP_W), 0) == 0
        d_s[h:h + blk, :] = jnp.where(first_row, y_nyq, 0.0).astype(BF16)
        for k in range(n_blk):
            rows = slice(k * blk, (k + 1) * blk)
            a = _dot(ch_s[rows, :], ve_s[...]) + sign_col * v_nyq
            b = _dot(sh_s[rows, :], vo_s[...])
            yf_s[rows, :] = (a + b).astype(BF16)
            d_s[rows, :] = (a - b).astype(BF16)
        for k in range(n_blk):
            yf_s[h + k * blk:h + (k + 1) * blk, :] = reversed_block(
                d_s, 0, k, d_s[h:h + blk, :]).astype(BF16)

    tm = x_ref.shape[0]
    r0 = pl.multiple_of(m * tm, tm)
    yd = (on_ref[...].astype(F32) * z_ref[...].astype(F32)).astype(BF16)
    half = wout_s.shape[0] // 2
    y = _dot(yf_s[pl.ds(r0, tm), :], wout_s[:half, :]) + _dot(yd, wout_s[half:, :])
    x1 = x_ref[...] + gt_ref[...] * (_rms(y) * gpost_ref[...])
    x1_ref[...] = x1
    h2 = _rms(x1) * gpre_ref[...]
    h2_ref[...] = (h2 * (1.0 + scale_ref[...]) + shift_ref[...]).astype(BF16)


def _mix(pg, on, ch, sh, mcs, w_out, x, g_post, gt, g_pre, shift, scale):
    b, l, d = x.shape
    tm = MIX_TILE
    h = l // 2
    vec = pl.BlockSpec((1, d), lambda i, m: (0, 0))
    mod = pl.BlockSpec((None, 1, d), lambda i, m: (i, 0, 0))
    whole = lambda a: pl.BlockSpec(a.shape, lambda i, m: (0,) * a.ndim)
    return pl.pallas_call(
        _mix_kernel,
        grid=(b, l // tm),
        in_specs=[pl.BlockSpec((None, l, GROUP_W), lambda i, m: (i, 0, 0)),
                  pl.BlockSpec((None, tm, GROUP_W), lambda i, m: (i, m, 0)),
                  pl.BlockSpec((None, tm, GROUP_W), lambda i, m: (i, m, N_GROUPS - 1)),
                  whole(ch), whole(sh), whole(mcs), whole(w_out),
                  pl.BlockSpec((None, tm, d), lambda i, m: (i, m, 0)),
                  vec, mod, vec, mod, mod],
        out_specs=[pl.BlockSpec((None, tm, d), lambda i, m: (i, m, 0)),
                   pl.BlockSpec((None, tm, d), lambda i, m: (i, m, 0))],
        out_shape=[jax.ShapeDtypeStruct((b, l, d), F32),
                   jax.ShapeDtypeStruct((b, l, d), BF16)],
        scratch_shapes=[pltpu.VMEM((l, GROUP_W), BF16),
                        pltpu.VMEM((h, h), BF16),
                        pltpu.VMEM((h, h), BF16),
                        pltpu.VMEM((GROUP_W, GROUP_W), BF16),
                        pltpu.VMEM((GROUP_W, GROUP_W), BF16),
                        pltpu.VMEM(w_out.shape, BF16),
                        pltpu.VMEM((h, GROUP_W), BF16),
                        pltpu.VMEM((h, GROUP_W), BF16),
                        pltpu.VMEM((h + PAIR, GROUP_W), BF16)],
        compiler_params=_params(48, ("arbitrary", "arbitrary")),
        name="mix",
    )(pg, on, pg, ch, sh, mcs, w_out, x, g_post, gt, g_pre, shift, scale)


def _ffn_kernel(h_ref, x1_ref, wv_ref, wg_ref, dwc_ref, wd_ref, gpost_ref, gt_ref,
                o_ref, gate_s, val_s, wvg_s, wd_s):
    j = pl.program_id(1)
    l = h_ref.shape[0]
    tf = wv_ref.shape[1]
    n_rb = l // ROW_BLOCK
    pad = GRID_W

    wvg_s[:, :tf] = wv_ref[...].astype(BF16)
    wvg_s[:, tf:] = wg_ref[...].astype(BF16)
    wd_s[...] = wd_ref[...].astype(BF16)
    gate_s[0:pad, :] = jnp.zeros((pad, tf), F32)
    gate_s[l + pad:l + 2 * pad, :] = jnp.zeros((pad, tf), F32)

    @pl.when(j == 0)
    def _():
        o_ref[...] = jnp.zeros(o_ref.shape, F32)

    n_ext = ROW_BLOCK + 2 * pad
    col = lax.broadcasted_iota(jnp.int32, (n_ext, tf), 0) % GRID_W
    not_first = jnp.where(col == 0, 0.0, 1.0)
    not_last = jnp.where(col == GRID_W - 1, 0.0, 1.0)
    dw = dwc_ref[...]

    def up(i):
        r0 = i * ROW_BLOCK
        vg = _dot(h_ref[r0:r0 + ROW_BLOCK, :], wvg_s[...])
        val_s[r0:r0 + ROW_BLOCK, :] = vg[:, :tf]
        gate_s[r0 + pad:r0 + pad + ROW_BLOCK, :] = vg[:, tf:]

    def down(i):
        r0 = i * ROW_BLOCK
        ext = gate_s[r0:r0 + n_ext, :]
        shifted = (pltpu.roll(ext, 1, 0) * not_first, ext, pltpu.roll(ext, n_ext - 1, 0) * not_last)
        conv = None
        for dy in range(3):
            for dx in range(3):
                term = shifted[dx][dy * pad:dy * pad + ROW_BLOCK] * dw[3 * dy + dx:3 * dy + dx + 1]
                conv = term if conv is None else conv + term
        act = (_silu(conv) * val_s[r0:r0 + ROW_BLOCK, :]).astype(BF16)
        o_ref[r0:r0 + ROW_BLOCK, :] += _dot(act, wd_s[...])

    up(0)
    up(1)
    for i in range(n_rb):
        if i + 2 < n_rb:
            up(i + 2)
        down(i)

    @pl.when(j == pl.num_programs(1) - 1)
    def _():
        def fin(rb, carry):
            r0 = pl.multiple_of(rb * ROW_BLOCK, ROW_BLOCK)
            y = _rms(o_ref[pl.ds(r0, ROW_BLOCK), :]) * gpost_ref[...]
            o_ref[pl.ds(r0, ROW_BLOCK), :] = x1_ref[pl.ds(r0, ROW_BLOCK), :] + gt_ref[...] * y
            return carry
        lax.fori_loop(0, n_rb, fin, 0)


def _ffn(h2, x1, w_up, dwc, w_down, g_post, gt):
    b, l, d = x1.shape
    d_ff = w_down.shape[0]
    tf = FFN_TILE
    n_f = d_ff // tf
    return pl.pallas_call(
        _ffn_kernel,
        grid=(b, n_f),
        in_specs=[pl.BlockSpec((None, l, d), lambda i, j: (i, 0, 0)),
                  pl.BlockSpec((None, l, d), lambda i, j: (i, 0, 0)),
                  pl.BlockSpec((d, tf), lambda i, j: (0, j)),
                  pl.BlockSpec((d, tf), lambda i, j: (0, n_f + j)),
                  pl.BlockSpec((9, tf), lambda i, j: (0, j)),
                  pl.BlockSpec((tf, d), lambda i, j: (j, 0)),
                  pl.BlockSpec((1, d), lambda i, j: (0, 0)),
                  pl.BlockSpec((None, 1, d), lambda i, j: (i, 0, 0))],
        out_specs=pl.BlockSpec((None, l, d), lambda i, j: (i, 0, 0)),
        out_shape=jax.ShapeDtypeStruct((b, l, d), F32),
        scratch_shapes=[pltpu.VMEM((l + 2 * GRID_W, tf), F32),
                        pltpu.VMEM((l, tf), F32),
                        pltpu.VMEM((d, 2 * tf), BF16),
                        pltpu.VMEM((tf, d), BF16)],
        compiler_params=_params(58, ("arbitrary", "arbitrary")),
        name="ffn",
    )(h2, x1, w_up, w_up, dwc, w_down, g_post, gt)


def kernel(x, c, ctx, c_ctx, w_ada, b_ada, g_pre_mix, g_post_mix, g_pre_ffn, g_post_ffn,
           w_in, w_qkv_conv, a_log, dt_bias, g_gdn, w_fourier, w_out, w_up, w_dwc, w_down):
    b, l, d = x.shape
    assert w_ada.shape[0] == 1, "single-layer stack"
    assert b < 16 and l % ROW_BLOCK == 0 and ctx.shape[1] % ROW_BLOCK == 0
    assert w_in.shape[2] == N_GROUPS * GROUP_W + N_GATES and d == 2 * GROUP_W

    cc = jnp.zeros((16, d), F32).at[:b].set(c).at[b].set(c_ctx)
    mod = _ada(cc, w_ada[0], b_ada)
    sh1, sc1, gt1, sh2, sc2, gt2 = [m[:, None, :] for m in jnp.split(mod[:b], 6, axis=-1)]
    sh1c, sc1c = [jnp.broadcast_to(m[None, None, :], (b, 1, d))
                  for m in jnp.split(mod[b], 6, axis=-1)[:2]]

    wabt = w_in[0][:, N_GROUPS * GROUP_W:].T
    zeros8 = jnp.zeros((8,), F32)
    alog = jnp.concatenate([zeros8, a_log[0].reshape(-1)])[:, None]
    dtb = jnp.concatenate([zeros8, dt_bias[0].reshape(-1)])[:, None]

    pg, kt, gat = _inproj(x, g_pre_mix, sh1, sc1, w_in[0], wabt, w_qkv_conv[0], alog, dtb)
    pgc, ktc, gatc = _inproj(ctx, g_pre_mix, sh1c, sc1c, w_in[0], wabt, w_qkv_conv[0], alog, dtb)
    on = _gdn(pg, kt, gat, pgc, ktc, gatc, g_gdn)

    ch, sh, c_c, s_c = _dft_constants(l, HEAD_D)
    mcs = _fourier_w(c_c, s_c, w_fourier[0])
    x1, h2 = _mix(pg, on, ch, sh, mcs, w_out[0], x, g_post_mix, gt1, g_pre_ffn, sh2, sc2)
    d_ff = w_down.shape[1]
    return _ffn(h2, x1, w_up[0], w_dwc[0].reshape(9, d_ff), w_down[0], g_post_ffn, gt2)
```

```python
import functools
import math

import numpy as np
import jax
import jax.numpy as jnp
from jax import lax
from jax.experimental import pallas as pl
from jax.experimental.pallas import tpu as pltpu

F32 = jnp.float32
BF16 = jnp.bfloat16

GRID_W = 64
HEADS = 4
HEAD_D = 128
CHUNK = 64
PAIR = 2 * CHUNK
GROUP_W = HEADS * HEAD_D
N_GROUPS = 5
N_GATES = 16
EPS = 1e-6
ROW_BLOCK = 256
FFN_TILE = 256
MIX_TILE = 256
PAIRS_PER_STEP = 2
MIB = 1024 * 1024

_NT = (((1,), (1,)), ((), ()))


def _dot(a, b):
    return jnp.dot(a, b, preferred_element_type=F32)


def _silu(x):
    return x / (1.0 + jnp.exp(-x))


def _rms(x):
    return x * lax.rsqrt(jnp.mean(x * x, axis=-1, keepdims=True) + EPS)


def _params(vmem_mib, semantics):
    return pltpu.CompilerParams(dimension_semantics=semantics,
                                vmem_limit_bytes=vmem_mib * MIB)


def _ada_kernel(c_ref, w_ref, b_ref, o_ref):
    s = _silu(c_ref[...]).astype(BF16)
    o_ref[...] = _dot(s, w_ref[...].astype(BF16)) + b_ref[...]


def _ada(cc, w, b):
    rows, d = cc.shape
    n = w.shape[1]
    tn = 1024
    return pl.pallas_call(
        _ada_kernel,
        grid=(n // tn,),
        in_specs=[pl.BlockSpec((rows, d), lambda j: (0, 0)),
                  pl.BlockSpec((d, tn), lambda j: (0, j)),
                  pl.BlockSpec((1, tn), lambda j: (0, j))],
        out_specs=pl.BlockSpec((rows, tn), lambda j: (0, j)),
        out_shape=jax.ShapeDtypeStruct((rows, n), F32),
        compiler_params=_params(32, ("arbitrary",)),
        name="ada",
    )(cc, w, b)


def _inproj_kernel(x_ref, g_ref, sh_ref, sc_ref, w_ref, wabt_ref, conv_ref, alog_ref, dtb_ref,
                   pg_ref, kt_ref, gat_ref, h_s, w_s, p_s):
    j = pl.program_id(1)
    lx = x_ref.shape[0]
    n_rb = lx // ROW_BLOCK
    slabs = ROW_BLOCK // PAIR
    n_ext = ROW_BLOCK + 16

    w_s[...] = w_ref[...].astype(BF16)

    def rows(rb):
        return slice(rb * ROW_BLOCK, (rb + 1) * ROW_BLOCK)

    def project(rb):
        return _dot(h_s[rows(rb), :], w_s[...])

    def pipelined(first, second, lead):
        for rb in range(min(lead, n_rb)):
            first(rb)
        for rb in range(n_rb):
            if rb + lead < n_rb:
                first(rb + lead)
            second(rb)

    @pl.when(j == 0)
    def _u():
        p_s[0:8, :] = jnp.zeros((8, GROUP_W), F32)
        p_s[lx + 8:lx + 16, :] = jnp.zeros((8, GROUP_W), F32)
        wabt = wabt_ref[...].astype(BF16)
        row = lax.broadcasted_iota(jnp.int32, (N_GATES, PAIR), 0)

        def prologue(rb):
            h = _rms(x_ref[rows(rb), :]) * g_ref[...]
            hb = (h * (1.0 + sc_ref[...]) + sh_ref[...]).astype(BF16)
            h_s[rows(rb), :] = hb
            for s in range(slabs):
                ab = lax.dot_general(wabt, hb[s * PAIR:(s + 1) * PAIR], _NT,
                                     preferred_element_type=F32)
                beta = 1.0 / (1.0 + jnp.exp(-ab))
                xs = ab + dtb_ref[...]
                softplus = jnp.maximum(xs, 0.0) + jnp.log1p(jnp.exp(-jnp.abs(xs)))
                gate = -jnp.exp(alog_ref[...]) * softplus
                gat_ref[rb * slabs + s] = jnp.where(row < 8, beta, gate)

        def store_u(rb):
            pg_ref[rows(rb), :] = project(rb).astype(BF16)

        pipelined(prologue, store_u, 2)

    @pl.when(j == N_GROUPS - 1)
    def _z():
        for rb in range(n_rb):
            pg_ref[rows(rb), :] = _silu(project(rb)).astype(BF16)

    def conv_group(normalise, transpose):
        cw = conv_ref[...]

        def raw(rb):
            p_s[rb * ROW_BLOCK + 8:(rb + 1) * ROW_BLOCK + 8, :] = project(rb)

        def finish(rb):
            ext = p_s[rb * ROW_BLOCK:rb * ROW_BLOCK + n_ext, :]
            prev = pltpu.roll(ext, 1, 0)[8:8 + ROW_BLOCK]
            nxt = pltpu.roll(ext, n_ext - 1, 0)[8:8 + ROW_BLOCK]
            cur = ext[8:8 + ROW_BLOCK]
            a = _silu(prev * cw[0:1] + cur * cw[1:2] + nxt * cw[2:3])
            if normalise:
                segs = []
                for hh in range(HEADS):
                    seg = a[:, hh * HEAD_D:(hh + 1) * HEAD_D]
                    segs.append(seg * lax.rsqrt(jnp.sum(seg * seg, axis=-1, keepdims=True) + EPS))
                a = jnp.concatenate(segs, axis=1)
            pg_ref[rows(rb), :] = a.astype(BF16)
            if transpose:
                at = a.T.astype(BF16)
                for s in range(slabs):
                    kt_ref[rb * slabs + s] = at[:, s * PAIR:(s + 1) * PAIR]

        pipelined(raw, finish, 2)

    @pl.when(j == 1)
    def _q():
        conv_group(True, False)

    @pl.when(j == 2)
    def _k():
        conv_group(True, True)

    @pl.when(j == 3)
    def _v():
        conv_group(False, False)


def _inproj(x, g, shift, scale, w_in, wabt, conv, alog, dtb):
    b, lx, d = x.shape
    n_pairs = lx // PAIR
    return pl.pallas_call(
        _inproj_kernel,
        grid=(b, N_GROUPS),
        in_specs=[pl.BlockSpec((None, lx, d), lambda i, j: (i, 0, 0)),
                  pl.BlockSpec((1, d), lambda i, j: (0, 0)),
                  pl.BlockSpec((None, 1, d), lambda i, j: (i, 0, 0)),
                  pl.BlockSpec((None, 1, d), lambda i, j: (i, 0, 0)),
                  pl.BlockSpec((d, GROUP_W), lambda i, j: (0, j)),
                  pl.BlockSpec((N_GATES, d), lambda i, j: (0, 0)),
                  pl.BlockSpec((3, GROUP_W), lambda i, j: (0, jnp.clip(j - 1, 0, 2))),
                  pl.BlockSpec((N_GATES, 1), lambda i, j: (0, 0)),
                  pl.BlockSpec((N_GATES, 1), lambda i, j: (0, 0))],
        out_specs=[pl.BlockSpec((None, lx, GROUP_W), lambda i, j: (i, 0, j)),
                   pl.BlockSpec((None, n_pairs, GROUP_W, PAIR), lambda i, j: (i, 0, 0, 0)),
                   pl.BlockSpec((None, n_pairs, N_GATES, PAIR), lambda i, j: (i, 0, 0, 0))],
        out_shape=[jax.ShapeDtypeStruct((b, lx, N_GROUPS * GROUP_W), BF16),
                   jax.ShapeDtypeStruct((b, n_pairs, GROUP_W, PAIR), BF16),
                   jax.ShapeDtypeStruct((b, n_pairs, N_GATES, PAIR), F32)],
        scratch_shapes=[pltpu.VMEM((lx, d), BF16),
                        pltpu.VMEM((d, GROUP_W), BF16),
                        pltpu.VMEM((lx + 16, GROUP_W), F32)],
        compiler_params=_params(48, ("arbitrary", "arbitrary")),
        name="inproj",
    )(x, g, shift, scale, w_in, wabt, conv, alog, dtb)


def _bd2(a, b):
    z = jnp.zeros_like(a)
    return jnp.concatenate([jnp.concatenate([a, z], axis=1),
                            jnp.concatenate([z, b], axis=1)], axis=0)


def _gdn_scan(q_ref, k_ref, v_ref, kt_ref, gat_ref, n_pairs, write_o,
              u_s, wq_s, at_s, kdt_s, egl_s, s_s, o_s):
    ri = lax.broadcasted_iota(jnp.int32, (PAIR, PAIR), 0)
    ci = lax.broadcasted_iota(jnp.int32, (PAIR, PAIR), 1)
    same = (ri // CHUNK) == (ci // CHUNK)
    incl = (jnp.where(same & (ri >= ci), 1.0, 0.0), jnp.where(same & (ri <= ci), 1.0, 0.0))
    strict = (jnp.where(same & (ri > ci), 1.0, 0.0), jnp.where(same & (ri < ci), 1.0, 0.0))
    tri_u = incl[1]
    tri_l = incl[0]
    eye = jnp.where(ri == ci, 1.0, 0.0)
    eye2 = jnp.concatenate([eye, eye], axis=1)
    off = ([], [])
    for lvl in range(6):
        b = 2 ** lvl
        joined = (ri // (2 * b)) == (ci // (2 * b))
        off[0].append(jnp.where(joined & (ri % (2 * b) >= b) & (ci % (2 * b) < b), 1.0, 0.0))
        off[1].append(jnp.where(joined & (ri % (2 * b) < b) & (ci % (2 * b) >= b), 1.0, 0.0))

    def joining_blocks(a, d, lvl):
        m = off[d][lvl]
        return jnp.concatenate([a[:, :HEAD_D] * m, a[:, HEAD_D:] * m], axis=1)
    lane = lax.broadcasted_iota(jnp.int32, (1, PAIR), 1)
    first_half = lane < CHUNK
    scale = HEAD_D ** -0.5
    hi = lax.Precision.HIGHEST

    def md_and_friends(p, gates, gcs, gcs_t, kk, qk, q4, kt4, hp, d):
        ps, attn, qd, kdt, erow = [], [], [], [], []
        for hh in range(2):
            h = 2 * hp + hh
            brow = gates[4 * d + h:4 * d + h + 1, :]
            gcr = gcs[d][8 + 4 * d + h:9 + 4 * d + h, :]
            gcc = gcs_t[d][:, 8 + 4 * d + h:9 + 4 * d + h]
            diff = (gcc - gcr) * incl[d]
            decay = jnp.exp(diff) * incl[d]
            ps.append(kk[h] * decay * strict[d] * brow)
            attn.append(qk[h] * decay * (brow * scale))
            qd.append(q4[h].astype(F32) * (jnp.exp(gcc) * scale))
            if d == 0:
                gl0, gl1 = gcr[:, CHUNK - 1:CHUNK], gcr[:, PAIR - 1:PAIR]
            else:
                gl0, gl1 = gcr[:, 0:1], gcr[:, CHUNK:CHUNK + 1]
            glr = jnp.where(first_half, gl0, gl1)
            kdt.append(kt4[h].astype(F32) * (jnp.exp(glr - gcr) * brow))
            erow.append(jnp.exp(gcr))
            egl_s[d, 2 * p, h:h + 1, :] = jnp.broadcast_to(jnp.exp(gl0), (1, PAIR))
            egl_s[d, 2 * p + 1, h:h + 1, :] = jnp.broadcast_to(jnp.exp(gl1), (1, PAIR))
        return (jnp.concatenate(ps, axis=1), jnp.concatenate(attn, axis=1).astype(BF16),
                jnp.concatenate(qd, axis=1).astype(BF16), jnp.concatenate(kdt, axis=1).astype(BF16),
                jnp.concatenate(erow, axis=1))

    def bd_of(x):
        xb = x.astype(BF16)
        return xb, _bd2(xb[:, :HEAD_D], xb[:, HEAD_D:])

    def phase_a(it, carry):
        loaded = []
        for pi in range(PAIRS_PER_STEP):
            p = it * PAIRS_PER_STEP + pi
            t0 = pl.multiple_of(p * PAIR, PAIR)
            gates = gat_ref[p]
            kt_all = kt_ref[p]
            heads = [slice(h * HEAD_D, (h + 1) * HEAD_D) for h in range(HEADS)]
            k4 = [k_ref[pl.ds(t0, PAIR), hl] for hl in heads]
            q4 = [q_ref[pl.ds(t0, PAIR), hl] for hl in heads]
            v4 = [v_ref[pl.ds(t0, PAIR), hl] for hl in heads]
            kt4 = [kt_all[hl, :] for hl in heads]
            kk = [_dot(k4[h], kt4[h]) for h in range(HEADS)]
            qk = [_dot(q4[h], kt4[h]) for h in range(HEADS)]
            gcs = (jnp.dot(gates, tri_u, precision=hi, preferred_element_type=F32),
                   jnp.dot(gates, tri_l, precision=hi, preferred_element_type=F32))
            loaded.append((p, t0, gates, gcs, (gcs[0].T, gcs[1].T), k4, q4, v4, kt4, kk, qk))

        streams = []
        for (p, t0, gates, gcs, gcs_t, k4, q4, v4, kt4, kk, qk) in loaded:
            for hp in range(HEADS // 2):
                bd_k = _bd2(k4[2 * hp], k4[2 * hp + 1])
                bd_v = _bd2(v4[2 * hp], v4[2 * hp + 1])
                for d in range(2):
                    a, attn, qdb, kdtb, erow = md_and_friends(
                        p, gates, gcs, gcs_t, kk, qk, q4, kt4, hp, d)
                    streams.append(dict(p=p, t0=t0, hp=hp, d=d, a=a, attn=attn, qd=qdb,
                                        kdt=kdtb, erow=erow, bd_k=bd_k, bd_v=bd_v))

        for st in streams:
            st["t"] = eye2 - joining_blocks(st["a"], st["d"], 0)
        for lvl in range(1, 6):
            for st in streams:
                _, bd_t = bd_of(st["t"])
                st["y"] = _dot(joining_blocks(st["a"], st["d"], lvl).astype(BF16), bd_t)
            for st in streams:
                _, bd_y = bd_of(st["y"])
                st["t"] = st["t"] - _dot(st["t"].astype(BF16), bd_y)
        for st in streams:
            t = st["t"]
            st["u"] = _dot(t.astype(BF16), st["bd_v"])
            st["w"] = _dot((t * st["erow"]).astype(BF16), st["bd_k"])
        for st in streams:
            d, hp, t0, p = st["d"], st["hp"], st["t0"], st["p"]
            lanes = slice(hp * 2 * HEAD_D, (hp + 1) * 2 * HEAD_D)
            u_s[d, pl.ds(t0, PAIR), lanes] = st["u"].astype(BF16)
            wb, qdb = st["w"].astype(BF16), st["qd"]
            base = pl.multiple_of(2 * t0, 2 * PAIR)
            wq_s[d, pl.ds(base, 2 * PAIR), lanes] = jnp.concatenate(
                [wb[:CHUNK], qdb[:CHUNK], wb[CHUNK:], qdb[CHUNK:]], axis=0)
            at_s[d, pl.ds(t0, PAIR), lanes] = st["attn"]
            kdt_s[d, 2 * p + hp] = st["kdt"]
        return carry

    lax.fori_loop(0, n_pairs // PAIRS_PER_STEP, phase_a, 0)

    zeros_half = jnp.zeros((CHUNK, 2 * HEAD_D), BF16)
    stream_ids = [(hp, d) for hp in range(HEADS // 2) for d in range(2)]

    def phase_b(ip, carry):
        lanes = {hp: slice(hp * 2 * HEAD_D, (hp + 1) * 2 * HEAD_D) for hp in range(HEADS // 2)}
        pp = {0: ip, 1: n_pairs - 1 - ip}
        state = {sid: s_s[sid[1], :, lanes[sid[0]]] for sid in stream_ids}
        for step in range(2):
            e = {0: step, 1: 1 - step}
            res, bd_x = {}, {}
            for (hp, d) in stream_ids:
                base = pl.multiple_of(2 * pp[d] * PAIR + e[d] * PAIR, PAIR)
                _, bd_s = bd_of(state[(hp, d)])
                res[(hp, d)] = _dot(wq_s[d, pl.ds(base, PAIR), lanes[hp]], bd_s)
            for (hp, d) in stream_ids:
                r0 = pl.multiple_of(pp[d] * PAIR + e[d] * CHUNK, CHUNK)
                vn = u_s[d, pl.ds(r0, CHUNK), lanes[hp]].astype(F32) - res[(hp, d)][:CHUNK]
                vnb = vn.astype(BF16)
                x = jnp.concatenate([vnb, zeros_half] if e[d] == 0 else [zeros_half, vnb], axis=0)
                bd_x[(hp, d)] = _bd2(x[:, :HEAD_D], x[:, HEAD_D:])
            for (hp, d) in stream_ids:
                c = 2 * pp[d] + e[d]
                eg = jnp.concatenate([egl_s[d, c, 2 * hp:2 * hp + 1, :],
                                      egl_s[d, c, 2 * hp + 1:2 * hp + 2, :]], axis=1)
                state[(hp, d)] = state[(hp, d)] * eg + _dot(kdt_s[d, 2 * pp[d] + hp], bd_x[(hp, d)])
            if write_o:
                for (hp, d) in stream_ids:
                    r0 = pl.multiple_of(pp[d] * PAIR + e[d] * CHUNK, CHUNK)
                    o = res[(hp, d)][CHUNK:] + _dot(at_s[d, pl.ds(r0, CHUNK), lanes[hp]], bd_x[(hp, d)])
                    o_s[pl.ds(r0, CHUNK), lanes[hp]] += o
        for (hp, d) in stream_ids:
            s_s[d, :, lanes[hp]] = state[(hp, d)]
        return carry

    lax.fori_loop(0, n_pairs, phase_b, 0)


def _gdn_kernel(q_ref, k_ref, v_ref, kt_ref, gat_ref,
                qc_ref, kc_ref, vc_ref, ktc_ref, gatc_ref, g_ref,
                y_ref, u_s, wq_s, at_s, kdt_s, egl_s, s_s, o_s):
    l = q_ref.shape[0]
    lc = qc_ref.shape[0]
    s_s[...] = jnp.zeros(s_s.shape, F32)
    o_s[...] = jnp.zeros(o_s.shape, F32)
    scratch = (u_s, wq_s, at_s, kdt_s, egl_s, s_s, o_s)
    _gdn_scan(qc_ref, kc_ref, vc_ref, ktc_ref, gatc_ref, lc // PAIR, False, *scratch)
    _gdn_scan(q_ref, k_ref, v_ref, kt_ref, gat_ref, l // PAIR, True, *scratch)

    def epilogue(rb, carry):
        r0 = pl.multiple_of(rb * ROW_BLOCK, ROW_BLOCK)
        o = o_s[pl.ds(r0, ROW_BLOCK), :]
        segs = [_rms(o[:, h * HEAD_D:(h + 1) * HEAD_D]) * g_ref[...] for h in range(HEADS)]
        y_ref[pl.ds(r0, ROW_BLOCK), :] = jnp.concatenate(segs, axis=1).astype(BF16)
        return carry

    lax.fori_loop(0, l // ROW_BLOCK, epilogue, 0)


def _gdn(pg, kt, gat, pgc, ktc, gatc, g_gdn):
    b, l, _ = pg.shape
    lc = pgc.shape[1]
    n_pairs, n_pairs_c = l // PAIR, lc // PAIR

    def group(length, g):
        return pl.BlockSpec((None, length, GROUP_W), lambda i, g=g: (i, 0, g))

    return pl.pallas_call(
        _gdn_kernel,
        grid=(b,),
        in_specs=[group(l, 1), group(l, 2), group(l, 3),
                  pl.BlockSpec((None, n_pairs, GROUP_W, PAIR), lambda i: (i, 0, 0, 0)),
                  pl.BlockSpec((None, n_pairs, N_GATES, PAIR), lambda i: (i, 0, 0, 0)),
                  group(lc, 1), group(lc, 2), group(lc, 3),
                  pl.BlockSpec((None, n_pairs_c, GROUP_W, PAIR), lambda i: (i, 0, 0, 0)),
                  pl.BlockSpec((None, n_pairs_c, N_GATES, PAIR), lambda i: (i, 0, 0, 0)),
                  pl.BlockSpec((1, HEAD_D), lambda i: (0, 0))],
        out_specs=pl.BlockSpec((None, l, GROUP_W), lambda i: (i, 0, 0)),
        out_shape=jax.ShapeDtypeStruct((b, l, GROUP_W), BF16),
        scratch_shapes=[pltpu.VMEM((2, l, GROUP_W), BF16),
                        pltpu.VMEM((2, 2 * l, GROUP_W), BF16),
                        pltpu.VMEM((2, l, GROUP_W), BF16),
                        pltpu.VMEM((2, 2 * n_pairs, HEAD_D, 2 * PAIR), BF16),
                        pltpu.VMEM((2, 2 * n_pairs, 8, PAIR), F32),
                        pltpu.VMEM((2, HEAD_D, GROUP_W), F32),
                        pltpu.VMEM((l, GROUP_W), F32)],
        compiler_params=_params(58, ("arbitrary",)),
        name="gdn",
    )(pg, pg, pg, kt, gat, pgc, pgc, pgc, ktc, gatc, g_gdn)


def _fourier_w_kernel(cc_ref, sc_ref, w_ref, o_ref):
    hi = lax.Precision.HIGHEST
    for g in range(w_ref.shape[0]):
        w = w_ref[g]
        o_ref[g] = jnp.concatenate(
            [jnp.dot(cc_ref[...], w, precision=hi, preferred_element_type=F32),
             -jnp.dot(sc_ref[...], w, precision=hi, preferred_element_type=F32)], axis=1)


def _fourier_w(cc, sc, w_f):
    groups, c, d = w_f.shape
    return pl.pallas_call(
        _fourier_w_kernel,
        out_shape=jax.ShapeDtypeStruct((groups, c, 2 * d), F32),
        name="fourier_w",
    )(cc, sc, w_f)


@functools.lru_cache(maxsize=None)
def _dft_constants(length, channels):
    def cos_sin(n, keep):
        idx = np.arange(keep, dtype=np.int64)
        ang = (np.outer(idx, idx) % n).astype(np.float64) * (2.0 * np.pi / n)
        return np.cos(ang), np.sin(ang)
    c_l, s_l = cos_sin(length, length // 2)
    c_c, s_c = cos_sin(channels, channels)
    norm = 1.0 / math.sqrt(length * channels)
    return (jnp.asarray(c_l.astype(np.float32)), jnp.asarray(s_l.astype(np.float32)),
            jnp.asarray((c_c * norm).astype(np.float32)),
            jnp.asarray((s_c * norm).astype(np.float32)))


def _mix_kernel(u_ref, on_ref, z_ref, ch_ref, sh_ref, mcs_ref, wout_ref, x_ref, gpost_ref, gt_ref,
                gpre_ref, shift_ref, scale_ref, x1_ref, h2_ref,
                yf_s, ch_s, sh_s, mc_s, ms_s, wout_s, ve_s, vo_s, d_s):
    i, m = pl.program_id(0), pl.program_id(1)
    l = u_ref.shape[0]
    h = l // 2
    blk = PAIR
    n_blk = h // blk

    @pl.when((i == 0) & (m == 0))
    def _():
        ch_s[...] = ch_ref[...].astype(BF16)
        sh_s[...] = sh_ref[...].astype(BF16)
        wout_s[...] = wout_ref[...].astype(BF16)
        mc_s[...] = jnp.zeros(mc_s.shape, BF16)
        ms_s[...] = jnp.zeros(ms_s.shape, BF16)
        for g in range(HEADS):
            gs = slice(g * HEAD_D, (g + 1) * HEAD_D)
            mc_s[gs, gs] = mcs_ref[g][:, :HEAD_D].astype(BF16)
            ms_s[gs, gs] = mcs_ref[g][:, HEAD_D:].astype(BF16)

    @pl.when(m == 0)
    def _():
        ri = lax.broadcasted_iota(jnp.int32, (blk, 2 * blk), 0)
        ci = lax.broadcasted_iota(jnp.int32, (blk, 2 * blk), 1)
        rev = jnp.where(((ci < blk) & (ri + ci == blk)) | ((ri == 0) & (ci == blk)), 1.0, 0.0).astype(BF16)
        zero_blk = jnp.zeros((blk, GROUP_W), BF16)

        def reversed_block(ref, base, k, last):
            top = ref[base + (n_blk - 1 - k) * blk:base + (n_blk - k) * blk, :]
            bottom = last if k == 0 else ref[base + (n_blk - k) * blk:base + (n_blk - k + 1) * blk, :]
            return _dot(rev, jnp.concatenate([top, bottom], axis=0))

        for k in range(n_blk):
            low = u_ref[k * blk:(k + 1) * blk, :].astype(F32)
            mirrored = reversed_block(u_ref, h, k, zero_blk)
            ve_s[k * blk:(k + 1) * blk, :] = _dot((low + mirrored).astype(BF16), mc_s[...]).astype(BF16)
            vo_s[k * blk:(k + 1) * blk, :] = _dot((low - mirrored).astype(BF16), ms_s[...]).astype(BF16)
        sign_row = jnp.where(lax.broadcasted_iota(jnp.int32, (16, l), 1) % 2 == 0, 1.0, -1.0)
        sign_row = jnp.where(lax.broadcasted_iota(jnp.int32, (16, l), 0) == 0, sign_row, 0.0).astype(BF16)
        alt_u = _dot(sign_row, u_ref[...])
        y_nyq = _dot(alt_u.astype(BF16), mc_s[...])[0:1, :]
        v_nyq = _dot(u_ref[h:h + 16, :], mc_s[...])[0:1, :]
        sign_col = jnp.where(lax.broadcasted_iota(jnp.int32, (blk, GROUP_W), 0) % 2 == 0, 1.0, -1.0)
        first_row = lax.broadcasted_iota(jnp.int32, (blk, GROUP_W), 0) == 0
        d_s[h:h + blk, :] = jnp.where(first_row, y_nyq, 0.0).astype(BF16)
        for k in range(n_blk):
            rows = slice(k * blk, (k + 1) * blk)
            a = _dot(ch_s[rows, :], ve_s[...]) + sign_col * v_nyq
            b = _dot(sh_s[rows, :], vo_s[...])
            yf_s[rows, :] = (a + b).astype(BF16)
            d_s[rows, :] = (a - b).astype(BF16)
        for k in range(n_blk):
            yf_s[h + k * blk:h + (k + 1) * blk, :] = reversed_block(
                d_s, 0, k, d_s[h:h + blk, :]).astype(BF16)

    tm = x_ref.shape[0]
    r0 = pl.multiple_of(m * tm, tm)
    yd = (on_ref[...].astype(F32) * z_ref[...].astype(F32)).astype(BF16)
    half = wout_s.shape[0] // 2
    y = _dot(yf_s[pl.ds(r0, tm), :], wout_s[:half, :]) + _dot(yd, wout_s[half:, :])
    x1 = x_ref[...] + gt_ref[...] * (_rms(y) * gpost_ref[...])
    x1_ref[...] = x1
    h2 = _rms(x1) * gpre_ref[...]
    h2_ref[...] = (h2 * (1.0 + scale_ref[...]) + shift_ref[...]).astype(BF16)


def _mix(pg, on, ch, sh, mcs, w_out, x, g_post, gt, g_pre, shift, scale):
    b, l, d = x.shape
    tm = MIX_TILE
    h = l // 2
    vec = pl.BlockSpec((1, d), lambda i, m: (0, 0))
    mod = pl.BlockSpec((None, 1, d), lambda i, m: (i, 0, 0))
    whole = lambda a: pl.BlockSpec(a.shape, lambda i, m: (0,) * a.ndim)
    return pl.pallas_call(
        _mix_kernel,
        grid=(b, l // tm),
        in_specs=[pl.BlockSpec((None, l, GROUP_W), lambda i, m: (i, 0, 0)),
                  pl.BlockSpec((None, tm, GROUP_W), lambda i, m: (i, m, 0)),
                  pl.BlockSpec((None, tm, GROUP_W), lambda i, m: (i, m, N_GROUPS - 1)),
                  whole(ch), whole(sh), whole(mcs), whole(w_out),
                  pl.BlockSpec((None, tm, d), lambda i, m: (i, m, 0)),
                  vec, mod, vec, mod, mod],
        out_specs=[pl.BlockSpec((None, tm, d), lambda i, m: (i, m, 0)),
                   pl.BlockSpec((None, tm, d), lambda i, m: (i, m, 0))],
        out_shape=[jax.ShapeDtypeStruct((b, l, d), F32),
                   jax.ShapeDtypeStruct((b, l, d), BF16)],
        scratch_shapes=[pltpu.VMEM((l, GROUP_W), BF16),
                        pltpu.VMEM((h, h), BF16),
                        pltpu.VMEM((h, h), BF16),
                        pltpu.VMEM((GROUP_W, GROUP_W), BF16),
                        pltpu.VMEM((GROUP_W, GROUP_W), BF16),
                        pltpu.VMEM(w_out.shape, BF16),
                        pltpu.VMEM((h, GROUP_W), BF16),
                        pltpu.VMEM((h, GROUP_W), BF16),
                        pltpu.VMEM((h + PAIR, GROUP_W), BF16)],
        compiler_params=_params(48, ("arbitrary", "arbitrary")),
        name="mix",
    )(pg, on, pg, ch, sh, mcs, w_out, x, g_post, gt, g_pre, shift, scale)


def _ffn_kernel(h_ref, x1_ref, wv_ref, wg_ref, dwc_ref, wd_ref, gpost_ref, gt_ref,
                o_ref, gate_s, val_s, wvg_s, wd_s):
    j = pl.program_id(1)
    l = h_ref.shape[0]
    tf = wv_ref.shape[1]
    n_rb = l // ROW_BLOCK
    pad = GRID_W

    wvg_s[:, :tf] = wv_ref[...].astype(BF16)
    wvg_s[:, tf:] = wg_ref[...].astype(BF16)
    wd_s[...] = wd_ref[...].astype(BF16)
    gate_s[0:pad, :] = jnp.zeros((pad, tf), F32)
    gate_s[l + pad:l + 2 * pad, :] = jnp.zeros((pad, tf), F32)

    @pl.when(j == 0)
    def _():
        o_ref[...] = jnp.zeros(o_ref.shape, F32)

    n_ext = ROW_BLOCK + 2 * pad
    col = lax.broadcasted_iota(jnp.int32, (n_ext, tf), 0) % GRID_W
    not_first = jnp.where(col == 0, 0.0, 1.0)
    not_last = jnp.where(col == GRID_W - 1, 0.0, 1.0)
    dw = dwc_ref[...]

    def up(i):
        r0 = i * ROW_BLOCK
        vg = _dot(h_ref[r0:r0 + ROW_BLOCK, :], wvg_s[...])
        val_s[r0:r0 + ROW_BLOCK, :] = vg[:, :tf]
        gate_s[r0 + pad:r0 + pad + ROW_BLOCK, :] = vg[:, tf:]

    def down(i):
        r0 = i * ROW_BLOCK
        ext = gate_s[r0:r0 + n_ext, :]
        shifted = (pltpu.roll(ext, 1, 0) * not_first, ext, pltpu.roll(ext, n_ext - 1, 0) * not_last)
        conv = None
        for dy in range(3):
            for dx in range(3):
                term = shifted[dx][dy * pad:dy * pad + ROW_BLOCK] * dw[3 * dy + dx:3 * dy + dx + 1]
                conv = term if conv is None else conv + term
        act = (_silu(conv) * val_s[r0:r0 + ROW_BLOCK, :]).astype(BF16)
        o_ref[r0:r0 + ROW_BLOCK, :] += _dot(act, wd_s[...])

    up(0)
    up(1)
    for i in range(n_rb):
        if i + 2 < n_rb:
            up(i + 2)
        down(i)

    @pl.when(j == pl.num_programs(1) - 1)
    def _():
        def fin(rb, carry):
            r0 = pl.multiple_of(rb * ROW_BLOCK, ROW_BLOCK)
            y = _rms(o_ref[pl.ds(r0, ROW_BLOCK), :]) * gpost_ref[...]
            o_ref[pl.ds(r0, ROW_BLOCK), :] = x1_ref[pl.ds(r0, ROW_BLOCK), :] + gt_ref[...] * y
            return carry
        lax.fori_loop(0, n_rb, fin, 0)


def _ffn(h2, x1, w_up, dwc, w_down, g_post, gt):
    b, l, d = x1.shape
    d_ff = w_down.shape[0]
    tf = FFN_TILE
    n_f = d_ff // tf
    return pl.pallas_call(
        _ffn_kernel,
        grid=(b, n_f),
        in_specs=[pl.BlockSpec((None, l, d), lambda i, j: (i, 0, 0)),
                  pl.BlockSpec((None, l, d), lambda i, j: (i, 0, 0)),
                  pl.BlockSpec((d, tf), lambda i, j: (0, j)),
                  pl.BlockSpec((d, tf), lambda i, j: (0, n_f + j)),
                  pl.BlockSpec((9, tf), lambda i, j: (0, j)),
                  pl.BlockSpec((tf, d), lambda i, j: (j, 0)),
                  pl.BlockSpec((1, d), lambda i, j: (0, 0)),
                  pl.BlockSpec((None, 1, d), lambda i, j: (i, 0, 0))],
        out_specs=pl.BlockSpec((None, l, d), lambda i, j: (i, 0, 0)),
        out_shape=jax.ShapeDtypeStruct((b, l, d), F32),
        scratch_shapes=[pltpu.VMEM((l + 2 * GRID_W, tf), F32),
                        pltpu.VMEM((l, tf), F32),
                        pltpu.VMEM((d, 2 * tf), BF16),
                        pltpu.VMEM((tf, d), BF16)],
        compiler_params=_params(58, ("arbitrary", "arbitrary")),
        name="ffn",
    )(h2, x1, w_up, w_up, dwc, w_down, g_post, gt)


def kernel(x, c, ctx, c_ctx, w_ada, b_ada, g_pre_mix, g_post_mix, g_pre_ffn, g_post_ffn,
           w_in, w_qkv_conv, a_log, dt_bias, g_gdn, w_fourier, w_out, w_up, w_dwc, w_down):
    b, l, d = x.shape
    assert w_ada.shape[0] == 1, "single-layer stack"
    assert b < 16 and l % ROW_BLOCK == 0 and ctx.shape[1] % ROW_BLOCK == 0
    assert w_in.shape[2] == N_GROUPS * GROUP_W + N_GATES and d == 2 * GROUP_W

    cc = jnp.zeros((16, d), F32).at[:b].set(c).at[b].set(c_ctx)
    mod = _ada(cc, w_ada[0], b_ada)
    sh1, sc1, gt1, sh2, sc2, gt2 = [m[:, None, :] for m in jnp.split(mod[:b], 6, axis=-1)]
    sh1c, sc1c = [jnp.broadcast_to(m[None, None, :], (b, 1, d))
                  for m in jnp.split(mod[b], 6, axis=-1)[:2]]

    wabt = w_in[0][:, N_GROUPS * GROUP_W:].T
    zeros8 = jnp.zeros((8,), F32)
    alog = jnp.concatenate([zeros8, a_log[0].reshape(-1)])[:, None]
    dtb = jnp.concatenate([zeros8, dt_bias[0].reshape(-1)])[:, None]

    pg, kt, gat = _inproj(x, g_pre_mix, sh1, sc1, w_in[0], wabt, w_qkv_conv[0], alog, dtb)
    pgc, ktc, gatc = _inproj(ctx, g_pre_mix, sh1c, sc1c, w_in[0], wabt, w_qkv_conv[0], alog, dtb)
    on = _gdn(pg, kt, gat, pgc, ktc, gatc, g_gdn)

    ch, sh, c_c, s_c = _dft_constants(l, HEAD_D)
    mcs = _fourier_w(c_c, s_c, w_fourier[0])
    x1, h2 = _mix(pg, on, ch, sh, mcs, w_out[0], x, g_post_mix, gt1, g_pre_ffn, sh2, sc2)
    d_ff = w_down.shape[1]
    return _ffn(h2, x1, w_up[0], w_dwc[0].reshape(9, d_ff), w_down[0], g_post_ffn, gt2)
```

```python
import functools
import itertools
import math

import numpy as np
import jax
import jax.numpy as jnp
from jax import lax
from jax.experimental import pallas as pl
from jax.experimental.pallas import tpu as pltpu

F32 = jnp.float32
BF16 = jnp.bfloat16

GRID_W = 64
HEADS = 4
HEAD_D = 128
CHUNK = 64
PAIR = 2 * CHUNK
GROUP_W = HEADS * HEAD_D
N_GROUPS = 5
N_GATES = 16
EPS = 1e-6
ROW_BLOCK = 256
FFN_TILE = 256
MIX_TILE = 512
MIB = 1024 * 1024

_NT = (((1,), (1,)), ((), ()))


def _dot(a, b):
    return jnp.dot(a, b, preferred_element_type=F32)


def _silu(x):
    return x / (1.0 + jnp.exp(-x))


def _rms(x):
    return x * lax.rsqrt(jnp.mean(x * x, axis=-1, keepdims=True) + EPS)


def _params(vmem_mib, semantics):
    return pltpu.CompilerParams(dimension_semantics=semantics,
                                vmem_limit_bytes=vmem_mib * MIB)


def _ada_kernel(c_ref, w_ref, b_ref, o_ref):
    s = _silu(c_ref[...]).astype(BF16)
    o_ref[...] = _dot(s, w_ref[...].astype(BF16)) + b_ref[...]


def _ada(cc, w, b):
    rows, d = cc.shape
    n = w.shape[1]
    tn = 1024
    return pl.pallas_call(
        _ada_kernel,
        grid=(n // tn,),
        in_specs=[pl.BlockSpec((rows, d), lambda j: (0, 0)),
                  pl.BlockSpec((d, tn), lambda j: (0, j)),
                  pl.BlockSpec((1, tn), lambda j: (0, j))],
        out_specs=pl.BlockSpec((rows, tn), lambda j: (0, j)),
        out_shape=jax.ShapeDtypeStruct((rows, n), F32),
        compiler_params=_params(32, ("arbitrary",)),
        name="ada",
    )(cc, w, b)


def _inproj_kernel(x_ref, ctx_ref, g_ref, sh_ref, sc_ref, shc_ref, scc_ref, w_ref, wabt_ref,
                   conv_ref, alog_ref, dtb_ref, pg_ref, kt_ref, gat_ref, h_s, w_s, p_s):
    j = pl.program_id(1)
    l, lc = x_ref.shape[0], ctx_ref.shape[0]
    slabs = ROW_BLOCK // PAIR
    n_ext = ROW_BLOCK + 16
    blocks = [(x_ref, r, r, r + 8, sh_ref, sc_ref) for r in range(0, l, ROW_BLOCK)]
    blocks += [(ctx_ref, r, l + r, l + r + 16, shc_ref, scc_ref) for r in range(0, lc, ROW_BLOCK)]
    n_rb = len(blocks)

    w_s[...] = w_ref[...].astype(BF16)

    def out_rows(rb):
        return slice(blocks[rb][2], blocks[rb][2] + ROW_BLOCK)

    def project(rb):
        return _dot(h_s[out_rows(rb), :], w_s[...])

    def pipelined(first, second, lead):
        for rb in range(min(lead, n_rb)):
            first(rb)
        for rb in range(n_rb):
            if rb + lead < n_rb:
                first(rb + lead)
            second(rb)

    @pl.when(j == 0)
    def _u():
        for r in (0, l + 8, l + lc + 16):
            p_s[r:r + 8, :] = jnp.zeros((8, GROUP_W), F32)
        wabt = wabt_ref[...].astype(BF16)
        row = lax.broadcasted_iota(jnp.int32, (N_GATES, PAIR), 0)

        def prologue(rb):
            src, r, dst, _, shift, scale = blocks[rb]
            h = _rms(src[r:r + ROW_BLOCK, :]) * g_ref[...]
            hb = (h * (1.0 + scale[...]) + shift[...]).astype(BF16)
            h_s[dst:dst + ROW_BLOCK, :] = hb
            for s in range(slabs):
                ab = lax.dot_general(wabt, hb[s * PAIR:(s + 1) * PAIR], _NT,
                                     preferred_element_type=F32)
                beta = 1.0 / (1.0 + jnp.exp(-ab))
                xs = ab + dtb_ref[...]
                softplus = jnp.maximum(xs, 0.0) + jnp.log1p(jnp.exp(-jnp.abs(xs)))
                gate = -jnp.exp(alog_ref[...]) * softplus
                gat_ref[dst // PAIR + s] = jnp.where(row < 8, beta, gate)

        def store_u(rb):
            pg_ref[out_rows(rb), :] = project(rb).astype(BF16)

        pipelined(prologue, store_u, 2)

    @pl.when(j == N_GROUPS - 1)
    def _z():
        for rb in range(n_rb):
            pg_ref[out_rows(rb), :] = _silu(project(rb)).astype(BF16)

    def conv_group(normalise, transpose):
        cw = conv_ref[...]

        def raw(rb):
            p0 = blocks[rb][3]
            p_s[p0:p0 + ROW_BLOCK, :] = project(rb)

        def finish(rb):
            p0 = blocks[rb][3]
            ext = p_s[p0 - 8:p0 - 8 + n_ext, :]
            prev = pltpu.roll(ext, 1, 0)[8:8 + ROW_BLOCK]
            nxt = pltpu.roll(ext, n_ext - 1, 0)[8:8 + ROW_BLOCK]
            cur = ext[8:8 + ROW_BLOCK]
            a = _silu(prev * cw[0:1] + cur * cw[1:2] + nxt * cw[2:3])
            if normalise:
                segs = []
                for hh in range(HEADS):
                    seg = a[:, hh * HEAD_D:(hh + 1) * HEAD_D]
                    segs.append(seg * lax.rsqrt(jnp.sum(seg * seg, axis=-1, keepdims=True) + EPS))
                a = jnp.concatenate(segs, axis=1)
            pg_ref[out_rows(rb), :] = a.astype(BF16)
            if transpose:
                at = a.T.astype(BF16)
                for s in range(slabs):
                    kt_ref[blocks[rb][2] // PAIR + s] = at[:, s * PAIR:(s + 1) * PAIR]

        pipelined(raw, finish, 2)

    @pl.when(j == 1)
    def _q():
        conv_group(True, False)

    @pl.when(j == 2)
    def _k():
        conv_group(True, True)

    @pl.when(j == 3)
    def _v():
        conv_group(False, False)


def _inproj(x, ctx, g, shift, scale, shift_c, scale_c, w_in, wabt, conv, alog, dtb):
    b, l, d = x.shape
    lt = l + ctx.shape[1]
    n_pairs = lt // PAIR
    vec = pl.BlockSpec((1, d), lambda i, j: (0, 0))
    mod = pl.BlockSpec((None, 1, d), lambda i, j: (i, 0, 0))
    return pl.pallas_call(
        _inproj_kernel,
        grid=(b, N_GROUPS),
        in_specs=[pl.BlockSpec((None, l, d), lambda i, j: (i, 0, 0)),
                  pl.BlockSpec((None, lt - l, d), lambda i, j: (i, 0, 0)),
                  vec, mod, mod, vec, vec,
                  pl.BlockSpec((d, GROUP_W), lambda i, j: (0, j)),
                  pl.BlockSpec((N_GATES, d), lambda i, j: (0, 0)),
                  pl.BlockSpec((3, GROUP_W), lambda i, j: (0, jnp.clip(j - 1, 0, 2))),
                  pl.BlockSpec((N_GATES, 1), lambda i, j: (0, 0)),
                  pl.BlockSpec((N_GATES, 1), lambda i, j: (0, 0))],
        out_specs=[pl.BlockSpec((None, lt, GROUP_W), lambda i, j: (i, 0, j)),
                   pl.BlockSpec((None, n_pairs, GROUP_W, PAIR), lambda i, j: (i, 0, 0, 0)),
                   pl.BlockSpec((None, n_pairs, N_GATES, PAIR), lambda i, j: (i, 0, 0, 0))],
        out_shape=[jax.ShapeDtypeStruct((b, lt, N_GROUPS * GROUP_W), BF16),
                   jax.ShapeDtypeStruct((b, n_pairs, GROUP_W, PAIR), BF16),
                   jax.ShapeDtypeStruct((b, n_pairs, N_GATES, PAIR), F32)],
        scratch_shapes=[pltpu.VMEM((lt, d), BF16),
                        pltpu.VMEM((d, GROUP_W), BF16),
                        pltpu.VMEM((lt + 24, GROUP_W), F32)],
        compiler_params=_params(52, ("arbitrary", "arbitrary")),
        name="inproj",
    )(x, ctx, g, shift, scale, shift_c, scale_c, w_in, wabt, conv, alog, dtb)


def _bd2(a, b):
    z = jnp.zeros_like(a)
    return jnp.concatenate([jnp.concatenate([a, z], axis=1),
                            jnp.concatenate([z, b], axis=1)], axis=0)


def _bd_of(x):
    xb = x.astype(BF16)
    return _bd2(xb[:, :HEAD_D], xb[:, HEAD_D:])


def _round_robin(*stage_generators):
    for _ in itertools.zip_longest(*stage_generators):
        pass


def _gdn_kernel(q_ref, k_ref, v_ref, kt_ref, gat_ref, g_ref, y_ref,
                u_s, wq_s, at_s, kdt_s, egl_s, s_s, o_s):
    lt, l = q_ref.shape[0], y_ref.shape[0]
    n_steps = lt // PAIR
    n_ctx = (lt - l) // PAIR
    per_iter = 2
    assert n_steps % per_iter == 0

    ri = lax.broadcasted_iota(jnp.int32, (PAIR, PAIR), 0)
    ci = lax.broadcasted_iota(jnp.int32, (PAIR, PAIR), 1)
    same = (ri // CHUNK) == (ci // CHUNK)
    incl = (jnp.where(same & (ri >= ci), 1.0, 0.0), jnp.where(same & (ri <= ci), 1.0, 0.0))
    strict = (jnp.where(same & (ri > ci), 1.0, 0.0), jnp.where(same & (ri < ci), 1.0, 0.0))
    tri = (incl[1], incl[0])
    eye = jnp.where(ri == ci, 1.0, 0.0)
    eye2 = jnp.concatenate([eye, eye], axis=1)
    off = ([], [])
    for lvl in range(6):
        b = 2 ** lvl
        joined = (ri // (2 * b)) == (ci // (2 * b))
        off[0].append(jnp.where(joined & (ri % (2 * b) >= b) & (ci % (2 * b) < b), 1.0, 0.0))
        off[1].append(jnp.where(joined & (ri % (2 * b) < b) & (ci % (2 * b) >= b), 1.0, 0.0))

    def joining_blocks(a, d, lvl):
        m = off[d][lvl]
        return jnp.concatenate([a[:, :HEAD_D] * m, a[:, HEAD_D:] * m], axis=1)

    lane = lax.broadcasted_iota(jnp.int32, (1, PAIR), 1)
    first_half = lane < CHUNK
    scale = HEAD_D ** -0.5
    hi = lax.Precision.HIGHEST
    lanes = [slice(hp * 2 * HEAD_D, (hp + 1) * 2 * HEAD_D) for hp in range(HEADS // 2)]
    stream_ids = [(hp, d) for hp in range(HEADS // 2) for d in range(2)]
    zeros_half = jnp.zeros((CHUNK, 2 * HEAD_D), BF16)

    def pair_of(step, d):
        if d == 0:
            return lax.rem(step + (n_steps - n_ctx), n_steps)
        return n_steps - 1 - step

    def prepare(slot, step0):
        streams = []
        for kk in range(per_iter):
            for d in range(2):
                p = pair_of(step0 + kk, d)
                t0 = pl.multiple_of(p * PAIR, PAIR)
                gates = gat_ref[p]
                kt_all = kt_ref[p]
                gc = jnp.dot(gates, tri[d], precision=hi, preferred_element_type=F32)
                for hp in range(HEADS // 2):
                    k2 = k_ref[pl.ds(t0, PAIR), lanes[hp]]
                    q2 = q_ref[pl.ds(t0, PAIR), lanes[hp]]
                    v2 = v_ref[pl.ds(t0, PAIR), lanes[hp]]
                    kt2 = kt_all[hp * 2 * HEAD_D:(hp + 1) * 2 * HEAD_D, :]
                    bd_kt = _bd2(kt2[:HEAD_D], kt2[HEAD_D:])
                    streams.append(dict(kk=kk, d=d, hp=hp, gates=gates, gc=gc, k2=k2, q2=q2, v2=v2,
                                        kt2=kt2, kk_=_dot(k2, bd_kt), qk_=_dot(q2, bd_kt)))
        yield
        for st in streams:
            kk, d, hp, gates, gc = st["kk"], st["d"], st["hp"], st["gates"], st["gc"]
            gc_t = gc.T
            a, attn, qd, kdt, erow = [], [], [], [], []
            for hh in range(2):
                h = 2 * hp + hh
                hs = slice(hh * HEAD_D, (hh + 1) * HEAD_D)
                brow = gates[4 * d + h:4 * d + h + 1, :]
                gcr = gc[8 + 4 * d + h:9 + 4 * d + h, :]
                gcc = gc_t[:, 8 + 4 * d + h:9 + 4 * d + h]
                decay = jnp.exp((gcc - gcr) * incl[d]) * incl[d]
                a.append(st["kk_"][:, hs] * decay * strict[d] * brow)
                attn.append(st["qk_"][:, hs] * decay * (brow * scale))
                qd.append(st["q2"][:, hs].astype(F32) * (jnp.exp(gcc) * scale))
                if d == 0:
                    gl0, gl1 = gcr[:, CHUNK - 1:CHUNK], gcr[:, PAIR - 1:PAIR]
                else:
                    gl0, gl1 = gcr[:, 0:1], gcr[:, CHUNK:CHUNK + 1]
                glr = jnp.where(first_half, gl0, gl1)
                kdt.append(st["kt2"][hs, :].astype(F32) * (jnp.exp(glr - gcr) * brow))
                erow.append(jnp.exp(gcr))
                egl_s[slot, d, kk, 0, h:h + 1, :] = jnp.broadcast_to(jnp.exp(gl0), (1, PAIR))
                egl_s[slot, d, kk, 1, h:h + 1, :] = jnp.broadcast_to(jnp.exp(gl1), (1, PAIR))
            st["a"] = jnp.concatenate(a, axis=1)
            st["erow"] = jnp.concatenate(erow, axis=1)
            at_s[slot, d, kk, :, lanes[hp]] = jnp.concatenate(attn, axis=1).astype(BF16)
            kdt_s[slot, d, kk, hp] = jnp.concatenate(kdt, axis=1).astype(BF16)
            st["qd"] = jnp.concatenate(qd, axis=1).astype(BF16)
            st["t"] = eye2 - joining_blocks(st["a"], d, 0)
        yield
        for lvl in range(1, 6):
            for st in streams:
                st["y"] = _dot(joining_blocks(st["a"], st["d"], lvl).astype(BF16), _bd_of(st["t"]))
            yield
            for st in streams:
                st["t"] = st["t"] - _dot(st["t"].astype(BF16), _bd_of(st["y"]))
            yield
        for st in streams:
            t = st["t"]
            st["u"] = _dot(t.astype(BF16), _bd2(st["v2"][:, :HEAD_D], st["v2"][:, HEAD_D:]))
            st["w"] = _dot((t * st["erow"]).astype(BF16), _bd2(st["k2"][:, :HEAD_D], st["k2"][:, HEAD_D:]))
        yield
        for st in streams:
            kk, d, hp = st["kk"], st["d"], st["hp"]
            u_s[slot, d, kk, :, lanes[hp]] = st["u"].astype(BF16)
            wb, qdb = st["w"].astype(BF16), st["qd"]
            wq_s[slot, d, kk, :, lanes[hp]] = jnp.concatenate(
                [wb[:CHUNK], qdb[:CHUNK], wb[CHUNK:], qdb[CHUNK:]], axis=0)
        yield

    def scan(slot, step0):
        state = {(hp, d): s_s[d, :, lanes[hp]] for (hp, d) in stream_ids}
        for kk in range(per_iter):
            for half in range(2):
                e = {0: half, 1: 1 - half}
                res, bd_x = {}, {}
                for (hp, d) in stream_ids:
                    res[(hp, d)] = _dot(wq_s[slot, d, kk, e[d] * PAIR:(e[d] + 1) * PAIR, lanes[hp]],
                                        _bd_of(state[(hp, d)]))
                yield
                for (hp, d) in stream_ids:
                    vn = (u_s[slot, d, kk, e[d] * CHUNK:(e[d] + 1) * CHUNK, lanes[hp]].astype(F32)
                          - res[(hp, d)][:CHUNK])
                    vnb = vn.astype(BF16)
                    x = jnp.concatenate([vnb, zeros_half] if e[d] == 0 else [zeros_half, vnb], axis=0)
                    bd_x[(hp, d)] = _bd2(x[:, :HEAD_D], x[:, HEAD_D:])
                for (hp, d) in stream_ids:
                    eg = jnp.concatenate([egl_s[slot, d, kk, e[d], 2 * hp:2 * hp + 1, :],
                                          egl_s[slot, d, kk, e[d], 2 * hp + 1:2 * hp + 2, :]], axis=1)
                    state[(hp, d)] = state[(hp, d)] * eg + _dot(kdt_s[slot, d, kk, hp], bd_x[(hp, d)])
                for (hp, d) in stream_ids:
                    r0 = pl.multiple_of(pair_of(step0 + kk, d) * PAIR + e[d] * CHUNK, CHUNK)
                    o = res[(hp, d)][CHUNK:] + _dot(
                        at_s[slot, d, kk, e[d] * CHUNK:(e[d] + 1) * CHUNK, lanes[hp]], bd_x[(hp, d)])
                    o_s[pl.ds(r0, CHUNK), lanes[hp]] += o
                yield
        for (hp, d) in stream_ids:
            s_s[d, :, lanes[hp]] = state[(hp, d)]

    s_s[...] = jnp.zeros(s_s.shape, F32)
    o_s[...] = jnp.zeros(o_s.shape, F32)
    n_iter = n_steps // per_iter
    _round_robin(prepare(0, 0))

    def body(it, carry):
        _round_robin(prepare(lax.rem(it + 1, 2), (it + 1) * per_iter), scan(lax.rem(it, 2), it * per_iter))
        return carry

    lax.fori_loop(0, n_iter - 1, body, 0)
    _round_robin(scan((n_iter - 1) % 2, (n_iter - 1) * per_iter))

    def epilogue(rb, carry):
        r0 = pl.multiple_of(rb * ROW_BLOCK, ROW_BLOCK)
        o = o_s[pl.ds(r0, ROW_BLOCK), :]
        segs = [_rms(o[:, h * HEAD_D:(h + 1) * HEAD_D]) * g_ref[...] for h in range(HEADS)]
        y_ref[pl.ds(r0, ROW_BLOCK), :] = jnp.concatenate(segs, axis=1).astype(BF16)
        return carry

    lax.fori_loop(0, l // ROW_BLOCK, epilogue, 0)


def _gdn(pg, kt, gat, g_gdn, l):
    b, lt, _ = pg.shape
    n_pairs = lt // PAIR

    def group(g):
        return pl.BlockSpec((None, lt, GROUP_W), lambda i, g=g: (i, 0, g))

    ring = (2, 2, 2)
    return pl.pallas_call(
        _gdn_kernel,
        grid=(b,),
        in_specs=[group(1), group(2), group(3),
                  pl.BlockSpec((None, n_pairs, GROUP_W, PAIR), lambda i: (i, 0, 0, 0)),
                  pl.BlockSpec((None, n_pairs, N_GATES, PAIR), lambda i: (i, 0, 0, 0)),
                  pl.BlockSpec((1, HEAD_D), lambda i: (0, 0))],
        out_specs=pl.BlockSpec((None, l, GROUP_W), lambda i: (i, 0, 0)),
        out_shape=jax.ShapeDtypeStruct((b, l, GROUP_W), BF16),
        scratch_shapes=[pltpu.VMEM(ring + (PAIR, GROUP_W), BF16),
                        pltpu.VMEM(ring + (2 * PAIR, GROUP_W), BF16),
                        pltpu.VMEM(ring + (PAIR, GROUP_W), BF16),
                        pltpu.VMEM(ring + (HEADS // 2, HEAD_D, 2 * PAIR), BF16),
                        pltpu.VMEM(ring + (2, 8, PAIR), F32),
                        pltpu.VMEM((2, HEAD_D, GROUP_W), F32),
                        pltpu.VMEM((lt, GROUP_W), F32)],
        compiler_params=_params(48, ("arbitrary",)),
        name="gdn",
    )(pg, pg, pg, kt, gat, g_gdn)


def _fourier_w_kernel(cc_ref, sc_ref, w_ref, o_ref):
    hi = lax.Precision.HIGHEST
    for g in range(w_ref.shape[0]):
        w = w_ref[g]
        o_ref[g] = jnp.concatenate(
            [jnp.dot(cc_ref[...], w, precision=hi, preferred_element_type=F32),
             -jnp.dot(sc_ref[...], w, precision=hi, preferred_element_type=F32)], axis=1)


def _fourier_w(cc, sc, w_f):
    groups, c, d = w_f.shape
    return pl.pallas_call(
        _fourier_w_kernel,
        out_shape=jax.ShapeDtypeStruct((groups, c, 2 * d), F32),
        name="fourier_w",
    )(cc, sc, w_f)


@functools.lru_cache(maxsize=None)
def _dft_constants(length, channels):
    def cos_sin(n, keep):
        idx = np.arange(keep, dtype=np.int64)
        ang = (np.outer(idx, idx) % n).astype(np.float64) * (2.0 * np.pi / n)
        return np.cos(ang), np.sin(ang)
    c_l, s_l = cos_sin(length, length // 2)
    c_c, s_c = cos_sin(channels, channels)
    norm = 1.0 / math.sqrt(length * channels)
    return (jnp.asarray(c_l.astype(np.float32)), jnp.asarray(s_l.astype(np.float32)),
            jnp.asarray((c_c * norm).astype(np.float32)),
            jnp.asarray((s_c * norm).astype(np.float32)))


def _mix_kernel(u_ref, on_ref, z_ref, ch_ref, sh_ref, mcs_ref, wout_ref, x_ref, gpost_ref, gt_ref,
                gpre_ref, shift_ref, scale_ref, x1_ref, h2_ref,
                yf_s, ch_s, sh_s, mc_s, ms_s, wout_s, ve_s, vo_s, d_s):
    i, m = pl.program_id(0), pl.program_id(1)
    l = u_ref.shape[0]
    h = l // 2
    blk = PAIR
    n_blk = h // blk

    @pl.when((i == 0) & (m == 0))
    def _():
        ch_s[...] = ch_ref[...].astype(BF16)
        sh_s[...] = sh_ref[...].astype(BF16)
        wout_s[...] = wout_ref[...].astype(BF16)
        mc_s[...] = jnp.zeros(mc_s.shape, BF16)
        ms_s[...] = jnp.zeros(ms_s.shape, BF16)
        for g in range(HEADS):
            gs = slice(g * HEAD_D, (g + 1) * HEAD_D)
            mc_s[gs, gs] = mcs_ref[g][:, :HEAD_D].astype(BF16)
            ms_s[gs, gs] = mcs_ref[g][:, HEAD_D:].astype(BF16)

    @pl.when(m == 0)
    def _():
        ri = lax.broadcasted_iota(jnp.int32, (blk, 2 * blk), 0)
        ci = lax.broadcasted_iota(jnp.int32, (blk, 2 * blk), 1)
        rev = jnp.where(((ci < blk) & (ri + ci == blk)) | ((ri == 0) & (ci == blk)), 1.0, 0.0).astype(BF16)
        zero_blk = jnp.zeros((blk, GROUP_W), BF16)

        def reversed_block(ref, base, k, last):
            top = ref[base + (n_blk - 1 - k) * blk:base + (n_blk - k) * blk, :]
            bottom = last if k == 0 else ref[base + (n_blk - k) * blk:base + (n_blk - k + 1) * blk, :]
            return _dot(rev, jnp.concatenate([top, bottom], axis=0))

        for k in range(n_blk):
            low = u_ref[k * blk:(k + 1) * blk, :].astype(F32)
            mirrored = reversed_block(u_ref, h, k, zero_blk)
            ve_s[k * blk:(k + 1) * blk, :] = _dot((low + mirrored).astype(BF16), mc_s[...]).astype(BF16)
            vo_s[k * blk:(k + 1) * blk, :] = _dot((low - mirrored).astype(BF16), ms_s[...]).astype(BF16)
        sign_row = jnp.where(lax.broadcasted_iota(jnp.int32, (16, l), 1) % 2 == 0, 1.0, -1.0)
        sign_row = jnp.where(lax.broadcasted_iota(jnp.int32, (16, l), 0) == 0, sign_row, 0.0).astype(BF16)
        alt_u = _dot(sign_row, u_ref[...])
        y_nyq = _dot(alt_u.astype(BF16), mc_s[...])[0:1, :]
        v_nyq = _dot(u_ref[h:h + 16, :], mc_s[...])[0:1, :]
        sign_col = jnp.where(lax.broadcasted_iota(jnp.int32, (blk, GROUP_W), 0) % 2 == 0, 1.0, -1.0)
        first_row = lax.broadcasted_iota(jnp.int32, (blk, GROUP_W), 0) == 0
        d_s[h:h + blk, :] = jnp.where(first_row, y_nyq, 0.0).astype(BF16)
        for k in range(n_blk):
            rows = slice(k * blk, (k + 1) * blk)
            a = _dot(ch_s[rows, :], ve_s[...]) + sign_col * v_nyq
            b = _dot(sh_s[rows, :], vo_s[...])
            yf_s[rows, :] = (a + b).astype(BF16)
            d_s[rows, :] = (a - b).astype(BF16)
        for k in range(n_blk):
            yf_s[h + k * blk:h + (k + 1) * blk, :] = reversed_block(
                d_s, 0, k, d_s[h:h + blk, :]).astype(BF16)

    tm = x_ref.shape[0]
    half = wout_s.shape[0] // 2
    subs = [slice(sb * ROW_BLOCK, (sb + 1) * ROW_BLOCK) for sb in range(tm // ROW_BLOCK)]
    ys = []
    for sb, rs in enumerate(subs):
        r0 = pl.multiple_of(m * tm + sb * ROW_BLOCK, ROW_BLOCK)
        yd = (on_ref[rs, :].astype(F32) * z_ref[rs, :].astype(F32)).astype(BF16)
        ys.append(_dot(yf_s[pl.ds(r0, ROW_BLOCK), :], wout_s[:half, :]) + _dot(yd, wout_s[half:, :]))
    for rs, y in zip(subs, ys):
        x1 = x_ref[rs, :] + gt_ref[...] * (_rms(y) * gpost_ref[...])
        x1_ref[rs, :] = x1
        h2 = _rms(x1) * gpre_ref[...]
        h2_ref[rs, :] = (h2 * (1.0 + scale_ref[...]) + shift_ref[...]).astype(BF16)


def _mix(pg, on, ch, sh, mcs, w_out, x, g_post, gt, g_pre, shift, scale):
    b, l, d = x.shape
    tm = MIX_TILE
    h = l // 2
    vec = pl.BlockSpec((1, d), lambda i, m: (0, 0))
    mod = pl.BlockSpec((None, 1, d), lambda i, m: (i, 0, 0))
    whole = lambda a: pl.BlockSpec(a.shape, lambda i, m: (0,) * a.ndim)
    return pl.pallas_call(
        _mix_kernel,
        grid=(b, l // tm),
        in_specs=[pl.BlockSpec((None, l, GROUP_W), lambda i, m: (i, 0, 0)),
                  pl.BlockSpec((None, tm, GROUP_W), lambda i, m: (i, m, 0)),
                  pl.BlockSpec((None, tm, GROUP_W), lambda i, m: (i, m, N_GROUPS - 1)),
                  whole(ch), whole(sh), whole(mcs), whole(w_out),
                  pl.BlockSpec((None, tm, d), lambda i, m: (i, m, 0)),
                  vec, mod, vec, mod, mod],
        out_specs=[pl.BlockSpec((None, tm, d), lambda i, m: (i, m, 0)),
                   pl.BlockSpec((None, tm, d), lambda i, m: (i, m, 0))],
        out_shape=[jax.ShapeDtypeStruct((b, l, d), F32),
                   jax.ShapeDtypeStruct((b, l, d), BF16)],
        scratch_shapes=[pltpu.VMEM((l, GROUP_W), BF16),
                        pltpu.VMEM((h, h), BF16),
                        pltpu.VMEM((h, h), BF16),
                        pltpu.VMEM((GROUP_W, GROUP_W), BF16),
                        pltpu.VMEM((GROUP_W, GROUP_W), BF16),
                        pltpu.VMEM(w_out.shape, BF16),
                        pltpu.VMEM((h, GROUP_W), BF16),
                        pltpu.VMEM((h, GROUP_W), BF16),
                        pltpu.VMEM((h + PAIR, GROUP_W), BF16)],
        compiler_params=_params(48, ("arbitrary", "arbitrary")),
        name="mix",
    )(pg, on, pg, ch, sh, mcs, w_out, x, g_post, gt, g_pre, shift, scale)


def _ffn_kernel(h_ref, x1_ref, wv_ref, wg_ref, dwc_ref, wd_ref, gpost_ref, gt_ref,
                o_ref, gate_s, val_s, wvg_s, wd_s):
    j = pl.program_id(1)
    l = h_ref.shape[0]
    tf = wv_ref.shape[1]
    n_rb = l // ROW_BLOCK
    pad = GRID_W

    wvg_s[:, :tf] = wv_ref[...].astype(BF16)
    wvg_s[:, tf:] = wg_ref[...].astype(BF16)
    wd_s[...] = wd_ref[...].astype(BF16)
    gate_s[0:pad, :] = jnp.zeros((pad, tf), F32)
    gate_s[l + pad:l + 2 * pad, :] = jnp.zeros((pad, tf), F32)

    @pl.when(j == 0)
    def _():
        o_ref[...] = jnp.zeros(o_ref.shape, F32)

    n_ext = ROW_BLOCK + 2 * pad
    col = lax.broadcasted_iota(jnp.int32, (n_ext, tf), 0) % GRID_W
    not_first = jnp.where(col == 0, 0.0, 1.0)
    not_last = jnp.where(col == GRID_W - 1, 0.0, 1.0)
    dw = dwc_ref[...]

    def up(i):
        r0 = i * ROW_BLOCK
        vg = _dot(h_ref[r0:r0 + ROW_BLOCK, :], wvg_s[...])
        val_s[r0:r0 + ROW_BLOCK, :] = vg[:, :tf]
        gate_s[r0 + pad:r0 + pad + ROW_BLOCK, :] = vg[:, tf:]

    def down(i):
        r0 = i * ROW_BLOCK
        ext = gate_s[r0:r0 + n_ext, :]
        shifted = (pltpu.roll(ext, 1, 0) * not_first, ext, pltpu.roll(ext, n_ext - 1, 0) * not_last)
        conv = None
        for dy in range(3):
            for dx in range(3):
                term = shifted[dx][dy * pad:dy * pad + ROW_BLOCK] * dw[3 * dy + dx:3 * dy + dx + 1]
                conv = term if conv is None else conv + term
        act = (_silu(conv) * val_s[r0:r0 + ROW_BLOCK, :]).astype(BF16)
        o_ref[r0:r0 + ROW_BLOCK, :] += _dot(act, wd_s[...])

    up(0)
    up(1)
    for i in range(n_rb):
        if i + 2 < n_rb:
            up(i + 2)
        down(i)

    @pl.when(j == pl.num_programs(1) - 1)
    def _():
        def fin(rb, carry):
            r0 = pl.multiple_of(rb * ROW_BLOCK, ROW_BLOCK)
            y = _rms(o_ref[pl.ds(r0, ROW_BLOCK), :]) * gpost_ref[...]
            o_ref[pl.ds(r0, ROW_BLOCK), :] = x1_ref[pl.ds(r0, ROW_BLOCK), :] + gt_ref[...] * y
            return carry
        lax.fori_loop(0, n_rb, fin, 0)


def _ffn(h2, x1, w_up, dwc, w_down, g_post, gt):
    b, l, d = x1.shape
    d_ff = w_down.shape[0]
    tf = FFN_TILE
    n_f = d_ff // tf
    return pl.pallas_call(
        _ffn_kernel,
        grid=(b, n_f),
        in_specs=[pl.BlockSpec((None, l, d), lambda i, j: (i, 0, 0)),
                  pl.BlockSpec((None, l, d), lambda i, j: (i, 0, 0)),
                  pl.BlockSpec((d, tf), lambda i, j: (0, j)),
                  pl.BlockSpec((d, tf), lambda i, j: (0, n_f + j)),
                  pl.BlockSpec((9, tf), lambda i, j: (0, j)),
                  pl.BlockSpec((tf, d), lambda i, j: (j, 0)),
                  pl.BlockSpec((1, d), lambda i, j: (0, 0)),
                  pl.BlockSpec((None, 1, d), lambda i, j: (i, 0, 0))],
        out_specs=pl.BlockSpec((None, l, d), lambda i, j: (i, 0, 0)),
        out_shape=jax.ShapeDtypeStruct((b, l, d), F32),
        scratch_shapes=[pltpu.VMEM((l + 2 * GRID_W, tf), F32),
                        pltpu.VMEM((l, tf), F32),
                        pltpu.VMEM((d, 2 * tf), BF16),
                        pltpu.VMEM((tf, d), BF16)],
        compiler_params=_params(58, ("arbitrary", "arbitrary")),
        name="ffn",
    )(h2, x1, w_up, w_up, dwc, w_down, g_post, gt)


def kernel(x, c, ctx, c_ctx, w_ada, b_ada, g_pre_mix, g_post_mix, g_pre_ffn, g_post_ffn,
           w_in, w_qkv_conv, a_log, dt_bias, g_gdn, w_fourier, w_out, w_up, w_dwc, w_down):
    b, l, d = x.shape
    assert w_ada.shape[0] == 1, "single-layer stack"
    assert b < 16 and l % ROW_BLOCK == 0 and ctx.shape[1] % ROW_BLOCK == 0
    assert w_in.shape[2] == N_GROUPS * GROUP_W + N_GATES and d == 2 * GROUP_W

    cc = jnp.zeros((16, d), F32).at[:b].set(c).at[b].set(c_ctx)
    mod = _ada(cc, w_ada[0], b_ada)
    sh1, sc1, gt1, sh2, sc2, gt2 = [m[:, None, :] for m in jnp.split(mod[:b], 6, axis=-1)]
    sh1c, sc1c = [m[None, :] for m in jnp.split(mod[b], 6, axis=-1)[:2]]

    wabt = w_in[0][:, N_GROUPS * GROUP_W:].T
    zeros8 = jnp.zeros((8,), F32)
    alog = jnp.concatenate([zeros8, a_log[0].reshape(-1)])[:, None]
    dtb = jnp.concatenate([zeros8, dt_bias[0].reshape(-1)])[:, None]

    pg, kt, gat = _inproj(x, ctx, g_pre_mix, sh1, sc1, sh1c, sc1c, w_in[0], wabt, w_qkv_conv[0], alog, dtb)
    on = _gdn(pg, kt, gat, g_gdn, l)

    ch, sh, c_c, s_c = _dft_constants(l, HEAD_D)
    mcs = _fourier_w(c_c, s_c, w_fourier[0])
    x1, h2 = _mix(pg, on, ch, sh, mcs, w_out[0], x, g_post_mix, gt1, g_pre_ffn, sh2, sc2)
    d_ff = w_down.shape[1]
    return _ffn(h2, x1, w_up[0], w_dwc[0].reshape(9, d_ff), w_down[0], g_post_ffn, gt2)
```

```python
import functools
import itertools
import math

import numpy as np
import jax
import jax.numpy as jnp
from jax import lax
from jax.experimental import pallas as pl
from jax.experimental.pallas import tpu as pltpu

F32 = jnp.float32
BF16 = jnp.bfloat16

GRID_W = 64
HEADS = 4
HEAD_D = 128
CHUNK = 64
PAIR = 2 * CHUNK
GROUP_W = HEADS * HEAD_D
N_GROUPS = 5
N_GATES = 16
EPS = 1e-6
ROW_BLOCK = 256
FFN_TILE = 256
MIX_TILE = 512
MIB = 1024 * 1024

_NT = (((1,), (1,)), ((), ()))


def _dot(a, b):
    return jnp.dot(a, b, preferred_element_type=F32)


def _silu(x):
    return x / (1.0 + jnp.exp(-x))


def _rms(x):
    return x * lax.rsqrt(jnp.mean(x * x, axis=-1, keepdims=True) + EPS)


def _params(vmem_mib, semantics):
    return pltpu.CompilerParams(dimension_semantics=semantics,
                                vmem_limit_bytes=vmem_mib * MIB)


def _ada_kernel(c_ref, w_ref, b_ref, o_ref):
    s = _silu(c_ref[...]).astype(BF16)
    o_ref[...] = _dot(s, w_ref[...].astype(BF16)) + b_ref[...]


def _ada(cc, w, b):
    rows, d = cc.shape
    n = w.shape[1]
    tn = 1024
    return pl.pallas_call(
        _ada_kernel,
        grid=(n // tn,),
        in_specs=[pl.BlockSpec((rows, d), lambda j: (0, 0)),
                  pl.BlockSpec((d, tn), lambda j: (0, j)),
                  pl.BlockSpec((1, tn), lambda j: (0, j))],
        out_specs=pl.BlockSpec((rows, tn), lambda j: (0, j)),
        out_shape=jax.ShapeDtypeStruct((rows, n), F32),
        compiler_params=_params(32, ("arbitrary",)),
        name="ada",
    )(cc, w, b)


def _inproj_kernel(x_ref, ctx_ref, g_ref, sh_ref, sc_ref, shc_ref, scc_ref, w_ref, wab_ref,
                   conv_ref, alog_ref, dtb_ref, pg_ref, kt_ref, gat_ref, h_s, w_s, p_s):
    j = pl.program_id(1)
    l, lc = x_ref.shape[0], ctx_ref.shape[0]
    slabs = ROW_BLOCK // PAIR
    n_ext = ROW_BLOCK + 16
    blocks = [(x_ref, r, r, r + 8, sh_ref, sc_ref) for r in range(0, l, ROW_BLOCK)]
    blocks += [(ctx_ref, r, l + r, l + r + 16, shc_ref, scc_ref) for r in range(0, lc, ROW_BLOCK)]
    n_rb = len(blocks)

    w_s[...] = w_ref[...].astype(BF16)

    def out_rows(rb):
        return slice(blocks[rb][2], blocks[rb][2] + ROW_BLOCK)

    def project(rb):
        return _dot(h_s[out_rows(rb), :], w_s[...])

    def pipelined(first, second, lead):
        for rb in range(min(lead, n_rb)):
            first(rb)
        for rb in range(n_rb):
            if rb + lead < n_rb:
                first(rb + lead)
            second(rb)

    @pl.when(j == 0)
    def _u():
        for r in (0, l + 8, l + lc + 16):
            p_s[r:r + 8, :] = jnp.zeros((8, GROUP_W), F32)
        wabt = wab_ref[...].T[:N_GATES].astype(BF16)
        row = lax.broadcasted_iota(jnp.int32, (N_GATES, PAIR), 0)

        def prologue(rb):
            src, r, dst, _, shift, scale = blocks[rb]
            h = _rms(src[r:r + ROW_BLOCK, :]) * g_ref[...]
            hb = (h * (1.0 + scale[...]) + shift[...]).astype(BF16)
            h_s[dst:dst + ROW_BLOCK, :] = hb
            for s in range(slabs):
                ab = lax.dot_general(wabt, hb[s * PAIR:(s + 1) * PAIR], _NT,
                                     preferred_element_type=F32)
                beta = 1.0 / (1.0 + jnp.exp(-ab))
                xs = ab + dtb_ref[...]
                softplus = jnp.maximum(xs, 0.0) + jnp.log1p(jnp.exp(-jnp.abs(xs)))
                gate = -jnp.exp(alog_ref[...]) * softplus
                gat_ref[dst // PAIR + s] = jnp.where(row < 8, beta, gate)

        def store_u(rb):
            pg_ref[out_rows(rb), :] = project(rb).astype(BF16)

        pipelined(prologue, store_u, 2)

    @pl.when(j == N_GROUPS - 1)
    def _z():
        for rb in range(n_rb):
            pg_ref[out_rows(rb), :] = _silu(project(rb)).astype(BF16)

    def conv_group(normalise, transpose):
        cw = conv_ref[...]

        def raw(rb):
            p0 = blocks[rb][3]
            p_s[p0:p0 + ROW_BLOCK, :] = project(rb)

        def finish(rb):
            p0 = blocks[rb][3]
            ext = p_s[p0 - 8:p0 - 8 + n_ext, :]
            prev = pltpu.roll(ext, 1, 0)[8:8 + ROW_BLOCK]
            nxt = pltpu.roll(ext, n_ext - 1, 0)[8:8 + ROW_BLOCK]
            cur = ext[8:8 + ROW_BLOCK]
            a = _silu(prev * cw[0:1] + cur * cw[1:2] + nxt * cw[2:3])
            if normalise:
                segs = []
                for hh in range(HEADS):
                    seg = a[:, hh * HEAD_D:(hh + 1) * HEAD_D]
                    segs.append(seg * lax.rsqrt(jnp.sum(seg * seg, axis=-1, keepdims=True) + EPS))
                a = jnp.concatenate(segs, axis=1)
            pg_ref[out_rows(rb), :] = a.astype(BF16)
            if transpose:
                at = a.T.astype(BF16)
                for s in range(slabs):
                    kt_ref[blocks[rb][2] // PAIR + s] = at[:, s * PAIR:(s + 1) * PAIR]

        pipelined(raw, finish, 2)

    @pl.when(j == 1)
    def _q():
        conv_group(True, False)

    @pl.when(j == 2)
    def _k():
        conv_group(True, True)

    @pl.when(j == 3)
    def _v():
        conv_group(False, False)


def _inproj(x, ctx, g, shift, scale, shift_c, scale_c, w_in, wab, conv, alog, dtb):
    b, l, d = x.shape
    lt = l + ctx.shape[1]
    n_pairs = lt // PAIR
    vec = pl.BlockSpec((1, d), lambda i, j: (0, 0))
    mod = pl.BlockSpec((None, 1, d), lambda i, j: (i, 0, 0))
    return pl.pallas_call(
        _inproj_kernel,
        grid=(b, N_GROUPS),
        in_specs=[pl.BlockSpec((None, l, d), lambda i, j: (i, 0, 0)),
                  pl.BlockSpec((None, lt - l, d), lambda i, j: (i, 0, 0)),
                  vec, mod, mod, vec, vec,
                  pl.BlockSpec((d, GROUP_W), lambda i, j: (0, j)),
                  pl.BlockSpec((d, PAIR), lambda i, j: (0, 0)),
                  pl.BlockSpec((3, GROUP_W), lambda i, j: (0, jnp.clip(j - 1, 0, 2))),
                  pl.BlockSpec((N_GATES, 1), lambda i, j: (0, 0)),
                  pl.BlockSpec((N_GATES, 1), lambda i, j: (0, 0))],
        out_specs=[pl.BlockSpec((None, lt, GROUP_W), lambda i, j: (i, 0, j)),
                   pl.BlockSpec((None, n_pairs, GROUP_W, PAIR), lambda i, j: (i, 0, 0, 0)),
                   pl.BlockSpec((None, n_pairs, N_GATES, PAIR), lambda i, j: (i, 0, 0, 0))],
        out_shape=[jax.ShapeDtypeStruct((b, lt, N_GROUPS * GROUP_W), BF16),
                   jax.ShapeDtypeStruct((b, n_pairs, GROUP_W, PAIR), BF16),
                   jax.ShapeDtypeStruct((b, n_pairs, N_GATES, PAIR), F32)],
        scratch_shapes=[pltpu.VMEM((lt, d), BF16),
                        pltpu.VMEM((d, GROUP_W), BF16),
                        pltpu.VMEM((lt + 24, GROUP_W), F32)],
        compiler_params=_params(52, ("arbitrary", "arbitrary")),
        name="inproj",
    )(x, ctx, g, shift, scale, shift_c, scale_c, w_in, wab, conv, alog, dtb)


def _bd2(a, b):
    z = jnp.zeros_like(a)
    return jnp.concatenate([jnp.concatenate([a, z], axis=1),
                            jnp.concatenate([z, b], axis=1)], axis=0)


def _bd_of(x):
    xb = x.astype(BF16)
    return _bd2(xb[:, :HEAD_D], xb[:, HEAD_D:])


def _round_robin(*stage_generators):
    for _ in itertools.zip_longest(*stage_generators):
        pass


def _gdn_kernel(q_ref, k_ref, v_ref, kt_ref, gat_ref, g_ref, y_ref,
                u_s, wq_s, at_s, kdt_s, egl_s, s_s, o_s):
    lt, l = q_ref.shape[0], y_ref.shape[0]
    n_steps = lt // PAIR
    n_ctx = (lt - l) // PAIR
    per_iter = 2
    assert n_steps % per_iter == 0

    ri = lax.broadcasted_iota(jnp.int32, (PAIR, PAIR), 0)
    ci = lax.broadcasted_iota(jnp.int32, (PAIR, PAIR), 1)
    same = (ri // CHUNK) == (ci // CHUNK)
    incl = (jnp.where(same & (ri >= ci), 1.0, 0.0), jnp.where(same & (ri <= ci), 1.0, 0.0))
    strict = (jnp.where(same & (ri > ci), 1.0, 0.0), jnp.where(same & (ri < ci), 1.0, 0.0))
    tri = (incl[1], incl[0])
    eye = jnp.where(ri == ci, 1.0, 0.0)
    eye2 = jnp.concatenate([eye, eye], axis=1)
    off = ([], [])
    for lvl in range(6):
        b = 2 ** lvl
        joined = (ri // (2 * b)) == (ci // (2 * b))
        off[0].append(jnp.where(joined & (ri % (2 * b) >= b) & (ci % (2 * b) < b), 1.0, 0.0))
        off[1].append(jnp.where(joined & (ri % (2 * b) < b) & (ci % (2 * b) >= b), 1.0, 0.0))

    def joining_blocks(a, d, lvl):
        m = off[d][lvl]
        return jnp.concatenate([a[:, :HEAD_D] * m, a[:, HEAD_D:] * m], axis=1)

    lane = lax.broadcasted_iota(jnp.int32, (1, PAIR), 1)
    first_half = lane < CHUNK
    scale = HEAD_D ** -0.5
    hi = lax.Precision.HIGHEST
    lanes = [slice(hp * 2 * HEAD_D, (hp + 1) * 2 * HEAD_D) for hp in range(HEADS // 2)]
    stream_ids = [(hp, d) for hp in range(HEADS // 2) for d in range(2)]
    zeros_half = jnp.zeros((CHUNK, 2 * HEAD_D), BF16)

    def pair_of(step, d):
        if d == 0:
            return lax.rem(step + (n_steps - n_ctx), n_steps)
        return n_steps - 1 - step

    def prepare(slot, step0):
        streams = []
        for kk in range(per_iter):
            for d in range(2):
                p = pair_of(step0 + kk, d)
                t0 = pl.multiple_of(p * PAIR, PAIR)
                gates = gat_ref[p]
                kt_all = kt_ref[p]
                gc = jnp.dot(gates, tri[d], precision=hi, preferred_element_type=F32)
                for hp in range(HEADS // 2):
                    k2 = k_ref[pl.ds(t0, PAIR), lanes[hp]]
                    q2 = q_ref[pl.ds(t0, PAIR), lanes[hp]]
                    v2 = v_ref[pl.ds(t0, PAIR), lanes[hp]]
                    kt2 = kt_all[hp * 2 * HEAD_D:(hp + 1) * 2 * HEAD_D, :]
                    bd_kt = _bd2(kt2[:HEAD_D], kt2[HEAD_D:])
                    streams.append(dict(kk=kk, d=d, hp=hp, gates=gates, gc=gc, k2=k2, q2=q2, v2=v2,
                                        kt2=kt2, kk_=_dot(k2, bd_kt), qk_=_dot(q2, bd_kt)))
        yield
        for st in streams:
            kk, d, hp, gates, gc = st["kk"], st["d"], st["hp"], st["gates"], st["gc"]
            gc_t = gc.T
            a, attn, qd, kdt, erow = [], [], [], [], []
            for hh in range(2):
                h = 2 * hp + hh
                hs = slice(hh * HEAD_D, (hh + 1) * HEAD_D)
                brow = gates[4 * d + h:4 * d + h + 1, :]
                gcr = gc[8 + 4 * d + h:9 + 4 * d + h, :]
                gcc = gc_t[:, 8 + 4 * d + h:9 + 4 * d + h]
                decay = jnp.exp((gcc - gcr) * incl[d]) * incl[d]
                a.append(st["kk_"][:, hs] * decay * strict[d] * brow)
                attn.append(st["qk_"][:, hs] * decay * (brow * scale))
                qd.append(st["q2"][:, hs].astype(F32) * (jnp.exp(gcc) * scale))
                if d == 0:
                    gl0, gl1 = gcr[:, CHUNK - 1:CHUNK], gcr[:, PAIR - 1:PAIR]
                else:
                    gl0, gl1 = gcr[:, 0:1], gcr[:, CHUNK:CHUNK + 1]
                glr = jnp.where(first_half, gl0, gl1)
                kdt.append(st["kt2"][hs, :].astype(F32) * (jnp.exp(glr - gcr) * brow))
                erow.append(jnp.exp(gcr))
                egl_s[slot, d, kk, 0, h:h + 1, :] = jnp.broadcast_to(jnp.exp(gl0), (1, PAIR))
                egl_s[slot, d, kk, 1, h:h + 1, :] = jnp.broadcast_to(jnp.exp(gl1), (1, PAIR))
            st["a"] = jnp.concatenate(a, axis=1)
            st["erow"] = jnp.concatenate(erow, axis=1)
            at_s[slot, d, kk, :, lanes[hp]] = jnp.concatenate(attn, axis=1).astype(BF16)
            kdt_s[slot, d, kk, hp] = jnp.concatenate(kdt, axis=1).astype(BF16)
            st["qd"] = jnp.concatenate(qd, axis=1).astype(BF16)
            st["t"] = eye2 - joining_blocks(st["a"], d, 0)
        yield
        for lvl in range(1, 6):
            for st in streams:
                st["y"] = _dot(joining_blocks(st["a"], st["d"], lvl).astype(BF16), _bd_of(st["t"]))
            yield
            for st in streams:
                st["t"] = st["t"] - _dot(st["t"].astype(BF16), _bd_of(st["y"]))
            yield
        for st in streams:
            t = st["t"]
            st["u"] = _dot(t.astype(BF16), _bd2(st["v2"][:, :HEAD_D], st["v2"][:, HEAD_D:]))
            st["w"] = _dot((t * st["erow"]).astype(BF16), _bd2(st["k2"][:, :HEAD_D], st["k2"][:, HEAD_D:]))
        yield
        for st in streams:
            kk, d, hp = st["kk"], st["d"], st["hp"]
            u_s[slot, d, kk, :, lanes[hp]] = st["u"].astype(BF16)
            wb, qdb = st["w"].astype(BF16), st["qd"]
            wq_s[slot, d, kk, :, lanes[hp]] = jnp.concatenate(
                [wb[:CHUNK], qdb[:CHUNK], wb[CHUNK:], qdb[CHUNK:]], axis=0)
        yield

    def scan(slot, step0):
        state = {(hp, d): s_s[d, :, lanes[hp]] for (hp, d) in stream_ids}
        for kk in range(per_iter):
            for half in range(2):
                e = {0: half, 1: 1 - half}
                res, bd_x = {}, {}
                for (hp, d) in stream_ids:
                    res[(hp, d)] = _dot(wq_s[slot, d, kk, e[d] * PAIR:(e[d] + 1) * PAIR, lanes[hp]],
                                        _bd_of(state[(hp, d)]))
                yield
                for (hp, d) in stream_ids:
                    vn = (u_s[slot, d, kk, e[d] * CHUNK:(e[d] + 1) * CHUNK, lanes[hp]].astype(F32)
                          - res[(hp, d)][:CHUNK])
                    vnb = vn.astype(BF16)
                    x = jnp.concatenate([vnb, zeros_half] if e[d] == 0 else [zeros_half, vnb], axis=0)
                    bd_x[(hp, d)] = _bd2(x[:, :HEAD_D], x[:, HEAD_D:])
                for (hp, d) in stream_ids:
                    both = _dot(jnp.concatenate(
                        [kdt_s[slot, d, kk, hp],
                         at_s[slot, d, kk, e[d] * CHUNK:(e[d] + 1) * CHUNK, lanes[hp]]], axis=0),
                        bd_x[(hp, d)])
                    eg = jnp.concatenate([egl_s[slot, d, kk, e[d], 2 * hp:2 * hp + 1, :],
                                          egl_s[slot, d, kk, e[d], 2 * hp + 1:2 * hp + 2, :]], axis=1)
                    state[(hp, d)] = state[(hp, d)] * eg + both[:HEAD_D]
                    r0 = pl.multiple_of(pair_of(step0 + kk, d) * PAIR + e[d] * CHUNK, CHUNK)
                    o_s[pl.ds(r0, CHUNK), lanes[hp]] += res[(hp, d)][CHUNK:] + both[HEAD_D:]
                yield
        for (hp, d) in stream_ids:
            s_s[d, :, lanes[hp]] = state[(hp, d)]

    s_s[...] = jnp.zeros(s_s.shape, F32)
    o_s[...] = jnp.zeros(o_s.shape, F32)
    n_iter = n_steps // per_iter
    _round_robin(prepare(0, 0))

    def body(it, carry):
        _round_robin(prepare(lax.rem(it + 1, 2), (it + 1) * per_iter), scan(lax.rem(it, 2), it * per_iter))
        return carry

    lax.fori_loop(0, n_iter - 1, body, 0)
    _round_robin(scan((n_iter - 1) % 2, (n_iter - 1) * per_iter))

    def epilogue(rb, carry):
        r0 = pl.multiple_of(rb * ROW_BLOCK, ROW_BLOCK)
        o = o_s[pl.ds(r0, ROW_BLOCK), :]
        segs = [_rms(o[:, h * HEAD_D:(h + 1) * HEAD_D]) * g_ref[...] for h in range(HEADS)]
        y_ref[pl.ds(r0, ROW_BLOCK), :] = jnp.concatenate(segs, axis=1).astype(BF16)
        return carry

    lax.fori_loop(0, l // ROW_BLOCK, epilogue, 0)


def _gdn(pg, kt, gat, g_gdn, l):
    b, lt, _ = pg.shape
    n_pairs = lt // PAIR

    def group(g):
        return pl.BlockSpec((None, lt, GROUP_W), lambda i, g=g: (i, 0, g))

    ring = (2, 2, 2)
    return pl.pallas_call(
        _gdn_kernel,
        grid=(b,),
        in_specs=[group(1), group(2), group(3),
                  pl.BlockSpec((None, n_pairs, GROUP_W, PAIR), lambda i: (i, 0, 0, 0)),
                  pl.BlockSpec((None, n_pairs, N_GATES, PAIR), lambda i: (i, 0, 0, 0)),
                  pl.BlockSpec((1, HEAD_D), lambda i: (0, 0))],
        out_specs=pl.BlockSpec((None, l, GROUP_W), lambda i: (i, 0, 0)),
        out_shape=jax.ShapeDtypeStruct((b, l, GROUP_W), BF16),
        scratch_shapes=[pltpu.VMEM(ring + (PAIR, GROUP_W), BF16),
                        pltpu.VMEM(ring + (2 * PAIR, GROUP_W), BF16),
                        pltpu.VMEM(ring + (PAIR, GROUP_W), BF16),
                        pltpu.VMEM(ring + (HEADS // 2, HEAD_D, 2 * PAIR), BF16),
                        pltpu.VMEM(ring + (2, 8, PAIR), F32),
                        pltpu.VMEM((2, HEAD_D, GROUP_W), F32),
                        pltpu.VMEM((lt, GROUP_W), F32)],
        compiler_params=_params(48, ("arbitrary",)),
        name="gdn",
    )(pg, pg, pg, kt, gat, g_gdn)


def _fourier_w_kernel(cc_ref, sc_ref, w_ref, o_ref):
    hi = lax.Precision.HIGHEST
    for g in range(w_ref.shape[0]):
        w = w_ref[g]
        o_ref[g] = jnp.concatenate(
            [jnp.dot(cc_ref[...], w, precision=hi, preferred_element_type=F32),
             -jnp.dot(sc_ref[...], w, precision=hi, preferred_element_type=F32)], axis=1)


def _fourier_w(cc, sc, w_f):
    groups, c, d = w_f.shape
    return pl.pallas_call(
        _fourier_w_kernel,
        out_shape=jax.ShapeDtypeStruct((groups, c, 2 * d), F32),
        name="fourier_w",
    )(cc, sc, w_f)


@functools.lru_cache(maxsize=None)
def _dft_constants(length, channels):
    def cos_sin(n, keep):
        idx = np.arange(keep, dtype=np.int64)
        ang = (np.outer(idx, idx) % n).astype(np.float64) * (2.0 * np.pi / n)
        return np.cos(ang), np.sin(ang)
    c_l, s_l = cos_sin(length, length // 2)
    c_c, s_c = cos_sin(channels, channels)
    norm = 1.0 / math.sqrt(length * channels)
    return (jnp.asarray(c_l.astype(np.float32)), jnp.asarray(s_l.astype(np.float32)),
            jnp.asarray((c_c * norm).astype(np.float32)),
            jnp.asarray((s_c * norm).astype(np.float32)))


def _mix_kernel(u_ref, on_ref, z_ref, ch_ref, sh_ref, mcs_ref, wout_ref, x_ref, gpost_ref, gt_ref,
                gpre_ref, shift_ref, scale_ref, x1_ref, h2_ref,
                yf_s, ch_s, sh_s, mc_s, ms_s, wout_s, ve_s, vo_s, d_s):
    i, m = pl.program_id(0), pl.program_id(1)
    l = u_ref.shape[0]
    h = l // 2
    blk = PAIR
    n_blk = h // blk

    @pl.when((i == 0) & (m == 0))
    def _():
        ch_s[...] = ch_ref[...].astype(BF16)
        sh_s[...] = sh_ref[...].astype(BF16)
        wout_s[...] = wout_ref[...].astype(BF16)
        mc_s[...] = jnp.zeros(mc_s.shape, BF16)
        ms_s[...] = jnp.zeros(ms_s.shape, BF16)
        for g in range(HEADS):
            gs = slice(g * HEAD_D, (g + 1) * HEAD_D)
            mc_s[gs, gs] = mcs_ref[g][:, :HEAD_D].astype(BF16)
            ms_s[gs, gs] = mcs_ref[g][:, HEAD_D:].astype(BF16)

    @pl.when(m == 0)
    def _():
        ri = lax.broadcasted_iota(jnp.int32, (blk, 2 * blk), 0)
        ci = lax.broadcasted_iota(jnp.int32, (blk, 2 * blk), 1)
        rev = jnp.where(((ci < blk) & (ri + ci == blk)) | ((ri == 0) & (ci == blk)), 1.0, 0.0).astype(BF16)
        zero_blk = jnp.zeros((blk, GROUP_W), BF16)

        def reversed_block(ref, base, k, last):
            top = ref[base + (n_blk - 1 - k) * blk:base + (n_blk - k) * blk, :]
            bottom = last if k == 0 else ref[base + (n_blk - k) * blk:base + (n_blk - k + 1) * blk, :]
            return _dot(rev, jnp.concatenate([top, bottom], axis=0))

        for k in range(n_blk):
            low = u_ref[k * blk:(k + 1) * blk, :].astype(F32)
            mirrored = reversed_block(u_ref, h, k, zero_blk)
            ve_s[k * blk:(k + 1) * blk, :] = _dot((low + mirrored).astype(BF16), mc_s[...]).astype(BF16)
            vo_s[k * blk:(k + 1) * blk, :] = _dot((low - mirrored).astype(BF16), ms_s[...]).astype(BF16)
        sign_row = jnp.where(lax.broadcasted_iota(jnp.int32, (16, l), 1) % 2 == 0, 1.0, -1.0)
        sign_row = jnp.where(lax.broadcasted_iota(jnp.int32, (16, l), 0) == 0, sign_row, 0.0).astype(BF16)
        alt_u = _dot(sign_row, u_ref[...])
        y_nyq = _dot(alt_u.astype(BF16), mc_s[...])[0:1, :]
        v_nyq = _dot(u_ref[h:h + 16, :], mc_s[...])[0:1, :]
        sign_col = jnp.where(lax.broadcasted_iota(jnp.int32, (blk, GROUP_W), 0) % 2 == 0, 1.0, -1.0)
        first_row = lax.broadcasted_iota(jnp.int32, (blk, GROUP_W), 0) == 0
        d_s[h:h + blk, :] = jnp.where(first_row, y_nyq, 0.0).astype(BF16)
        for k in range(n_blk):
            rows = slice(k * blk, (k + 1) * blk)
            a = _dot(ch_s[rows, :], ve_s[...]) + sign_col * v_nyq
            b = _dot(sh_s[rows, :], vo_s[...])
            yf_s[rows, :] = (a + b).astype(BF16)
            d_s[rows, :] = (a - b).astype(BF16)
        for k in range(n_blk):
            yf_s[h + k * blk:h + (k + 1) * blk, :] = reversed_block(
                d_s, 0, k, d_s[h:h + blk, :]).astype(BF16)

    tm = x_ref.shape[0]
    half = wout_s.shape[0] // 2
    subs = [slice(sb * ROW_BLOCK, (sb + 1) * ROW_BLOCK) for sb in range(tm // ROW_BLOCK)]
    ys = []
    for sb, rs in enumerate(subs):
        r0 = pl.multiple_of(m * tm + sb * ROW_BLOCK, ROW_BLOCK)
        yd = (on_ref[rs, :].astype(F32) * z_ref[rs, :].astype(F32)).astype(BF16)
        ys.append(_dot(yf_s[pl.ds(r0, ROW_BLOCK), :], wout_s[:half, :]) + _dot(yd, wout_s[half:, :]))
    for rs, y in zip(subs, ys):
        x1 = x_ref[rs, :] + gt_ref[...] * (_rms(y) * gpost_ref[...])
        x1_ref[rs, :] = x1
        h2 = _rms(x1) * gpre_ref[...]
        h2_ref[rs, :] = (h2 * (1.0 + scale_ref[...]) + shift_ref[...]).astype(BF16)


def _mix(pg, on, ch, sh, mcs, w_out, x, g_post, gt, g_pre, shift, scale):
    b, l, d = x.shape
    tm = MIX_TILE
    h = l // 2
    vec = pl.BlockSpec((1, d), lambda i, m: (0, 0))
    mod = pl.BlockSpec((None, 1, d), lambda i, m: (i, 0, 0))
    whole = lambda a: pl.BlockSpec(a.shape, lambda i, m: (0,) * a.ndim)
    return pl.pallas_call(
        _mix_kernel,
        grid=(b, l // tm),
        in_specs=[pl.BlockSpec((None, l, GROUP_W), lambda i, m: (i, 0, 0)),
                  pl.BlockSpec((None, tm, GROUP_W), lambda i, m: (i, m, 0)),
                  pl.BlockSpec((None, tm, GROUP_W), lambda i, m: (i, m, N_GROUPS - 1)),
                  whole(ch), whole(sh), whole(mcs), whole(w_out),
                  pl.BlockSpec((None, tm, d), lambda i, m: (i, m, 0)),
                  vec, mod, vec, mod, mod],
        out_specs=[pl.BlockSpec((None, tm, d), lambda i, m: (i, m, 0)),
                   pl.BlockSpec((None, tm, d), lambda i, m: (i, m, 0))],
        out_shape=[jax.ShapeDtypeStruct((b, l, d), F32),
                   jax.ShapeDtypeStruct((b, l, d), BF16)],
        scratch_shapes=[pltpu.VMEM((l, GROUP_W), BF16),
                        pltpu.VMEM((h, h), BF16),
                        pltpu.VMEM((h, h), BF16),
                        pltpu.VMEM((GROUP_W, GROUP_W), BF16),
                        pltpu.VMEM((GROUP_W, GROUP_W), BF16),
                        pltpu.VMEM(w_out.shape, BF16),
                        pltpu.VMEM((h, GROUP_W), BF16),
                        pltpu.VMEM((h, GROUP_W), BF16),
                        pltpu.VMEM((h + PAIR, GROUP_W), BF16)],
        compiler_params=_params(48, ("arbitrary", "arbitrary")),
        name="mix",
    )(pg, on, pg, ch, sh, mcs, w_out, x, g_post, gt, g_pre, shift, scale)


def _ffn_kernel(h_ref, x1_ref, wv_ref, wg_ref, dwc_ref, wd_ref, gpost_ref, gt_ref,
                o_ref, gate_s, val_s, wvg_s, wd_s):
    j = pl.program_id(1)
    l = h_ref.shape[0]
    tf = wv_ref.shape[1]
    n_rb = l // ROW_BLOCK
    pad = GRID_W

    wvg_s[:, :tf] = wv_ref[...].astype(BF16)
    wvg_s[:, tf:] = wg_ref[...].astype(BF16)
    wd_s[...] = wd_ref[...].astype(BF16)
    gate_s[0:pad, :] = jnp.zeros((pad, tf), F32)
    gate_s[l + pad:l + 2 * pad, :] = jnp.zeros((pad, tf), F32)

    @pl.when(j == 0)
    def _():
        o_ref[...] = jnp.zeros(o_ref.shape, F32)

    n_ext = ROW_BLOCK + 2 * pad
    col = lax.broadcasted_iota(jnp.int32, (n_ext, tf), 0) % GRID_W
    not_first = jnp.where(col == 0, 0.0, 1.0)
    not_last = jnp.where(col == GRID_W - 1, 0.0, 1.0)
    dw = dwc_ref[...]

    def up(i):
        r0 = i * ROW_BLOCK
        vg = _dot(h_ref[r0:r0 + ROW_BLOCK, :], wvg_s[...])
        val_s[r0:r0 + ROW_BLOCK, :] = vg[:, :tf]
        gate_s[r0 + pad:r0 + pad + ROW_BLOCK, :] = vg[:, tf:]

    def down(i):
        r0 = i * ROW_BLOCK
        ext = gate_s[r0:r0 + n_ext, :]
        shifted = (pltpu.roll(ext, 1, 0) * not_first, ext, pltpu.roll(ext, n_ext - 1, 0) * not_last)
        conv = None
        for dy in range(3):
            for dx in range(3):
                term = shifted[dx][dy * pad:dy * pad + ROW_BLOCK] * dw[3 * dy + dx:3 * dy + dx + 1]
                conv = term if conv is None else conv + term
        act = (_silu(conv) * val_s[r0:r0 + ROW_BLOCK, :]).astype(BF16)
        o_ref[r0:r0 + ROW_BLOCK, :] += _dot(act, wd_s[...])

    up(0)
    up(1)
    for i in range(n_rb):
        if i + 2 < n_rb:
            up(i + 2)
        down(i)

    @pl.when(j == pl.num_programs(1) - 1)
    def _():
        def fin(rb, carry):
            r0 = pl.multiple_of(rb * ROW_BLOCK, ROW_BLOCK)
            y = _rms(o_ref[pl.ds(r0, ROW_BLOCK), :]) * gpost_ref[...]
            o_ref[pl.ds(r0, ROW_BLOCK), :] = x1_ref[pl.ds(r0, ROW_BLOCK), :] + gt_ref[...] * y
            return carry
        lax.fori_loop(0, n_rb, fin, 0)


def _ffn(h2, x1, w_up, dwc, w_down, g_post, gt):
    b, l, d = x1.shape
    d_ff = w_down.shape[0]
    tf = FFN_TILE
    n_f = d_ff // tf
    return pl.pallas_call(
        _ffn_kernel,
        grid=(b, n_f),
        in_specs=[pl.BlockSpec((None, l, d), lambda i, j: (i, 0, 0)),
                  pl.BlockSpec((None, l, d), lambda i, j: (i, 0, 0)),
                  pl.BlockSpec((d, tf), lambda i, j: (0, j)),
                  pl.BlockSpec((d, tf), lambda i, j: (0, n_f + j)),
                  pl.BlockSpec((9, tf), lambda i, j: (0, j)),
                  pl.BlockSpec((tf, d), lambda i, j: (j, 0)),
                  pl.BlockSpec((1, d), lambda i, j: (0, 0)),
                  pl.BlockSpec((None, 1, d), lambda i, j: (i, 0, 0))],
        out_specs=pl.BlockSpec((None, l, d), lambda i, j: (i, 0, 0)),
        out_shape=jax.ShapeDtypeStruct((b, l, d), F32),
        scratch_shapes=[pltpu.VMEM((l + 2 * GRID_W, tf), F32),
                        pltpu.VMEM((l, tf), F32),
                        pltpu.VMEM((d, 2 * tf), BF16),
                        pltpu.VMEM((tf, d), BF16)],
        compiler_params=_params(58, ("arbitrary", "arbitrary")),
        name="ffn",
    )(h2, x1, w_up, w_up, dwc, w_down, g_post, gt)


def kernel(x, c, ctx, c_ctx, w_ada, b_ada, g_pre_mix, g_post_mix, g_pre_ffn, g_post_ffn,
           w_in, w_qkv_conv, a_log, dt_bias, g_gdn, w_fourier, w_out, w_up, w_dwc, w_down):
    b, l, d = x.shape
    assert w_ada.shape[0] == 1, "single-layer stack"
    assert b < 16 and l % ROW_BLOCK == 0 and ctx.shape[1] % ROW_BLOCK == 0
    assert w_in.shape[2] == N_GROUPS * GROUP_W + N_GATES and d == 2 * GROUP_W

    cc = jnp.zeros((16, d), F32).at[:b].set(c).at[b].set(c_ctx)
    mod = _ada(cc, w_ada[0], b_ada)
    sh1, sc1, gt1, sh2, sc2, gt2 = [m[:, None, :] for m in jnp.split(mod[:b], 6, axis=-1)]
    sh1c, sc1c = [m[None, :] for m in jnp.split(mod[b], 6, axis=-1)[:2]]

    wab = jnp.pad(w_in[0][:, N_GROUPS * GROUP_W:], ((0, 0), (0, PAIR - N_GATES)))
    zeros8 = jnp.zeros((8,), F32)
    alog = jnp.concatenate([zeros8, a_log[0].reshape(-1)])[:, None]
    dtb = jnp.concatenate([zeros8, dt_bias[0].reshape(-1)])[:, None]

    pg, kt, gat = _inproj(x, ctx, g_pre_mix, sh1, sc1, sh1c, sc1c, w_in[0], wab, w_qkv_conv[0], alog, dtb)
    on = _gdn(pg, kt, gat, g_gdn, l)

    ch, sh, c_c, s_c = _dft_constants(l, HEAD_D)
    mcs = _fourier_w(c_c, s_c, w_fourier[0])
    x1, h2 = _mix(pg, on, ch, sh, mcs, w_out[0], x, g_post_mix, gt1, g_pre_ffn, sh2, sc2)
    d_ff = w_down.shape[1]
    return _ffn(h2, x1, w_up[0], w_dwc[0].reshape(9, d_ff), w_down[0], g_post_ffn, gt2)
```

```python
import functools
import itertools
import math

import numpy as np
import jax
import jax.numpy as jnp
from jax import lax
from jax.experimental import pallas as pl
from jax.experimental.pallas import tpu as pltpu

F32 = jnp.float32
BF16 = jnp.bfloat16

GRID_W = 64
HEADS = 4
HEAD_D = 128
CHUNK = 64
PAIR = 2 * CHUNK
GROUP_W = HEADS * HEAD_D
N_GROUPS = 5
N_GATES = 16
EPS = 1e-6
ROW_BLOCK = 256
FFN_TILE = 256
MIX_TILE = 512
MIB = 1024 * 1024

_NT = (((1,), (1,)), ((), ()))


def _dot(a, b):
    return jnp.dot(a, b, preferred_element_type=F32)


def _silu(x):
    return x / (1.0 + jnp.exp(-x))


def _rms(x):
    return x * lax.rsqrt(jnp.mean(x * x, axis=-1, keepdims=True) + EPS)


def _params(vmem_mib, semantics):
    return pltpu.CompilerParams(dimension_semantics=semantics,
                                vmem_limit_bytes=vmem_mib * MIB)


def _ada_kernel(c_ref, w_ref, b_ref, o_ref):
    s = _silu(c_ref[...]).astype(BF16)
    o_ref[...] = _dot(s, w_ref[...].astype(BF16)) + b_ref[...]


def _ada(cc, w, b):
    rows, d = cc.shape
    n = w.shape[1]
    tn = 1024
    return pl.pallas_call(
        _ada_kernel,
        grid=(n // tn,),
        in_specs=[pl.BlockSpec((rows, d), lambda j: (0, 0)),
                  pl.BlockSpec((d, tn), lambda j: (0, j)),
                  pl.BlockSpec((1, tn), lambda j: (0, j))],
        out_specs=pl.BlockSpec((rows, tn), lambda j: (0, j)),
        out_shape=jax.ShapeDtypeStruct((rows, n), F32),
        compiler_params=_params(32, ("arbitrary",)),
        name="ada",
    )(cc, w, b)


def _inproj_kernel(x_ref, ctx_ref, g_ref, sh_ref, sc_ref, shc_ref, scc_ref, w_ref, wab_ref,
                   conv_ref, alog_ref, dtb_ref, pg_ref, kt_ref, gat_ref, h_s, p_s):
    j = pl.program_id(1)
    l, lc = x_ref.shape[0], ctx_ref.shape[0]
    slabs = ROW_BLOCK // PAIR
    n_ext = ROW_BLOCK + 16
    blocks = [(x_ref, r, r, r + 8, sh_ref, sc_ref) for r in range(0, l, ROW_BLOCK)]
    blocks += [(ctx_ref, r, l + r, l + r + 16, shc_ref, scc_ref) for r in range(0, lc, ROW_BLOCK)]
    n_rb = len(blocks)

    def out_rows(rb):
        return slice(blocks[rb][2], blocks[rb][2] + ROW_BLOCK)

    def project(rb):
        return _dot(h_s[out_rows(rb), :], w_ref[...])

    def pipelined(first, second, lead):
        for rb in range(min(lead, n_rb)):
            first(rb)
        for rb in range(n_rb):
            if rb + lead < n_rb:
                first(rb + lead)
            second(rb)

    @pl.when(j == 0)
    def _u():
        for r in (0, l + 8, l + lc + 16):
            p_s[r:r + 8, :] = jnp.zeros((8, GROUP_W), F32)
        wabt = wab_ref[...].T[:N_GATES].astype(BF16)
        row = lax.broadcasted_iota(jnp.int32, (N_GATES, PAIR), 0)

        def prologue(rb):
            src, r, dst, _, shift, scale = blocks[rb]
            h = _rms(src[r:r + ROW_BLOCK, :]) * g_ref[...]
            hb = (h * (1.0 + scale[...]) + shift[...]).astype(BF16)
            h_s[dst:dst + ROW_BLOCK, :] = hb
            for s in range(slabs):
                ab = lax.dot_general(wabt, hb[s * PAIR:(s + 1) * PAIR], _NT,
                                     preferred_element_type=F32)
                beta = 1.0 / (1.0 + jnp.exp(-ab))
                xs = ab + dtb_ref[...]
                softplus = jnp.maximum(xs, 0.0) + jnp.log1p(jnp.exp(-jnp.abs(xs)))
                gate = -jnp.exp(alog_ref[...]) * softplus
                gat_ref[dst // PAIR + s] = jnp.where(row < 8, beta, gate)

        def store_u(rb):
            pg_ref[out_rows(rb), :] = project(rb).astype(BF16)

        pipelined(prologue, store_u, 2)

    @pl.when(j == N_GROUPS - 1)
    def _z():
        for rb in range(n_rb):
            pg_ref[out_rows(rb), :] = _silu(project(rb)).astype(BF16)

    def conv_group(normalise, transpose):
        cw = conv_ref[...]

        def raw(rb):
            p0 = blocks[rb][3]
            p_s[p0:p0 + ROW_BLOCK, :] = project(rb)

        def finish(rb):
            p0 = blocks[rb][3]
            ext = p_s[p0 - 8:p0 - 8 + n_ext, :]
            prev = pltpu.roll(ext, 1, 0)[8:8 + ROW_BLOCK]
            nxt = pltpu.roll(ext, n_ext - 1, 0)[8:8 + ROW_BLOCK]
            cur = ext[8:8 + ROW_BLOCK]
            a = _silu(prev * cw[0:1] + cur * cw[1:2] + nxt * cw[2:3])
            if normalise:
                segs = []
                for hh in range(HEADS):
                    seg = a[:, hh * HEAD_D:(hh + 1) * HEAD_D]
                    segs.append(seg * lax.rsqrt(jnp.sum(seg * seg, axis=-1, keepdims=True) + EPS))
                a = jnp.concatenate(segs, axis=1)
            pg_ref[out_rows(rb), :] = a.astype(BF16)
            if transpose:
                at = a.T.astype(BF16)
                for s in range(slabs):
                    kt_ref[blocks[rb][2] // PAIR + s] = at[:, s * PAIR:(s + 1) * PAIR]

        pipelined(raw, finish, 2)

    @pl.when(j == 1)
    def _q():
        conv_group(True, False)

    @pl.when(j == 2)
    def _k():
        conv_group(True, True)

    @pl.when(j == 3)
    def _v():
        conv_group(False, False)


def _inproj(x, ctx, g, shift, scale, shift_c, scale_c, w_in, wab, conv, alog, dtb):
    b, l, d = x.shape
    lt = l + ctx.shape[1]
    n_pairs = lt // PAIR
    vec = pl.BlockSpec((1, d), lambda i, j: (0, 0))
    mod = pl.BlockSpec((None, 1, d), lambda i, j: (i, 0, 0))
    return pl.pallas_call(
        _inproj_kernel,
        grid=(b, N_GROUPS),
        in_specs=[pl.BlockSpec((None, l, d), lambda i, j: (i, 0, 0)),
                  pl.BlockSpec((None, lt - l, d), lambda i, j: (i, 0, 0)),
                  vec, mod, mod, vec, vec,
                  pl.BlockSpec((d, GROUP_W), lambda i, j: (0, j)),
                  pl.BlockSpec((d, PAIR), lambda i, j: (0, 0)),
                  pl.BlockSpec((3, GROUP_W), lambda i, j: (0, jnp.clip(j - 1, 0, 2))),
                  pl.BlockSpec((N_GATES, 1), lambda i, j: (0, 0)),
                  pl.BlockSpec((N_GATES, 1), lambda i, j: (0, 0))],
        out_specs=[pl.BlockSpec((None, lt, GROUP_W), lambda i, j: (i, 0, j)),
                   pl.BlockSpec((None, n_pairs, GROUP_W, PAIR), lambda i, j: (i, 0, 0, 0)),
                   pl.BlockSpec((None, n_pairs, N_GATES, PAIR), lambda i, j: (i, 0, 0, 0))],
        out_shape=[jax.ShapeDtypeStruct((b, lt, N_GROUPS * GROUP_W), BF16),
                   jax.ShapeDtypeStruct((b, n_pairs, GROUP_W, PAIR), BF16),
                   jax.ShapeDtypeStruct((b, n_pairs, N_GATES, PAIR), F32)],
        scratch_shapes=[pltpu.VMEM((lt, d), BF16),
                        pltpu.VMEM((lt + 24, GROUP_W), F32)],
        compiler_params=_params(52, ("arbitrary", "arbitrary")),
        name="inproj",
    )(x, ctx, g, shift, scale, shift_c, scale_c, w_in, wab, conv, alog, dtb)


def _bd2(a, b):
    z = jnp.zeros_like(a)
    return jnp.concatenate([jnp.concatenate([a, z], axis=1),
                            jnp.concatenate([z, b], axis=1)], axis=0)


def _bd_of(x):
    xb = x.astype(BF16)
    return _bd2(xb[:, :HEAD_D], xb[:, HEAD_D:])


def _round_robin(*stage_generators):
    for _ in itertools.zip_longest(*stage_generators):
        pass


def _gdn_kernel(q_ref, k_ref, v_ref, kt_ref, gat_ref, g_ref, y_ref,
                u_s, wq_s, at_s, kdt_s, egl_s, s_s, o_s):
    lt, l = q_ref.shape[0], y_ref.shape[0]
    n_steps = lt // PAIR
    n_ctx = (lt - l) // PAIR
    per_iter = 2
    assert n_steps % per_iter == 0

    ri = lax.broadcasted_iota(jnp.int32, (PAIR, PAIR), 0)
    ci = lax.broadcasted_iota(jnp.int32, (PAIR, PAIR), 1)
    same = (ri // CHUNK) == (ci // CHUNK)
    incl = (jnp.where(same & (ri >= ci), 1.0, 0.0), jnp.where(same & (ri <= ci), 1.0, 0.0))
    strict = (jnp.where(same & (ri > ci), 1.0, 0.0), jnp.where(same & (ri < ci), 1.0, 0.0))
    tri = (incl[1], incl[0])
    eye = jnp.where(ri == ci, 1.0, 0.0)
    eye2 = jnp.concatenate([eye, eye], axis=1)
    off = ([], [])
    for lvl in range(6):
        b = 2 ** lvl
        joined = (ri // (2 * b)) == (ci // (2 * b))
        off[0].append(jnp.where(joined & (ri % (2 * b) >= b) & (ci % (2 * b) < b), 1.0, 0.0))
        off[1].append(jnp.where(joined & (ri % (2 * b) < b) & (ci % (2 * b) >= b), 1.0, 0.0))

    def joining_blocks(a, d, lvl):
        m = off[d][lvl]
        return jnp.concatenate([a[:, :HEAD_D] * m, a[:, HEAD_D:] * m], axis=1)

    lane = lax.broadcasted_iota(jnp.int32, (1, PAIR), 1)
    first_half = lane < CHUNK
    scale = HEAD_D ** -0.5
    hi = lax.Precision.HIGHEST
    lanes = [slice(hp * 2 * HEAD_D, (hp + 1) * 2 * HEAD_D) for hp in range(HEADS // 2)]
    stream_ids = [(hp, d) for hp in range(HEADS // 2) for d in range(2)]
    zeros_half = jnp.zeros((CHUNK, 2 * HEAD_D), BF16)

    def pair_of(step, d):
        if d == 0:
            return lax.rem(step + (n_steps - n_ctx), n_steps)
        return n_steps - 1 - step

    def prepare(slot, step0):
        streams = []
        for kk in range(per_iter):
            for d in range(2):
                p = pair_of(step0 + kk, d)
                t0 = pl.multiple_of(p * PAIR, PAIR)
                gates = gat_ref[p]
                kt_all = kt_ref[p]
                gc = jnp.dot(gates, tri[d], precision=hi, preferred_element_type=F32)
                for hp in range(HEADS // 2):
                    k2 = k_ref[pl.ds(t0, PAIR), lanes[hp]]
                    q2 = q_ref[pl.ds(t0, PAIR), lanes[hp]]
                    v2 = v_ref[pl.ds(t0, PAIR), lanes[hp]]
                    kt2 = kt_all[hp * 2 * HEAD_D:(hp + 1) * 2 * HEAD_D, :]
                    bd_kt = _bd2(kt2[:HEAD_D], kt2[HEAD_D:])
                    streams.append(dict(kk=kk, d=d, hp=hp, gates=gates, gc=gc, k2=k2, q2=q2, v2=v2,
                                        kt2=kt2, kk_=_dot(k2, bd_kt), qk_=_dot(q2, bd_kt)))
        yield
        for st in streams:
            kk, d, hp, gates, gc = st["kk"], st["d"], st["hp"], st["gates"], st["gc"]
            gc_t = gc.T
            a, attn, qd, kdt, erow = [], [], [], [], []
            for hh in range(2):
                h = 2 * hp + hh
                hs = slice(hh * HEAD_D, (hh + 1) * HEAD_D)
                brow = gates[4 * d + h:4 * d + h + 1, :]
                gcr = gc[8 + 4 * d + h:9 + 4 * d + h, :]
                gcc = gc_t[:, 8 + 4 * d + h:9 + 4 * d + h]
                decay = jnp.exp((gcc - gcr) * incl[d]) * incl[d]
                a.append(st["kk_"][:, hs] * decay * strict[d] * brow)
                attn.append(st["qk_"][:, hs] * decay * (brow * scale))
                qd.append(st["q2"][:, hs].astype(F32) * (jnp.exp(gcc) * scale))
                if d == 0:
                    gl0, gl1 = gcr[:, CHUNK - 1:CHUNK], gcr[:, PAIR - 1:PAIR]
                else:
                    gl0, gl1 = gcr[:, 0:1], gcr[:, CHUNK:CHUNK + 1]
                glr = jnp.where(first_half, gl0, gl1)
                kdt.append(st["kt2"][hs, :].astype(F32) * (jnp.exp(glr - gcr) * brow))
                erow.append(jnp.exp(gcr))
                egl_s[slot, d, kk, 0, h:h + 1, :] = jnp.broadcast_to(jnp.exp(gl0), (1, PAIR))
                egl_s[slot, d, kk, 1, h:h + 1, :] = jnp.broadcast_to(jnp.exp(gl1), (1, PAIR))
            st["a"] = jnp.concatenate(a, axis=1)
            st["erow"] = jnp.concatenate(erow, axis=1)
            at_s[slot, d, kk, :, lanes[hp]] = jnp.concatenate(attn, axis=1).astype(BF16)
            kdt_s[slot, d, kk, hp] = jnp.concatenate(kdt, axis=1).astype(BF16)
            st["qd"] = jnp.concatenate(qd, axis=1).astype(BF16)
            st["t"] = eye2 - joining_blocks(st["a"], d, 0)
        yield
        for lvl in range(1, 6):
            for st in streams:
                st["y"] = _dot(joining_blocks(st["a"], st["d"], lvl).astype(BF16), _bd_of(st["t"]))
            yield
            for st in streams:
                st["t"] = st["t"] - _dot(st["t"].astype(BF16), _bd_of(st["y"]))
            yield
        for st in streams:
            t = st["t"]
            st["u"] = _dot(t.astype(BF16), _bd2(st["v2"][:, :HEAD_D], st["v2"][:, HEAD_D:]))
            st["w"] = _dot((t * st["erow"]).astype(BF16), _bd2(st["k2"][:, :HEAD_D], st["k2"][:, HEAD_D:]))
        yield
        for st in streams:
            kk, d, hp = st["kk"], st["d"], st["hp"]
            u_s[slot, d, kk, :, lanes[hp]] = st["u"].astype(BF16)
            wb, qdb = st["w"].astype(BF16), st["qd"]
            wq_s[slot, d, kk, :, lanes[hp]] = jnp.concatenate(
                [wb[:CHUNK], qdb[:CHUNK], wb[CHUNK:], qdb[CHUNK:]], axis=0)
        yield

    def scan(slot, step0):
        state = {(hp, d): s_s[d, :, lanes[hp]] for (hp, d) in stream_ids}
        for kk in range(per_iter):
            for half in range(2):
                e = {0: half, 1: 1 - half}
                res, bd_x = {}, {}
                for (hp, d) in stream_ids:
                    res[(hp, d)] = _dot(wq_s[slot, d, kk, e[d] * PAIR:(e[d] + 1) * PAIR, lanes[hp]],
                                        _bd_of(state[(hp, d)]))
                yield
                for (hp, d) in stream_ids:
                    vn = (u_s[slot, d, kk, e[d] * CHUNK:(e[d] + 1) * CHUNK, lanes[hp]].astype(F32)
                          - res[(hp, d)][:CHUNK])
                    vnb = vn.astype(BF16)
                    x = jnp.concatenate([vnb, zeros_half] if e[d] == 0 else [zeros_half, vnb], axis=0)
                    bd_x[(hp, d)] = _bd2(x[:, :HEAD_D], x[:, HEAD_D:])
                for (hp, d) in stream_ids:
                    both = _dot(jnp.concatenate(
                        [kdt_s[slot, d, kk, hp],
                         at_s[slot, d, kk, e[d] * CHUNK:(e[d] + 1) * CHUNK, lanes[hp]]], axis=0),
                        bd_x[(hp, d)])
                    eg = jnp.concatenate([egl_s[slot, d, kk, e[d], 2 * hp:2 * hp + 1, :],
                                          egl_s[slot, d, kk, e[d], 2 * hp + 1:2 * hp + 2, :]], axis=1)
                    state[(hp, d)] = state[(hp, d)] * eg + both[:HEAD_D]
                    r0 = pl.multiple_of(pair_of(step0 + kk, d) * PAIR + e[d] * CHUNK, CHUNK)
                    o_s[pl.ds(r0, CHUNK), lanes[hp]] += res[(hp, d)][CHUNK:] + both[HEAD_D:]
                yield
        for (hp, d) in stream_ids:
            s_s[d, :, lanes[hp]] = state[(hp, d)]

    s_s[...] = jnp.zeros(s_s.shape, F32)
    o_s[...] = jnp.zeros(o_s.shape, F32)
    n_iter = n_steps // per_iter
    _round_robin(prepare(0, 0))

    def body(it, carry):
        _round_robin(prepare(lax.rem(it + 1, 2), (it + 1) * per_iter), scan(lax.rem(it, 2), it * per_iter))
        return carry

    lax.fori_loop(0, n_iter - 1, body, 0)
    _round_robin(scan((n_iter - 1) % 2, (n_iter - 1) * per_iter))

    def epilogue(rb, carry):
        r0 = pl.multiple_of(rb * ROW_BLOCK, ROW_BLOCK)
        o = o_s[pl.ds(r0, ROW_BLOCK), :]
        segs = [_rms(o[:, h * HEAD_D:(h + 1) * HEAD_D]) * g_ref[...] for h in range(HEADS)]
        y_ref[pl.ds(r0, ROW_BLOCK), :] = jnp.concatenate(segs, axis=1).astype(BF16)
        return carry

    lax.fori_loop(0, l // ROW_BLOCK, epilogue, 0)


def _gdn(pg, kt, gat, g_gdn, l):
    b, lt, _ = pg.shape
    n_pairs = lt // PAIR

    def group(g):
        return pl.BlockSpec((None, lt, GROUP_W), lambda i, g=g: (i, 0, g))

    ring = (2, 2, 2)
    return pl.pallas_call(
        _gdn_kernel,
        grid=(b,),
        in_specs=[group(1), group(2), group(3),
                  pl.BlockSpec((None, n_pairs, GROUP_W, PAIR), lambda i: (i, 0, 0, 0)),
                  pl.BlockSpec((None, n_pairs, N_GATES, PAIR), lambda i: (i, 0, 0, 0)),
                  pl.BlockSpec((1, HEAD_D), lambda i: (0, 0))],
        out_specs=pl.BlockSpec((None, l, GROUP_W), lambda i: (i, 0, 0)),
        out_shape=jax.ShapeDtypeStruct((b, l, GROUP_W), BF16),
        scratch_shapes=[pltpu.VMEM(ring + (PAIR, GROUP_W), BF16),
                        pltpu.VMEM(ring + (2 * PAIR, GROUP_W), BF16),
                        pltpu.VMEM(ring + (PAIR, GROUP_W), BF16),
                        pltpu.VMEM(ring + (HEADS // 2, HEAD_D, 2 * PAIR), BF16),
                        pltpu.VMEM(ring + (2, 8, PAIR), F32),
                        pltpu.VMEM((2, HEAD_D, GROUP_W), F32),
                        pltpu.VMEM((lt, GROUP_W), F32)],
        compiler_params=_params(48, ("arbitrary",)),
        name="gdn",
    )(pg, pg, pg, kt, gat, g_gdn)


def _fourier_w_kernel(cc_ref, sc_ref, w_ref, o_ref):
    hi = lax.Precision.HIGHEST
    for g in range(w_ref.shape[0]):
        w = w_ref[g]
        o_ref[g] = jnp.concatenate(
            [jnp.dot(cc_ref[...], w, precision=hi, preferred_element_type=F32),
             -jnp.dot(sc_ref[...], w, precision=hi, preferred_element_type=F32)], axis=1)


def _fourier_w(cc, sc, w_f):
    groups, c, d = w_f.shape
    return pl.pallas_call(
        _fourier_w_kernel,
        out_shape=jax.ShapeDtypeStruct((groups, c, 2 * d), F32),
        name="fourier_w",
    )(cc, sc, w_f)


@functools.lru_cache(maxsize=None)
def _dft_constants(length, channels):
    def cos_sin(n, keep):
        idx = np.arange(keep, dtype=np.int64)
        ang = (np.outer(idx, idx) % n).astype(np.float64) * (2.0 * np.pi / n)
        return np.cos(ang), np.sin(ang)
    c_l, s_l = cos_sin(length, length // 2)
    c_c, s_c = cos_sin(channels, channels)
    norm = 1.0 / math.sqrt(length * channels)
    return (jnp.asarray(c_l.astype(np.float32)), jnp.asarray(s_l.astype(np.float32)),
            jnp.asarray((c_c * norm).astype(np.float32)),
            jnp.asarray((s_c * norm).astype(np.float32)))


def _mix_kernel(u_ref, on_ref, z_ref, ch_ref, sh_ref, mcs_ref, wout_ref, x_ref, gpost_ref, gt_ref,
                gpre_ref, shift_ref, scale_ref, x1_ref, h2_ref,
                yf_s, ch_s, sh_s, mc_s, ms_s, wout_s, ve_s, vo_s, d_s):
    i, m = pl.program_id(0), pl.program_id(1)
    l = u_ref.shape[0]
    h = l // 2
    blk = PAIR
    n_blk = h // blk

    @pl.when((i == 0) & (m == 0))
    def _():
        ch_s[...] = ch_ref[...].astype(BF16)
        sh_s[...] = sh_ref[...].astype(BF16)
        wout_s[...] = wout_ref[...].astype(BF16)
        mc_s[...] = jnp.zeros(mc_s.shape, BF16)
        ms_s[...] = jnp.zeros(ms_s.shape, BF16)
        for g in range(HEADS):
            gs = slice(g * HEAD_D, (g + 1) * HEAD_D)
            mc_s[gs, gs] = mcs_ref[g][:, :HEAD_D].astype(BF16)
            ms_s[gs, gs] = mcs_ref[g][:, HEAD_D:].astype(BF16)

    @pl.when(m == 0)
    def _():
        ri = lax.broadcasted_iota(jnp.int32, (blk, 2 * blk), 0)
        ci = lax.broadcasted_iota(jnp.int32, (blk, 2 * blk), 1)
        rev = jnp.where(((ci < blk) & (ri + ci == blk)) | ((ri == 0) & (ci == blk)), 1.0, 0.0).astype(BF16)
        zero_blk = jnp.zeros((blk, GROUP_W), BF16)

        def reversed_block(ref, base, k, last):
            top = ref[base + (n_blk - 1 - k) * blk:base + (n_blk - k) * blk, :]
            bottom = last if k == 0 else ref[base + (n_blk - k) * blk:base + (n_blk - k + 1) * blk, :]
            return _dot(rev, jnp.concatenate([top, bottom], axis=0))

        for k in range(n_blk):
            low = u_ref[k * blk:(k + 1) * blk, :].astype(F32)
            mirrored = reversed_block(u_ref, h, k, zero_blk)
            ve_s[k * blk:(k + 1) * blk, :] = _dot((low + mirrored).astype(BF16), mc_s[...]).astype(BF16)
            vo_s[k * blk:(k + 1) * blk, :] = _dot((low - mirrored).astype(BF16), ms_s[...]).astype(BF16)
        sign_row = jnp.where(lax.broadcasted_iota(jnp.int32, (16, l), 1) % 2 == 0, 1.0, -1.0)
        sign_row = jnp.where(lax.broadcasted_iota(jnp.int32, (16, l), 0) == 0, sign_row, 0.0).astype(BF16)
        alt_u = _dot(sign_row, u_ref[...])
        y_nyq = _dot(alt_u.astype(BF16), mc_s[...])[0:1, :]
        v_nyq = _dot(u_ref[h:h + 16, :], mc_s[...])[0:1, :]
        sign_col = jnp.where(lax.broadcasted_iota(jnp.int32, (blk, GROUP_W), 0) % 2 == 0, 1.0, -1.0)
        first_row = lax.broadcasted_iota(jnp.int32, (blk, GROUP_W), 0) == 0
        d_s[h:h + blk, :] = jnp.where(first_row, y_nyq, 0.0).astype(BF16)
        for k in range(n_blk):
            rows = slice(k * blk, (k + 1) * blk)
            a = _dot(ch_s[rows, :], ve_s[...]) + sign_col * v_nyq
            b = _dot(sh_s[rows, :], vo_s[...])
            yf_s[rows, :] = (a + b).astype(BF16)
            d_s[rows, :] = (a - b).astype(BF16)
        for k in range(n_blk):
            yf_s[h + k * blk:h + (k + 1) * blk, :] = reversed_block(
                d_s, 0, k, d_s[h:h + blk, :]).astype(BF16)

    tm = x_ref.shape[0]
    half = wout_s.shape[0] // 2
    subs = [slice(sb * ROW_BLOCK, (sb + 1) * ROW_BLOCK) for sb in range(tm // ROW_BLOCK)]
    ys = []
    for sb, rs in enumerate(subs):
        r0 = pl.multiple_of(m * tm + sb * ROW_BLOCK, ROW_BLOCK)
        yd = (on_ref[rs, :].astype(F32) * z_ref[rs, :].astype(F32)).astype(BF16)
        ys.append(_dot(yf_s[pl.ds(r0, ROW_BLOCK), :], wout_s[:half, :]) + _dot(yd, wout_s[half:, :]))
    for rs, y in zip(subs, ys):
        x1 = x_ref[rs, :] + gt_ref[...] * (_rms(y) * gpost_ref[...])
        x1_ref[rs, :] = x1
        h2 = _rms(x1) * gpre_ref[...]
        h2_ref[rs, :] = (h2 * (1.0 + scale_ref[...]) + shift_ref[...]).astype(BF16)


def _mix(pg, on, ch, sh, mcs, w_out, x, g_post, gt, g_pre, shift, scale):
    b, l, d = x.shape
    tm = MIX_TILE
    h = l // 2
    vec = pl.BlockSpec((1, d), lambda i, m: (0, 0))
    mod = pl.BlockSpec((None, 1, d), lambda i, m: (i, 0, 0))
    whole = lambda a: pl.BlockSpec(a.shape, lambda i, m: (0,) * a.ndim)
    return pl.pallas_call(
        _mix_kernel,
        grid=(b, l // tm),
        in_specs=[pl.BlockSpec((None, l, GROUP_W), lambda i, m: (i, 0, 0)),
                  pl.BlockSpec((None, tm, GROUP_W), lambda i, m: (i, m, 0)),
                  pl.BlockSpec((None, tm, GROUP_W), lambda i, m: (i, m, N_GROUPS - 1)),
                  whole(ch), whole(sh), whole(mcs), whole(w_out),
                  pl.BlockSpec((None, tm, d), lambda i, m: (i, m, 0)),
                  vec, mod, vec, mod, mod],
        out_specs=[pl.BlockSpec((None, tm, d), lambda i, m: (i, m, 0)),
                   pl.BlockSpec((None, tm, d), lambda i, m: (i, m, 0))],
        out_shape=[jax.ShapeDtypeStruct((b, l, d), F32),
                   jax.ShapeDtypeStruct((b, l, d), BF16)],
        scratch_shapes=[pltpu.VMEM((l, GROUP_W), BF16),
                        pltpu.VMEM((h, h), BF16),
                        pltpu.VMEM((h, h), BF16),
                        pltpu.VMEM((GROUP_W, GROUP_W), BF16),
                        pltpu.VMEM((GROUP_W, GROUP_W), BF16),
                        pltpu.VMEM(w_out.shape, BF16),
                        pltpu.VMEM((h, GROUP_W), BF16),
                        pltpu.VMEM((h, GROUP_W), BF16),
                        pltpu.VMEM((h + PAIR, GROUP_W), BF16)],
        compiler_params=_params(48, ("arbitrary", "arbitrary")),
        name="mix",
    )(pg, on, pg, ch, sh, mcs, w_out, x, g_post, gt, g_pre, shift, scale)


def _ffn_kernel(h_ref, x1_ref, wvg_ref, dwc_ref, wd_ref, gpost_ref, gt_ref,
                o_ref, gate_s, val_s):
    j = pl.program_id(1)
    l = h_ref.shape[0]
    tf = wd_ref.shape[0]
    n_rb = l // ROW_BLOCK
    pad = GRID_W

    gate_s[0:pad, :] = jnp.zeros((pad, tf), F32)
    gate_s[l + pad:l + 2 * pad, :] = jnp.zeros((pad, tf), F32)

    @pl.when(j == 0)
    def _():
        o_ref[...] = jnp.zeros(o_ref.shape, F32)

    n_ext = ROW_BLOCK + 2 * pad
    col = lax.broadcasted_iota(jnp.int32, (n_ext, tf), 0) % GRID_W
    not_first = jnp.where(col == 0, 0.0, 1.0)
    not_last = jnp.where(col == GRID_W - 1, 0.0, 1.0)
    dw = dwc_ref[...]

    def up(i):
        r0 = i * ROW_BLOCK
        vg = _dot(h_ref[r0:r0 + ROW_BLOCK, :], wvg_ref[...])
        val_s[r0:r0 + ROW_BLOCK, :] = vg[:, :tf]
        gate_s[r0 + pad:r0 + pad + ROW_BLOCK, :] = vg[:, tf:]

    def down(i):
        r0 = i * ROW_BLOCK
        ext = gate_s[r0:r0 + n_ext, :]
        shifted = (pltpu.roll(ext, 1, 0) * not_first, ext, pltpu.roll(ext, n_ext - 1, 0) * not_last)
        conv = None
        for dy in range(3):
            for dx in range(3):
                term = shifted[dx][dy * pad:dy * pad + ROW_BLOCK] * dw[3 * dy + dx:3 * dy + dx + 1]
                conv = term if conv is None else conv + term
        act = (_silu(conv) * val_s[r0:r0 + ROW_BLOCK, :]).astype(BF16)
        o_ref[r0:r0 + ROW_BLOCK, :] += _dot(act, wd_ref[...])

    up(0)
    up(1)
    for i in range(n_rb):
        if i + 2 < n_rb:
            up(i + 2)
        down(i)

    @pl.when(j == pl.num_programs(1) - 1)
    def _():
        def fin(rb, carry):
            r0 = pl.multiple_of(rb * ROW_BLOCK, ROW_BLOCK)
            y = _rms(o_ref[pl.ds(r0, ROW_BLOCK), :]) * gpost_ref[...]
            o_ref[pl.ds(r0, ROW_BLOCK), :] = x1_ref[pl.ds(r0, ROW_BLOCK), :] + gt_ref[...] * y
            return carry
        lax.fori_loop(0, n_rb, fin, 0)


def _ffn(h2, x1, w_vg, dwc, w_down, g_post, gt):
    b, l, d = x1.shape
    d_ff = w_down.shape[0]
    tf = FFN_TILE
    n_f = d_ff // tf
    return pl.pallas_call(
        _ffn_kernel,
        grid=(b, n_f),
        in_specs=[pl.BlockSpec((None, l, d), lambda i, j: (i, 0, 0)),
                  pl.BlockSpec((None, l, d), lambda i, j: (i, 0, 0)),
                  pl.BlockSpec((d, 2 * tf), lambda i, j: (0, j)),
                  pl.BlockSpec((9, tf), lambda i, j: (0, j)),
                  pl.BlockSpec((tf, d), lambda i, j: (j, 0)),
                  pl.BlockSpec((1, d), lambda i, j: (0, 0)),
                  pl.BlockSpec((None, 1, d), lambda i, j: (i, 0, 0))],
        out_specs=pl.BlockSpec((None, l, d), lambda i, j: (i, 0, 0)),
        out_shape=jax.ShapeDtypeStruct((b, l, d), F32),
        scratch_shapes=[pltpu.VMEM((l + 2 * GRID_W, tf), F32),
                        pltpu.VMEM((l, tf), F32)],
        compiler_params=_params(58, ("arbitrary", "arbitrary")),
        name="ffn",
    )(h2, x1, w_vg, dwc, w_down, g_post, gt)


def kernel(x, c, ctx, c_ctx, w_ada, b_ada, g_pre_mix, g_post_mix, g_pre_ffn, g_post_ffn,
           w_in, w_qkv_conv, a_log, dt_bias, g_gdn, w_fourier, w_out, w_up, w_dwc, w_down):
    b, l, d = x.shape
    assert w_ada.shape[0] == 1, "single-layer stack"
    assert b < 16 and l % ROW_BLOCK == 0 and ctx.shape[1] % ROW_BLOCK == 0
    assert w_in.shape[2] == N_GROUPS * GROUP_W + N_GATES and d == 2 * GROUP_W

    cc = jnp.zeros((16, d), F32).at[:b].set(c).at[b].set(c_ctx)
    mod = _ada(cc, w_ada[0], b_ada)
    sh1, sc1, gt1, sh2, sc2, gt2 = [m[:, None, :] for m in jnp.split(mod[:b], 6, axis=-1)]
    sh1c, sc1c = [m[None, :] for m in jnp.split(mod[b], 6, axis=-1)[:2]]

    wab = jnp.pad(w_in[0][:, N_GROUPS * GROUP_W:], ((0, 0), (0, PAIR - N_GATES)))
    zeros8 = jnp.zeros((8,), F32)
    alog = jnp.concatenate([zeros8, a_log[0].reshape(-1)])[:, None]
    dtb = jnp.concatenate([zeros8, dt_bias[0].reshape(-1)])[:, None]

    pg, kt, gat = _inproj(x, ctx, g_pre_mix, sh1, sc1, sh1c, sc1c, w_in[0].astype(BF16), wab, w_qkv_conv[0], alog, dtb)
    on = _gdn(pg, kt, gat, g_gdn, l)

    ch, sh, c_c, s_c = _dft_constants(l, HEAD_D)
    mcs = _fourier_w(c_c, s_c, w_fourier[0])
    x1, h2 = _mix(pg, on, ch, sh, mcs, w_out[0], x, g_post_mix, gt1, g_pre_ffn, sh2, sc2)
    d_ff = w_down.shape[1]
    w_vg = (w_up[0].reshape(d, 2, d_ff // FFN_TILE, FFN_TILE).transpose(0, 2, 1, 3)
            .reshape(d, 2 * d_ff).astype(BF16))
    return _ffn(h2, x1, w_vg, w_dwc[0].reshape(9, d_ff), w_down[0].astype(BF16), g_post_ffn, gt2)
```

```python
import functools
import itertools
import math

import numpy as np
import jax
import jax.numpy as jnp
from jax import lax
from jax.experimental import pallas as pl
from jax.experimental.pallas import tpu as pltpu

F32 = jnp.float32
BF16 = jnp.bfloat16

GRID_W = 64
HEADS = 4
HEAD_D = 128
CHUNK = 64
PAIR = 2 * CHUNK
GROUP_W = HEADS * HEAD_D
N_GROUPS = 5
N_GATES = 16
EPS = 1e-6
ROW_BLOCK = 256
FFN_TILE = 256
MIX_TILE = 512
FFN_ROWS = 512
MIB = 1024 * 1024

_NT = (((1,), (1,)), ((), ()))


def _dot(a, b):
    return jnp.dot(a, b, preferred_element_type=F32)


def _silu(x):
    return x / (1.0 + jnp.exp(-x))


def _rms(x):
    return x * lax.rsqrt(jnp.mean(x * x, axis=-1, keepdims=True) + EPS)


def _params(vmem_mib, semantics):
    return pltpu.CompilerParams(dimension_semantics=semantics,
                                vmem_limit_bytes=vmem_mib * MIB)


def _ada_kernel(c_ref, w_ref, b_ref, o_ref):
    s = _silu(c_ref[...]).astype(BF16)
    o_ref[...] = _dot(s, w_ref[...].astype(BF16)) + b_ref[...]


def _ada(cc, w, b):
    rows, d = cc.shape
    n = w.shape[1]
    tn = 1024
    return pl.pallas_call(
        _ada_kernel,
        grid=(n // tn,),
        in_specs=[pl.BlockSpec((rows, d), lambda j: (0, 0)),
                  pl.BlockSpec((d, tn), lambda j: (0, j)),
                  pl.BlockSpec((1, tn), lambda j: (0, j))],
        out_specs=pl.BlockSpec((rows, tn), lambda j: (0, j)),
        out_shape=jax.ShapeDtypeStruct((rows, n), F32),
        compiler_params=_params(32, ("arbitrary",)),
        name="ada",
    )(cc, w, b)


def _inproj_kernel(x_ref, ctx_ref, g_ref, sh_ref, sc_ref, shc_ref, scc_ref, w_ref, wab_ref,
                   conv_ref, alog_ref, dtb_ref, pg_ref, kt_ref, gat_ref, h_s, w_s, p_s):
    j = pl.program_id(1)
    l, lc = x_ref.shape[0], ctx_ref.shape[0]
    blocks = [(x_ref, r, r, r + 8, sh_ref, sc_ref, min(FFN_ROWS, l - r)) for r in range(0, l, FFN_ROWS)]
    blocks += [(ctx_ref, r, l + r, l + r + 16, shc_ref, scc_ref, min(FFN_ROWS, lc - r))
               for r in range(0, lc, FFN_ROWS)]
    n_rb = len(blocks)

    w_s[...] = w_ref[...].astype(BF16)

    def out_rows(rb):
        return slice(blocks[rb][2], blocks[rb][2] + blocks[rb][6])

    def project(rb):
        return _dot(h_s[out_rows(rb), :], w_s[...])

    def pipelined(first, second, lead):
        for rb in range(min(lead, n_rb)):
            first(rb)
        for rb in range(n_rb):
            if rb + lead < n_rb:
                first(rb + lead)
            second(rb)

    @pl.when(j == 0)
    def _u():
        for r in (0, l + 8, l + lc + 16):
            p_s[r:r + 8, :] = jnp.zeros((8, GROUP_W), F32)
        wabt = wab_ref[...].T[:N_GATES].astype(BF16)
        row = lax.broadcasted_iota(jnp.int32, (N_GATES, PAIR), 0)

        def prologue(rb):
            src, r, dst, _, shift, scale, n = blocks[rb]
            h = _rms(src[r:r + n, :]) * g_ref[...]
            hb = (h * (1.0 + scale[...]) + shift[...]).astype(BF16)
            h_s[dst:dst + n, :] = hb
            for s in range(n // PAIR):
                ab = lax.dot_general(wabt, hb[s * PAIR:(s + 1) * PAIR], _NT,
                                     preferred_element_type=F32)
                beta = 1.0 / (1.0 + jnp.exp(-ab))
                xs = ab + dtb_ref[...]
                softplus = jnp.maximum(xs, 0.0) + jnp.log1p(jnp.exp(-jnp.abs(xs)))
                gate = -jnp.exp(alog_ref[...]) * softplus
                gat_ref[dst // PAIR + s] = jnp.where(row < 8, beta, gate)

        def store_u(rb):
            pg_ref[out_rows(rb), :] = project(rb).astype(BF16)

        pipelined(prologue, store_u, 2)

    @pl.when(j == N_GROUPS - 1)
    def _z():
        for rb in range(n_rb):
            pg_ref[out_rows(rb), :] = _silu(project(rb)).astype(BF16)

    def conv_group(normalise, transpose):
        cw = conv_ref[...]

        def raw(rb):
            p0, n = blocks[rb][3], blocks[rb][6]
            p_s[p0:p0 + n, :] = project(rb)

        def finish(rb):
            p0, n = blocks[rb][3], blocks[rb][6]
            ext = p_s[p0 - 8:p0 + n + 8, :]
            prev = pltpu.roll(ext, 1, 0)[8:8 + n]
            nxt = pltpu.roll(ext, n + 15, 0)[8:8 + n]
            cur = ext[8:8 + n]
            a = _silu(prev * cw[0:1] + cur * cw[1:2] + nxt * cw[2:3])
            if normalise:
                segs = []
                for hh in range(HEADS):
                    seg = a[:, hh * HEAD_D:(hh + 1) * HEAD_D]
                    segs.append(seg * lax.rsqrt(jnp.sum(seg * seg, axis=-1, keepdims=True) + EPS))
                a = jnp.concatenate(segs, axis=1)
            pg_ref[out_rows(rb), :] = a.astype(BF16)
            if transpose:
                at = a.T.astype(BF16)
                for s in range(n // PAIR):
                    kt_ref[blocks[rb][2] // PAIR + s] = at[:, s * PAIR:(s + 1) * PAIR]

        pipelined(raw, finish, 2)

    @pl.when(j == 1)
    def _q():
        conv_group(True, False)

    @pl.when(j == 2)
    def _k():
        conv_group(True, True)

    @pl.when(j == 3)
    def _v():
        conv_group(False, False)


def _inproj(x, ctx, g, shift, scale, shift_c, scale_c, w_in, wab, conv, alog, dtb):
    b, l, d = x.shape
    lt = l + ctx.shape[1]
    n_pairs = lt // PAIR
    vec = pl.BlockSpec((1, d), lambda i, j: (0, 0))
    mod = pl.BlockSpec((None, 1, d), lambda i, j: (i, 0, 0))
    return pl.pallas_call(
        _inproj_kernel,
        grid=(b, N_GROUPS),
        in_specs=[pl.BlockSpec((None, l, d), lambda i, j: (i, 0, 0)),
                  pl.BlockSpec((None, lt - l, d), lambda i, j: (i, 0, 0)),
                  vec, mod, mod, vec, vec,
                  pl.BlockSpec((d, GROUP_W), lambda i, j: (0, j)),
                  pl.BlockSpec((d, PAIR), lambda i, j: (0, 0)),
                  pl.BlockSpec((3, GROUP_W), lambda i, j: (0, jnp.clip(j - 1, 0, 2))),
                  pl.BlockSpec((N_GATES, 1), lambda i, j: (0, 0)),
                  pl.BlockSpec((N_GATES, 1), lambda i, j: (0, 0))],
        out_specs=[pl.BlockSpec((None, lt, GROUP_W), lambda i, j: (i, 0, j)),
                   pl.BlockSpec((None, n_pairs, GROUP_W, PAIR), lambda i, j: (i, 0, 0, 0)),
                   pl.BlockSpec((None, n_pairs, N_GATES, PAIR), lambda i, j: (i, 0, 0, 0))],
        out_shape=[jax.ShapeDtypeStruct((b, lt, N_GROUPS * GROUP_W), BF16),
                   jax.ShapeDtypeStruct((b, n_pairs, GROUP_W, PAIR), BF16),
                   jax.ShapeDtypeStruct((b, n_pairs, N_GATES, PAIR), F32)],
        scratch_shapes=[pltpu.VMEM((lt, d), BF16),
                        pltpu.VMEM((d, GROUP_W), BF16),
                        pltpu.VMEM((lt + 24, GROUP_W), F32)],
        compiler_params=_params(52, ("arbitrary", "arbitrary")),
        name="inproj",
    )(x, ctx, g, shift, scale, shift_c, scale_c, w_in, wab, conv, alog, dtb)


def _bd2(a, b):
    z = jnp.zeros_like(a)
    return jnp.concatenate([jnp.concatenate([a, z], axis=1),
                            jnp.concatenate([z, b], axis=1)], axis=0)


def _bd_of(x):
    xb = x.astype(BF16)
    return _bd2(xb[:, :HEAD_D], xb[:, HEAD_D:])


def _round_robin(*stage_generators):
    for _ in itertools.zip_longest(*stage_generators):
        pass


def _gdn_kernel(q_ref, k_ref, v_ref, kt_ref, gat_ref, g_ref, y_ref,
                u_s, wq_s, at_s, kdt_s, egl_s, s_s, o_s):
    lt, l = q_ref.shape[0], y_ref.shape[0]
    n_steps = lt // PAIR
    n_ctx = (lt - l) // PAIR
    per_iter = 2
    assert n_steps % per_iter == 0

    ri = lax.broadcasted_iota(jnp.int32, (PAIR, PAIR), 0)
    ci = lax.broadcasted_iota(jnp.int32, (PAIR, PAIR), 1)
    same = (ri // CHUNK) == (ci // CHUNK)
    incl = (jnp.where(same & (ri >= ci), 1.0, 0.0), jnp.where(same & (ri <= ci), 1.0, 0.0))
    strict = (jnp.where(same & (ri > ci), 1.0, 0.0), jnp.where(same & (ri < ci), 1.0, 0.0))
    tri = (incl[1], incl[0])
    eye = jnp.where(ri == ci, 1.0, 0.0)
    eye2 = jnp.concatenate([eye, eye], axis=1)
    off = ([], [])
    for lvl in range(6):
        b = 2 ** lvl
        joined = (ri // (2 * b)) == (ci // (2 * b))
        off[0].append(jnp.where(joined & (ri % (2 * b) >= b) & (ci % (2 * b) < b), 1.0, 0.0))
        off[1].append(jnp.where(joined & (ri % (2 * b) < b) & (ci % (2 * b) >= b), 1.0, 0.0))

    def joining_blocks(a, d, lvl):
        m = off[d][lvl]
        return jnp.concatenate([a[:, :HEAD_D] * m, a[:, HEAD_D:] * m], axis=1)

    lane = lax.broadcasted_iota(jnp.int32, (1, PAIR), 1)
    first_half = lane < CHUNK
    scale = HEAD_D ** -0.5
    hi = lax.Precision.HIGHEST
    lanes = [slice(hp * 2 * HEAD_D, (hp + 1) * 2 * HEAD_D) for hp in range(HEADS // 2)]
    stream_ids = [(hp, d) for hp in range(HEADS // 2) for d in range(2)]
    zeros_half = jnp.zeros((CHUNK, 2 * HEAD_D), BF16)

    def pair_of(step, d):
        if d == 0:
            return lax.rem(step + (n_steps - n_ctx), n_steps)
        return n_steps - 1 - step

    def prepare(slot, step0):
        streams = []
        for kk in range(per_iter):
            for d in range(2):
                p = pair_of(step0 + kk, d)
                t0 = pl.multiple_of(p * PAIR, PAIR)
                gates = gat_ref[p]
                kt_all = kt_ref[p]
                gc = jnp.dot(gates, tri[d], precision=hi, preferred_element_type=F32)
                for hp in range(HEADS // 2):
                    k2 = k_ref[pl.ds(t0, PAIR), lanes[hp]]
                    q2 = q_ref[pl.ds(t0, PAIR), lanes[hp]]
                    v2 = v_ref[pl.ds(t0, PAIR), lanes[hp]]
                    kt2 = kt_all[hp * 2 * HEAD_D:(hp + 1) * 2 * HEAD_D, :]
                    bd_kt = _bd2(kt2[:HEAD_D], kt2[HEAD_D:])
                    streams.append(dict(kk=kk, d=d, hp=hp, gates=gates, gc=gc, k2=k2, q2=q2, v2=v2,
                                        kt2=kt2, kk_=_dot(k2, bd_kt), qk_=_dot(q2, bd_kt)))
        yield
        for st in streams:
            kk, d, hp, gates, gc = st["kk"], st["d"], st["hp"], st["gates"], st["gc"]
            gc_t = gc.T
            a, attn, qd, kdt, erow = [], [], [], [], []
            for hh in range(2):
                h = 2 * hp + hh
                hs = slice(hh * HEAD_D, (hh + 1) * HEAD_D)
                brow = gates[4 * d + h:4 * d + h + 1, :]
                gcr = gc[8 + 4 * d + h:9 + 4 * d + h, :]
                gcc = gc_t[:, 8 + 4 * d + h:9 + 4 * d + h]
                decay = jnp.exp((gcc - gcr) * incl[d]) * incl[d]
                a.append(st["kk_"][:, hs] * decay * strict[d] * brow)
                attn.append(st["qk_"][:, hs] * decay * (brow * scale))
                qd.append(st["q2"][:, hs].astype(F32) * (jnp.exp(gcc) * scale))
                if d == 0:
                    gl0, gl1 = gcr[:, CHUNK - 1:CHUNK], gcr[:, PAIR - 1:PAIR]
                else:
                    gl0, gl1 = gcr[:, 0:1], gcr[:, CHUNK:CHUNK + 1]
                glr = jnp.where(first_half, gl0, gl1)
                kdt.append(st["kt2"][hs, :].astype(F32) * (jnp.exp(glr - gcr) * brow))
                erow.append(jnp.exp(gcr))
                egl_s[slot, d, kk, 0, h:h + 1, :] = jnp.broadcast_to(jnp.exp(gl0), (1, PAIR))
                egl_s[slot, d, kk, 1, h:h + 1, :] = jnp.broadcast_to(jnp.exp(gl1), (1, PAIR))
            st["a"] = jnp.concatenate(a, axis=1)
            st["erow"] = jnp.concatenate(erow, axis=1)
            at_s[slot, d, kk, :, lanes[hp]] = jnp.concatenate(attn, axis=1).astype(BF16)
            kdt_s[slot, d, kk, hp] = jnp.concatenate(kdt, axis=1).astype(BF16)
            st["qd"] = jnp.concatenate(qd, axis=1).astype(BF16)
            st["t"] = eye2 - joining_blocks(st["a"], d, 0)
        yield
        for lvl in range(1, 6):
            for st in streams:
                st["y"] = _dot(joining_blocks(st["a"], st["d"], lvl).astype(BF16), _bd_of(st["t"]))
            yield
            for st in streams:
                st["t"] = st["t"] - _dot(st["t"].astype(BF16), _bd_of(st["y"]))
            yield
        for st in streams:
            t = st["t"]
            st["u"] = _dot(t.astype(BF16), _bd2(st["v2"][:, :HEAD_D], st["v2"][:, HEAD_D:]))
            st["w"] = _dot((t * st["erow"]).astype(BF16), _bd2(st["k2"][:, :HEAD_D], st["k2"][:, HEAD_D:]))
        yield
        for st in streams:
            kk, d, hp = st["kk"], st["d"], st["hp"]
            u_s[slot, d, kk, :, lanes[hp]] = st["u"].astype(BF16)
            wb, qdb = st["w"].astype(BF16), st["qd"]
            wq_s[slot, d, kk, :, lanes[hp]] = jnp.concatenate(
                [wb[:CHUNK], qdb[:CHUNK], wb[CHUNK:], qdb[CHUNK:]], axis=0)
        yield

    def scan(slot, step0):
        state = {(hp, d): s_s[d, :, lanes[hp]] for (hp, d) in stream_ids}
        for kk in range(per_iter):
            for half in range(2):
                e = {0: half, 1: 1 - half}
                res, bd_x = {}, {}
                for (hp, d) in stream_ids:
                    res[(hp, d)] = _dot(wq_s[slot, d, kk, e[d] * PAIR:(e[d] + 1) * PAIR, lanes[hp]],
                                        _bd_of(state[(hp, d)]))
                yield
                for (hp, d) in stream_ids:
                    vn = (u_s[slot, d, kk, e[d] * CHUNK:(e[d] + 1) * CHUNK, lanes[hp]].astype(F32)
                          - res[(hp, d)][:CHUNK])
                    vnb = vn.astype(BF16)
                    x = jnp.concatenate([vnb, zeros_half] if e[d] == 0 else [zeros_half, vnb], axis=0)
                    bd_x[(hp, d)] = _bd2(x[:, :HEAD_D], x[:, HEAD_D:])
                for (hp, d) in stream_ids:
                    both = _dot(jnp.concatenate(
                        [kdt_s[slot, d, kk, hp],
                         at_s[slot, d, kk, e[d] * CHUNK:(e[d] + 1) * CHUNK, lanes[hp]]], axis=0),
                        bd_x[(hp, d)])
                    eg = jnp.concatenate([egl_s[slot, d, kk, e[d], 2 * hp:2 * hp + 1, :],
                                          egl_s[slot, d, kk, e[d], 2 * hp + 1:2 * hp + 2, :]], axis=1)
                    state[(hp, d)] = state[(hp, d)] * eg + both[:HEAD_D]
                    r0 = pl.multiple_of(pair_of(step0 + kk, d) * PAIR + e[d] * CHUNK, CHUNK)
                    o_s[pl.ds(r0, CHUNK), lanes[hp]] += res[(hp, d)][CHUNK:] + both[HEAD_D:]
                yield
        for (hp, d) in stream_ids:
            s_s[d, :, lanes[hp]] = state[(hp, d)]

    s_s[...] = jnp.zeros(s_s.shape, F32)
    o_s[...] = jnp.zeros(o_s.shape, F32)
    n_iter = n_steps // per_iter
    _round_robin(prepare(0, 0))

    def body(it, carry):
        _round_robin(prepare(lax.rem(it + 1, 2), (it + 1) * per_iter), scan(lax.rem(it, 2), it * per_iter))
        return carry

    lax.fori_loop(0, n_iter - 1, body, 0)
    _round_robin(scan((n_iter - 1) % 2, (n_iter - 1) * per_iter))

    def epilogue(rb, carry):
        r0 = pl.multiple_of(rb * ROW_BLOCK, ROW_BLOCK)
        o = o_s[pl.ds(r0, ROW_BLOCK), :]
        segs = [_rms(o[:, h * HEAD_D:(h + 1) * HEAD_D]) * g_ref[...] for h in range(HEADS)]
        y_ref[pl.ds(r0, ROW_BLOCK), :] = jnp.concatenate(segs, axis=1).astype(BF16)
        return carry

    lax.fori_loop(0, l // ROW_BLOCK, epilogue, 0)


def _gdn(pg, kt, gat, g_gdn, l):
    b, lt, _ = pg.shape
    n_pairs = lt // PAIR

    def group(g):
        return pl.BlockSpec((None, lt, GROUP_W), lambda i, g=g: (i, 0, g))

    ring = (2, 2, 2)
    return pl.pallas_call(
        _gdn_kernel,
        grid=(b,),
        in_specs=[group(1), group(2), group(3),
                  pl.BlockSpec((None, n_pairs, GROUP_W, PAIR), lambda i: (i, 0, 0, 0)),
                  pl.BlockSpec((None, n_pairs, N_GATES, PAIR), lambda i: (i, 0, 0, 0)),
                  pl.BlockSpec((1, HEAD_D), lambda i: (0, 0))],
        out_specs=pl.BlockSpec((None, l, GROUP_W), lambda i: (i, 0, 0)),
        out_shape=jax.ShapeDtypeStruct((b, l, GROUP_W), BF16),
        scratch_shapes=[pltpu.VMEM(ring + (PAIR, GROUP_W), BF16),
                        pltpu.VMEM(ring + (2 * PAIR, GROUP_W), BF16),
                        pltpu.VMEM(ring + (PAIR, GROUP_W), BF16),
                        pltpu.VMEM(ring + (HEADS // 2, HEAD_D, 2 * PAIR), BF16),
                        pltpu.VMEM(ring + (2, 8, PAIR), F32),
                        pltpu.VMEM((2, HEAD_D, GROUP_W), F32),
                        pltpu.VMEM((lt, GROUP_W), F32)],
        compiler_params=_params(48, ("arbitrary",)),
        name="gdn",
    )(pg, pg, pg, kt, gat, g_gdn)


def _fourier_w_kernel(cc_ref, sc_ref, w_ref, o_ref):
    hi = lax.Precision.HIGHEST
    for g in range(w_ref.shape[0]):
        w = w_ref[g]
        o_ref[g] = jnp.concatenate(
            [jnp.dot(cc_ref[...], w, precision=hi, preferred_element_type=F32),
             -jnp.dot(sc_ref[...], w, precision=hi, preferred_element_type=F32)], axis=1)


def _fourier_w(cc, sc, w_f):
    groups, c, d = w_f.shape
    return pl.pallas_call(
        _fourier_w_kernel,
        out_shape=jax.ShapeDtypeStruct((groups, c, 2 * d), F32),
        name="fourier_w",
    )(cc, sc, w_f)


@functools.lru_cache(maxsize=None)
def _dft_constants(length, channels):
    def cos_sin(n, keep):
        idx = np.arange(keep, dtype=np.int64)
        ang = (np.outer(idx, idx) % n).astype(np.float64) * (2.0 * np.pi / n)
        return np.cos(ang), np.sin(ang)
    c_l, s_l = cos_sin(length, length // 2)
    c_c, s_c = cos_sin(channels, channels)
    norm = 1.0 / math.sqrt(length * channels)
    return (jnp.asarray(c_l.astype(np.float32)), jnp.asarray(s_l.astype(np.float32)),
            jnp.asarray((c_c * norm).astype(np.float32)),
            jnp.asarray((s_c * norm).astype(np.float32)))


def _mix_kernel(u_ref, on_ref, z_ref, ch_ref, sh_ref, mcs_ref, wout_ref, x_ref, gpost_ref, gt_ref,
                gpre_ref, shift_ref, scale_ref, x1_ref, h2_ref,
                yf_s, ch_s, sh_s, mc_s, ms_s, wout_s, ve_s, vo_s, d_s):
    i, m = pl.program_id(0), pl.program_id(1)
    l = u_ref.shape[0]
    h = l // 2
    blk = PAIR
    n_blk = h // blk

    @pl.when((i == 0) & (m == 0))
    def _():
        ch_s[...] = ch_ref[...].astype(BF16)
        sh_s[...] = sh_ref[...].astype(BF16)
        wout_s[...] = wout_ref[...].astype(BF16)
        mc_s[...] = jnp.zeros(mc_s.shape, BF16)
        ms_s[...] = jnp.zeros(ms_s.shape, BF16)
        for g in range(HEADS):
            gs = slice(g * HEAD_D, (g + 1) * HEAD_D)
            mc_s[gs, gs] = mcs_ref[g][:, :HEAD_D].astype(BF16)
            ms_s[gs, gs] = mcs_ref[g][:, HEAD_D:].astype(BF16)

    @pl.when(m == 0)
    def _():
        ri = lax.broadcasted_iota(jnp.int32, (blk, 2 * blk), 0)
        ci = lax.broadcasted_iota(jnp.int32, (blk, 2 * blk), 1)
        rev = jnp.where(((ci < blk) & (ri + ci == blk)) | ((ri == 0) & (ci == blk)), 1.0, 0.0).astype(BF16)
        zero_blk = jnp.zeros((blk, GROUP_W), BF16)

        def reversed_block(ref, base, k, last):
            top = ref[base + (n_blk - 1 - k) * blk:base + (n_blk - k) * blk, :]
            bottom = last if k == 0 else ref[base + (n_blk - k) * blk:base + (n_blk - k + 1) * blk, :]
            return _dot(rev, jnp.concatenate([top, bottom], axis=0))

        for k in range(n_blk):
            low = u_ref[k * blk:(k + 1) * blk, :].astype(F32)
            mirrored = reversed_block(u_ref, h, k, zero_blk)
            ve_s[k * blk:(k + 1) * blk, :] = _dot((low + mirrored).astype(BF16), mc_s[...]).astype(BF16)
            vo_s[k * blk:(k + 1) * blk, :] = _dot((low - mirrored).astype(BF16), ms_s[...]).astype(BF16)
        sign_row = jnp.where(lax.broadcasted_iota(jnp.int32, (16, l), 1) % 2 == 0, 1.0, -1.0)
        sign_row = jnp.where(lax.broadcasted_iota(jnp.int32, (16, l), 0) == 0, sign_row, 0.0).astype(BF16)
        alt_u = _dot(sign_row, u_ref[...])
        y_nyq = _dot(alt_u.astype(BF16), mc_s[...])[0:1, :]
        v_nyq = _dot(u_ref[h:h + 16, :], mc_s[...])[0:1, :]
        sign_col = jnp.where(lax.broadcasted_iota(jnp.int32, (blk, GROUP_W), 0) % 2 == 0, 1.0, -1.0)
        first_row = lax.broadcasted_iota(jnp.int32, (blk, GROUP_W), 0) == 0
        d_s[h:h + blk, :] = jnp.where(first_row, y_nyq, 0.0).astype(BF16)
        for k in range(n_blk):
            rows = slice(k * blk, (k + 1) * blk)
            a = _dot(ch_s[rows, :], ve_s[...]) + sign_col * v_nyq
            b = _dot(sh_s[rows, :], vo_s[...])
            yf_s[rows, :] = (a + b).astype(BF16)
            d_s[rows, :] = (a - b).astype(BF16)
        for k in range(n_blk):
            yf_s[h + k * blk:h + (k + 1) * blk, :] = reversed_block(
                d_s, 0, k, d_s[h:h + blk, :]).astype(BF16)

    tm = x_ref.shape[0]
    half = wout_s.shape[0] // 2
    subs = [slice(sb * ROW_BLOCK, (sb + 1) * ROW_BLOCK) for sb in range(tm // ROW_BLOCK)]
    ys = []
    for sb, rs in enumerate(subs):
        r0 = pl.multiple_of(m * tm + sb * ROW_BLOCK, ROW_BLOCK)
        yd = (on_ref[rs, :].astype(F32) * z_ref[rs, :].astype(F32)).astype(BF16)
        ys.append(_dot(yf_s[pl.ds(r0, ROW_BLOCK), :], wout_s[:half, :]) + _dot(yd, wout_s[half:, :]))
    for rs, y in zip(subs, ys):
        x1 = x_ref[rs, :] + gt_ref[...] * (_rms(y) * gpost_ref[...])
        x1_ref[rs, :] = x1
        h2 = _rms(x1) * gpre_ref[...]
        h2_ref[rs, :] = (h2 * (1.0 + scale_ref[...]) + shift_ref[...]).astype(BF16)


def _mix(pg, on, ch, sh, mcs, w_out, x, g_post, gt, g_pre, shift, scale):
    b, l, d = x.shape
    tm = MIX_TILE
    h = l // 2
    vec = pl.BlockSpec((1, d), lambda i, m: (0, 0))
    mod = pl.BlockSpec((None, 1, d), lambda i, m: (i, 0, 0))
    whole = lambda a: pl.BlockSpec(a.shape, lambda i, m: (0,) * a.ndim)
    return pl.pallas_call(
        _mix_kernel,
        grid=(b, l // tm),
        in_specs=[pl.BlockSpec((None, l, GROUP_W), lambda i, m: (i, 0, 0)),
                  pl.BlockSpec((None, tm, GROUP_W), lambda i, m: (i, m, 0)),
                  pl.BlockSpec((None, tm, GROUP_W), lambda i, m: (i, m, N_GROUPS - 1)),
                  whole(ch), whole(sh), whole(mcs), whole(w_out),
                  pl.BlockSpec((None, tm, d), lambda i, m: (i, m, 0)),
                  vec, mod, vec, mod, mod],
        out_specs=[pl.BlockSpec((None, tm, d), lambda i, m: (i, m, 0)),
                   pl.BlockSpec((None, tm, d), lambda i, m: (i, m, 0))],
        out_shape=[jax.ShapeDtypeStruct((b, l, d), F32),
                   jax.ShapeDtypeStruct((b, l, d), BF16)],
        scratch_shapes=[pltpu.VMEM((l, GROUP_W), BF16),
                        pltpu.VMEM((h, h), BF16),
                        pltpu.VMEM((h, h), BF16),
                        pltpu.VMEM((GROUP_W, GROUP_W), BF16),
                        pltpu.VMEM((GROUP_W, GROUP_W), BF16),
                        pltpu.VMEM(w_out.shape, BF16),
                        pltpu.VMEM((h, GROUP_W), BF16),
                        pltpu.VMEM((h, GROUP_W), BF16),
                        pltpu.VMEM((h + PAIR, GROUP_W), BF16)],
        compiler_params=_params(48, ("arbitrary", "arbitrary")),
        name="mix",
    )(pg, on, pg, ch, sh, mcs, w_out, x, g_post, gt, g_pre, shift, scale)


def _ffn_kernel(h_ref, x1_ref, wv_ref, wg_ref, dwc_ref, wd_ref, gpost_ref, gt_ref,
                o_ref, gate_s, val_s, wvg_s, wd_s):
    ROW_BLOCK = FFN_ROWS
    j = pl.program_id(1)
    l = h_ref.shape[0]
    tf = wv_ref.shape[1]
    n_rb = l // ROW_BLOCK
    pad = GRID_W

    wvg_s[:, :tf] = wv_ref[...].astype(BF16)
    wvg_s[:, tf:] = wg_ref[...].astype(BF16)
    wd_s[...] = wd_ref[...].astype(BF16)
    gate_s[0:pad, :] = jnp.zeros((pad, tf), F32)
    gate_s[l + pad:l + 2 * pad, :] = jnp.zeros((pad, tf), F32)

    @pl.when(j == 0)
    def _():
        o_ref[...] = jnp.zeros(o_ref.shape, F32)

    n_ext = ROW_BLOCK + 2 * pad
    col = lax.broadcasted_iota(jnp.int32, (n_ext, tf), 0) % GRID_W
    not_first = jnp.where(col == 0, 0.0, 1.0)
    not_last = jnp.where(col == GRID_W - 1, 0.0, 1.0)
    dw = dwc_ref[...]

    def up(i):
        r0 = i * ROW_BLOCK
        vg = _dot(h_ref[r0:r0 + ROW_BLOCK, :], wvg_s[...])
        val_s[r0:r0 + ROW_BLOCK, :] = vg[:, :tf]
        gate_s[r0 + pad:r0 + pad + ROW_BLOCK, :] = vg[:, tf:]

    def down(i):
        r0 = i * ROW_BLOCK
        ext = gate_s[r0:r0 + n_ext, :]
        shifted = (pltpu.roll(ext, 1, 0) * not_first, ext, pltpu.roll(ext, n_ext - 1, 0) * not_last)
        conv = None
        for dy in range(3):
            for dx in range(3):
                term = shifted[dx][dy * pad:dy * pad + ROW_BLOCK] * dw[3 * dy + dx:3 * dy + dx + 1]
                conv = term if conv is None else conv + term
        act = (_silu(conv) * val_s[r0:r0 + ROW_BLOCK, :]).astype(BF16)
        o_ref[r0:r0 + ROW_BLOCK, :] += _dot(act, wd_s[...])

    for i in range(min(2, n_rb)):
        up(i)
    for i in range(n_rb):
        if i + 2 < n_rb:
            up(i + 2)
        down(i)

    @pl.when(j == pl.num_programs(1) - 1)
    def _():
        def fin(rb, carry):
            r0 = pl.multiple_of(rb * ROW_BLOCK, ROW_BLOCK)
            y = _rms(o_ref[pl.ds(r0, ROW_BLOCK), :]) * gpost_ref[...]
            o_ref[pl.ds(r0, ROW_BLOCK), :] = x1_ref[pl.ds(r0, ROW_BLOCK), :] + gt_ref[...] * y
            return carry
        lax.fori_loop(0, n_rb, fin, 0)


def _ffn(h2, x1, w_up, dwc, w_down, g_post, gt):
    b, l, d = x1.shape
    d_ff = w_down.shape[0]
    tf = FFN_TILE
    n_f = d_ff // tf
    return pl.pallas_call(
        _ffn_kernel,
        grid=(b, n_f),
        in_specs=[pl.BlockSpec((None, l, d), lambda i, j: (i, 0, 0)),
                  pl.BlockSpec((None, l, d), lambda i, j: (i, 0, 0)),
                  pl.BlockSpec((d, tf), lambda i, j: (0, j)),
                  pl.BlockSpec((d, tf), lambda i, j: (0, n_f + j)),
                  pl.BlockSpec((9, tf), lambda i, j: (0, j)),
                  pl.BlockSpec((tf, d), lambda i, j: (j, 0)),
                  pl.BlockSpec((1, d), lambda i, j: (0, 0)),
                  pl.BlockSpec((None, 1, d), lambda i, j: (i, 0, 0))],
        out_specs=pl.BlockSpec((None, l, d), lambda i, j: (i, 0, 0)),
        out_shape=jax.ShapeDtypeStruct((b, l, d), F32),
        scratch_shapes=[pltpu.VMEM((l + 2 * GRID_W, tf), F32),
                        pltpu.VMEM((l, tf), F32),
                        pltpu.VMEM((d, 2 * tf), BF16),
                        pltpu.VMEM((tf, d), BF16)],
        compiler_params=_params(58, ("arbitrary", "arbitrary")),
        name="ffn",
    )(h2, x1, w_up, w_up, dwc, w_down, g_post, gt)


def kernel(x, c, ctx, c_ctx, w_ada, b_ada, g_pre_mix, g_post_mix, g_pre_ffn, g_post_ffn,
           w_in, w_qkv_conv, a_log, dt_bias, g_gdn, w_fourier, w_out, w_up, w_dwc, w_down):
    b, l, d = x.shape
    assert w_ada.shape[0] == 1, "single-layer stack"
    assert b < 16 and l % ROW_BLOCK == 0 and ctx.shape[1] % ROW_BLOCK == 0
    assert w_in.shape[2] == N_GROUPS * GROUP_W + N_GATES and d == 2 * GROUP_W

    cc = jnp.zeros((16, d), F32).at[:b].set(c).at[b].set(c_ctx)
    mod = _ada(cc, w_ada[0], b_ada)
    sh1, sc1, gt1, sh2, sc2, gt2 = [m[:, None, :] for m in jnp.split(mod[:b], 6, axis=-1)]
    sh1c, sc1c = [m[None, :] for m in jnp.split(mod[b], 6, axis=-1)[:2]]

    wab = jnp.pad(w_in[0][:, N_GROUPS * GROUP_W:], ((0, 0), (0, PAIR - N_GATES)))
    zeros8 = jnp.zeros((8,), F32)
    alog = jnp.concatenate([zeros8, a_log[0].reshape(-1)])[:, None]
    dtb = jnp.concatenate([zeros8, dt_bias[0].reshape(-1)])[:, None]

    pg, kt, gat = _inproj(x, ctx, g_pre_mix, sh1, sc1, sh1c, sc1c, w_in[0], wab, w_qkv_conv[0], alog, dtb)
    on = _gdn(pg, kt, gat, g_gdn, l)

    ch, sh, c_c, s_c = _dft_constants(l, HEAD_D)
    mcs = _fourier_w(c_c, s_c, w_fourier[0])
    x1, h2 = _mix(pg, on, ch, sh, mcs, w_out[0], x, g_post_mix, gt1, g_pre_ffn, sh2, sc2)
    d_ff = w_down.shape[1]
    return _ffn(h2, x1, w_up[0], w_dwc[0].reshape(9, d_ff), w_down[0], g_post_ffn, gt2)
```

```python
import functools
import itertools
import math

import numpy as np
import jax
import jax.numpy as jnp
from jax import lax
from jax.experimental import pallas as pl
from jax.experimental.pallas import tpu as pltpu

F32 = jnp.float32
BF16 = jnp.bfloat16

GRID_W = 64
HEADS = 4
HEAD_D = 128
CHUNK = 64
PAIR = 2 * CHUNK
GROUP_W = HEADS * HEAD_D
N_GROUPS = 5
N_GATES = 16
EPS = 1e-6
ROW_BLOCK = 256
FFN_TILE = 256
MIX_TILE = 1024
FFN_ROWS = 512
MIB = 1024 * 1024

_NT = (((1,), (1,)), ((), ()))


def _dot(a, b):
    return jnp.dot(a, b, preferred_element_type=F32)


def _silu(x):
    return x / (1.0 + jnp.exp(-x))


def _rms(x):
    return x * lax.rsqrt(jnp.mean(x * x, axis=-1, keepdims=True) + EPS)


def _params(vmem_mib, semantics):
    return pltpu.CompilerParams(dimension_semantics=semantics,
                                vmem_limit_bytes=vmem_mib * MIB)


def _ada_kernel(c_ref, w_ref, b_ref, o_ref):
    s = _silu(c_ref[...]).astype(BF16)
    o_ref[...] = _dot(s, w_ref[...].astype(BF16)) + b_ref[...]


def _ada(cc, w, b):
    rows, d = cc.shape
    n = w.shape[1]
    tn = 1024
    return pl.pallas_call(
        _ada_kernel,
        grid=(n // tn,),
        in_specs=[pl.BlockSpec((rows, d), lambda j: (0, 0)),
                  pl.BlockSpec((d, tn), lambda j: (0, j)),
                  pl.BlockSpec((1, tn), lambda j: (0, j))],
        out_specs=pl.BlockSpec((rows, tn), lambda j: (0, j)),
        out_shape=jax.ShapeDtypeStruct((rows, n), F32),
        compiler_params=_params(32, ("arbitrary",)),
        name="ada",
    )(cc, w, b)


def _inproj_kernel(x_ref, ctx_ref, g_ref, sh_ref, sc_ref, shc_ref, scc_ref, w_ref, wab_ref,
                   conv_ref, alog_ref, dtb_ref, pg_ref, kt_ref, gat_ref, h_s, w_s, p_s):
    j = pl.program_id(1)
    l, lc = x_ref.shape[0], ctx_ref.shape[0]
    blocks = [(x_ref, r, r, r + 8, sh_ref, sc_ref, min(FFN_ROWS, l - r)) for r in range(0, l, FFN_ROWS)]
    blocks += [(ctx_ref, r, l + r, l + r + 16, shc_ref, scc_ref, min(FFN_ROWS, lc - r))
               for r in range(0, lc, FFN_ROWS)]
    n_rb = len(blocks)

    w_s[...] = w_ref[...].astype(BF16)

    def out_rows(rb):
        return slice(blocks[rb][2], blocks[rb][2] + blocks[rb][6])

    def project(rb):
        return _dot(h_s[out_rows(rb), :], w_s[...])

    def pipelined(first, second, lead):
        for rb in range(min(lead, n_rb)):
            first(rb)
        for rb in range(n_rb):
            if rb + lead < n_rb:
                first(rb + lead)
            second(rb)

    @pl.when(j == 0)
    def _u():
        for r in (0, l + 8, l + lc + 16):
            p_s[r:r + 8, :] = jnp.zeros((8, GROUP_W), F32)
        wabt = wab_ref[...].T[:N_GATES].astype(BF16)
        row = lax.broadcasted_iota(jnp.int32, (N_GATES, PAIR), 0)

        def prologue(rb):
            src, r, dst, _, shift, scale, n = blocks[rb]
            h = _rms(src[r:r + n, :]) * g_ref[...]
            hb = (h * (1.0 + scale[...]) + shift[...]).astype(BF16)
            h_s[dst:dst + n, :] = hb
            for s in range(n // PAIR):
                ab = lax.dot_general(wabt, hb[s * PAIR:(s + 1) * PAIR], _NT,
                                     preferred_element_type=F32)
                beta = 1.0 / (1.0 + jnp.exp(-ab))
                xs = ab + dtb_ref[...]
                softplus = jnp.maximum(xs, 0.0) + jnp.log1p(jnp.exp(-jnp.abs(xs)))
                gate = -jnp.exp(alog_ref[...]) * softplus
                gat_ref[dst // PAIR + s] = jnp.where(row < 8, beta, gate)

        def store_u(rb):
            pg_ref[out_rows(rb), :] = project(rb).astype(BF16)

        pipelined(prologue, store_u, 2)

    @pl.when(j == N_GROUPS - 1)
    def _z():
        for rb in range(n_rb):
            pg_ref[out_rows(rb), :] = _silu(project(rb)).astype(BF16)

    def conv_group(normalise, transpose):
        cw = conv_ref[...]

        def raw(rb):
            p0, n = blocks[rb][3], blocks[rb][6]
            p_s[p0:p0 + n, :] = project(rb)

        def finish(rb):
            p0, n = blocks[rb][3], blocks[rb][6]
            ext = p_s[p0 - 8:p0 + n + 8, :]
            prev = pltpu.roll(ext, 1, 0)[8:8 + n]
            nxt = pltpu.roll(ext, n + 15, 0)[8:8 + n]
            cur = ext[8:8 + n]
            a = _silu(prev * cw[0:1] + cur * cw[1:2] + nxt * cw[2:3])
            if normalise:
                segs = []
                for hh in range(HEADS):
                    seg = a[:, hh * HEAD_D:(hh + 1) * HEAD_D]
                    segs.append(seg * lax.rsqrt(jnp.sum(seg * seg, axis=-1, keepdims=True) + EPS))
                a = jnp.concatenate(segs, axis=1)
            pg_ref[out_rows(rb), :] = a.astype(BF16)
            if transpose:
                at = a.T.astype(BF16)
                for s in range(n // PAIR):
                    kt_ref[blocks[rb][2] // PAIR + s] = at[:, s * PAIR:(s + 1) * PAIR]

        pipelined(raw, finish, 2)

    @pl.when(j == 1)
    def _q():
        conv_group(True, False)

    @pl.when(j == 2)
    def _k():
        conv_group(True, True)

    @pl.when(j == 3)
    def _v():
        conv_group(False, False)


def _inproj(x, ctx, g, shift, scale, shift_c, scale_c, w_in, wab, conv, alog, dtb):
    b, l, d = x.shape
    lt = l + ctx.shape[1]
    n_pairs = lt // PAIR
    vec = pl.BlockSpec((1, d), lambda i, j: (0, 0))
    mod = pl.BlockSpec((None, 1, d), lambda i, j: (i, 0, 0))
    return pl.pallas_call(
        _inproj_kernel,
        grid=(b, N_GROUPS),
        in_specs=[pl.BlockSpec((None, l, d), lambda i, j: (i, 0, 0)),
                  pl.BlockSpec((None, lt - l, d), lambda i, j: (i, 0, 0)),
                  vec, mod, mod, vec, vec,
                  pl.BlockSpec((d, GROUP_W), lambda i, j: (0, j)),
                  pl.BlockSpec((d, PAIR), lambda i, j: (0, 0)),
                  pl.BlockSpec((3, GROUP_W), lambda i, j: (0, jnp.clip(j - 1, 0, 2))),
                  pl.BlockSpec((N_GATES, 1), lambda i, j: (0, 0)),
                  pl.BlockSpec((N_GATES, 1), lambda i, j: (0, 0))],
        out_specs=[pl.BlockSpec((None, lt, GROUP_W), lambda i, j: (i, 0, j)),
                   pl.BlockSpec((None, n_pairs, GROUP_W, PAIR), lambda i, j: (i, 0, 0, 0)),
                   pl.BlockSpec((None, n_pairs, N_GATES, PAIR), lambda i, j: (i, 0, 0, 0))],
        out_shape=[jax.ShapeDtypeStruct((b, lt, N_GROUPS * GROUP_W), BF16),
                   jax.ShapeDtypeStruct((b, n_pairs, GROUP_W, PAIR), BF16),
                   jax.ShapeDtypeStruct((b, n_pairs, N_GATES, PAIR), F32)],
        scratch_shapes=[pltpu.VMEM((lt, d), BF16),
                        pltpu.VMEM((d, GROUP_W), BF16),
                        pltpu.VMEM((lt + 24, GROUP_W), F32)],
        compiler_params=_params(52, ("arbitrary", "arbitrary")),
        name="inproj",
    )(x, ctx, g, shift, scale, shift_c, scale_c, w_in, wab, conv, alog, dtb)


def _bd2(a, b):
    z = jnp.zeros_like(a)
    return jnp.concatenate([jnp.concatenate([a, z], axis=1),
                            jnp.concatenate([z, b], axis=1)], axis=0)


def _bd_of(x):
    xb = x.astype(BF16)
    return _bd2(xb[:, :HEAD_D], xb[:, HEAD_D:])


def _round_robin(*stage_generators):
    for _ in itertools.zip_longest(*stage_generators):
        pass


def _gdn_kernel(q_ref, k_ref, v_ref, kt_ref, gat_ref, g_ref, y_ref,
                u_s, wq_s, at_s, kdt_s, egl_s, s_s, o_s):
    lt, l = q_ref.shape[0], y_ref.shape[0]
    n_steps = lt // PAIR
    n_ctx = (lt - l) // PAIR
    per_iter = 2
    assert n_steps % per_iter == 0

    ri = lax.broadcasted_iota(jnp.int32, (PAIR, PAIR), 0)
    ci = lax.broadcasted_iota(jnp.int32, (PAIR, PAIR), 1)
    same = (ri // CHUNK) == (ci // CHUNK)
    incl = (jnp.where(same & (ri >= ci), 1.0, 0.0), jnp.where(same & (ri <= ci), 1.0, 0.0))
    strict = (jnp.where(same & (ri > ci), 1.0, 0.0), jnp.where(same & (ri < ci), 1.0, 0.0))
    tri = (incl[1], incl[0])
    eye = jnp.where(ri == ci, 1.0, 0.0)
    eye2 = jnp.concatenate([eye, eye], axis=1)
    off = ([], [])
    for lvl in range(6):
        b = 2 ** lvl
        joined = (ri // (2 * b)) == (ci // (2 * b))
        off[0].append(jnp.where(joined & (ri % (2 * b) >= b) & (ci % (2 * b) < b), 1.0, 0.0))
        off[1].append(jnp.where(joined & (ri % (2 * b) < b) & (ci % (2 * b) >= b), 1.0, 0.0))

    def joining_blocks(a, d, lvl):
        m = off[d][lvl]
        return jnp.concatenate([a[:, :HEAD_D] * m, a[:, HEAD_D:] * m], axis=1)

    lane = lax.broadcasted_iota(jnp.int32, (1, PAIR), 1)
    first_half = lane < CHUNK
    scale = HEAD_D ** -0.5
    hi = lax.Precision.HIGHEST
    lanes = [slice(hp * 2 * HEAD_D, (hp + 1) * 2 * HEAD_D) for hp in range(HEADS // 2)]
    stream_ids = [(hp, d) for hp in range(HEADS // 2) for d in range(2)]
    zeros_half = jnp.zeros((CHUNK, 2 * HEAD_D), BF16)

    def pair_of(step, d):
        if d == 0:
            return lax.rem(step + (n_steps - n_ctx), n_steps)
        return n_steps - 1 - step

    def prepare(slot, step0):
        streams = []
        for kk in range(per_iter):
            for d in range(2):
                p = pair_of(step0 + kk, d)
                t0 = pl.multiple_of(p * PAIR, PAIR)
                gates = gat_ref[p]
                kt_all = kt_ref[p]
                gc = jnp.dot(gates, tri[d], precision=hi, preferred_element_type=F32)
                for hp in range(HEADS // 2):
                    k2 = k_ref[pl.ds(t0, PAIR), lanes[hp]]
                    q2 = q_ref[pl.ds(t0, PAIR), lanes[hp]]
                    v2 = v_ref[pl.ds(t0, PAIR), lanes[hp]]
                    kt2 = kt_all[hp * 2 * HEAD_D:(hp + 1) * 2 * HEAD_D, :]
                    bd_kt = _bd2(kt2[:HEAD_D], kt2[HEAD_D:])
                    streams.append(dict(kk=kk, d=d, hp=hp, gates=gates, gc=gc, k2=k2, q2=q2, v2=v2,
                                        kt2=kt2, kk_=_dot(k2, bd_kt), qk_=_dot(q2, bd_kt)))
        yield
        for st in streams:
            kk, d, hp, gates, gc = st["kk"], st["d"], st["hp"], st["gates"], st["gc"]
            gc_t = gc.T
            a, attn, qd, kdt, erow = [], [], [], [], []
            for hh in range(2):
                h = 2 * hp + hh
                hs = slice(hh * HEAD_D, (hh + 1) * HEAD_D)
                brow = gates[4 * d + h:4 * d + h + 1, :]
                gcr = gc[8 + 4 * d + h:9 + 4 * d + h, :]
                gcc = gc_t[:, 8 + 4 * d + h:9 + 4 * d + h]
                decay = jnp.exp((gcc - gcr) * incl[d]) * incl[d]
                a.append(st["kk_"][:, hs] * decay * strict[d] * brow)
                attn.append(st["qk_"][:, hs] * decay * (brow * scale))
                qd.append(st["q2"][:, hs].astype(F32) * (jnp.exp(gcc) * scale))
                if d == 0:
                    gl0, gl1 = gcr[:, CHUNK - 1:CHUNK], gcr[:, PAIR - 1:PAIR]
                else:
                    gl0, gl1 = gcr[:, 0:1], gcr[:, CHUNK:CHUNK + 1]
                glr = jnp.where(first_half, gl0, gl1)
                kdt.append(st["kt2"][hs, :].astype(F32) * (jnp.exp(glr - gcr) * brow))
                erow.append(jnp.exp(gcr))
                egl_s[slot, d, kk, 0, h:h + 1, :] = jnp.broadcast_to(jnp.exp(gl0), (1, PAIR))
                egl_s[slot, d, kk, 1, h:h + 1, :] = jnp.broadcast_to(jnp.exp(gl1), (1, PAIR))
            st["a"] = jnp.concatenate(a, axis=1)
            st["erow"] = jnp.concatenate(erow, axis=1)
            at_s[slot, d, kk, :, lanes[hp]] = jnp.concatenate(attn, axis=1).astype(BF16)
            kdt_s[slot, d, kk, hp] = jnp.concatenate(kdt, axis=1).astype(BF16)
            st["qd"] = jnp.concatenate(qd, axis=1).astype(BF16)
            st["t"] = eye2 - joining_blocks(st["a"], d, 0)
        yield
        for lvl in range(1, 6):
            for st in streams:
                st["y"] = _dot(joining_blocks(st["a"], st["d"], lvl).astype(BF16), _bd_of(st["t"]))
            yield
            for st in streams:
                st["t"] = st["t"] - _dot(st["t"].astype(BF16), _bd_of(st["y"]))
            yield
        for st in streams:
            t = st["t"]
            st["u"] = _dot(t.astype(BF16), _bd2(st["v2"][:, :HEAD_D], st["v2"][:, HEAD_D:]))
            st["w"] = _dot((t * st["erow"]).astype(BF16), _bd2(st["k2"][:, :HEAD_D], st["k2"][:, HEAD_D:]))
        yield
        for st in streams:
            kk, d, hp = st["kk"], st["d"], st["hp"]
            u_s[slot, d, kk, :, lanes[hp]] = st["u"].astype(BF16)
            wb, qdb = st["w"].astype(BF16), st["qd"]
            wq_s[slot, d, kk, :, lanes[hp]] = jnp.concatenate(
                [wb[:CHUNK], qdb[:CHUNK], wb[CHUNK:], qdb[CHUNK:]], axis=0)
        yield

    def scan(slot, step0):
        state = {(hp, d): s_s[d, :, lanes[hp]] for (hp, d) in stream_ids}
        for kk in range(per_iter):
            for half in range(2):
                e = {0: half, 1: 1 - half}
                res, bd_x = {}, {}
                for (hp, d) in stream_ids:
                    res[(hp, d)] = _dot(wq_s[slot, d, kk, e[d] * PAIR:(e[d] + 1) * PAIR, lanes[hp]],
                                        _bd_of(state[(hp, d)]))
                yield
                for (hp, d) in stream_ids:
                    vn = (u_s[slot, d, kk, e[d] * CHUNK:(e[d] + 1) * CHUNK, lanes[hp]].astype(F32)
                          - res[(hp, d)][:CHUNK])
                    vnb = vn.astype(BF16)
                    x = jnp.concatenate([vnb, zeros_half] if e[d] == 0 else [zeros_half, vnb], axis=0)
                    bd_x[(hp, d)] = _bd2(x[:, :HEAD_D], x[:, HEAD_D:])
                for (hp, d) in stream_ids:
                    both = _dot(jnp.concatenate(
                        [kdt_s[slot, d, kk, hp],
                         at_s[slot, d, kk, e[d] * CHUNK:(e[d] + 1) * CHUNK, lanes[hp]]], axis=0),
                        bd_x[(hp, d)])
                    eg = jnp.concatenate([egl_s[slot, d, kk, e[d], 2 * hp:2 * hp + 1, :],
                                          egl_s[slot, d, kk, e[d], 2 * hp + 1:2 * hp + 2, :]], axis=1)
                    state[(hp, d)] = state[(hp, d)] * eg + both[:HEAD_D]
                    r0 = pl.multiple_of(pair_of(step0 + kk, d) * PAIR + e[d] * CHUNK, CHUNK)
                    o_s[pl.ds(r0, CHUNK), lanes[hp]] += res[(hp, d)][CHUNK:] + both[HEAD_D:]
                yield
        for (hp, d) in stream_ids:
            s_s[d, :, lanes[hp]] = state[(hp, d)]

    s_s[...] = jnp.zeros(s_s.shape, F32)
    o_s[...] = jnp.zeros(o_s.shape, F32)
    n_iter = n_steps // per_iter
    _round_robin(prepare(0, 0))

    def body(it, carry):
        _round_robin(prepare(lax.rem(it + 1, 2), (it + 1) * per_iter), scan(lax.rem(it, 2), it * per_iter))
        return carry

    lax.fori_loop(0, n_iter - 1, body, 0)
    _round_robin(scan((n_iter - 1) % 2, (n_iter - 1) * per_iter))

    def epilogue(rb, carry):
        r0 = pl.multiple_of(rb * ROW_BLOCK, ROW_BLOCK)
        o = o_s[pl.ds(r0, ROW_BLOCK), :]
        segs = [_rms(o[:, h * HEAD_D:(h + 1) * HEAD_D]) * g_ref[...] for h in range(HEADS)]
        y_ref[pl.ds(r0, ROW_BLOCK), :] = jnp.concatenate(segs, axis=1).astype(BF16)
        return carry

    lax.fori_loop(0, l // ROW_BLOCK, epilogue, 0)


def _gdn(pg, kt, gat, g_gdn, l):
    b, lt, _ = pg.shape
    n_pairs = lt // PAIR

    def group(g):
        return pl.BlockSpec((None, lt, GROUP_W), lambda i, g=g: (i, 0, g))

    ring = (2, 2, 2)
    return pl.pallas_call(
        _gdn_kernel,
        grid=(b,),
        in_specs=[group(1), group(2), group(3),
                  pl.BlockSpec((None, n_pairs, GROUP_W, PAIR), lambda i: (i, 0, 0, 0)),
                  pl.BlockSpec((None, n_pairs, N_GATES, PAIR), lambda i: (i, 0, 0, 0)),
                  pl.BlockSpec((1, HEAD_D), lambda i: (0, 0))],
        out_specs=pl.BlockSpec((None, l, GROUP_W), lambda i: (i, 0, 0)),
        out_shape=jax.ShapeDtypeStruct((b, l, GROUP_W), BF16),
        scratch_shapes=[pltpu.VMEM(ring + (PAIR, GROUP_W), BF16),
                        pltpu.VMEM(ring + (2 * PAIR, GROUP_W), BF16),
                        pltpu.VMEM(ring + (PAIR, GROUP_W), BF16),
                        pltpu.VMEM(ring + (HEADS // 2, HEAD_D, 2 * PAIR), BF16),
                        pltpu.VMEM(ring + (2, 8, PAIR), F32),
                        pltpu.VMEM((2, HEAD_D, GROUP_W), F32),
                        pltpu.VMEM((lt, GROUP_W), F32)],
        compiler_params=_params(48, ("arbitrary",)),
        name="gdn",
    )(pg, pg, pg, kt, gat, g_gdn)


def _fourier_w_kernel(cc_ref, sc_ref, w_ref, ch_ref, sh_ref, o_ref, chb_ref, shb_ref):
    hi = lax.Precision.HIGHEST
    for g in range(w_ref.shape[0]):
        w = w_ref[g]
        o_ref[g] = jnp.concatenate(
            [jnp.dot(cc_ref[...], w, precision=hi, preferred_element_type=F32),
             -jnp.dot(sc_ref[...], w, precision=hi, preferred_element_type=F32)], axis=1)
    chb_ref[...] = ch_ref[...].astype(BF16)
    shb_ref[...] = sh_ref[...].astype(BF16)


def _fourier_w(cc, sc, w_f, ch, sh):
    groups, c, d = w_f.shape
    return pl.pallas_call(
        _fourier_w_kernel,
        out_shape=[jax.ShapeDtypeStruct((groups, c, 2 * d), F32),
                   jax.ShapeDtypeStruct(ch.shape, BF16),
                   jax.ShapeDtypeStruct(sh.shape, BF16)],
        compiler_params=pltpu.CompilerParams(vmem_limit_bytes=32 * MIB),
        name="fourier_w",
    )(cc, sc, w_f, ch, sh)


@functools.lru_cache(maxsize=None)
def _dft_constants(length, channels):
    def cos_sin(n, keep):
        idx = np.arange(keep, dtype=np.int64)
        ang = (np.outer(idx, idx) % n).astype(np.float64) * (2.0 * np.pi / n)
        return np.cos(ang), np.sin(ang)
    c_l, s_l = cos_sin(length, length // 2)
    c_c, s_c = cos_sin(channels, channels)
    norm = 1.0 / math.sqrt(length * channels)
    return (jnp.asarray(c_l.astype(np.float32)), jnp.asarray(s_l.astype(np.float32)),
            jnp.asarray((c_c * norm).astype(np.float32)),
            jnp.asarray((s_c * norm).astype(np.float32)))


def _mix_kernel(u_ref, on_ref, z_ref, ch_ref, sh_ref, mcs_ref, wout_ref, x_ref, gpost_ref, gt_ref,
                gpre_ref, shift_ref, scale_ref, x1_ref, h2_ref,
                yf_s, mc_s, ms_s, wout_s, ve_s, vo_s, d_s):
    i, m = pl.program_id(0), pl.program_id(1)
    l = u_ref.shape[0]
    h = l // 2
    blk = PAIR
    n_blk = h // blk

    @pl.when((i == 0) & (m == 0))
    def _():
        wout_s[...] = wout_ref[...].astype(BF16)
        mc_s[...] = jnp.zeros(mc_s.shape, BF16)
        ms_s[...] = jnp.zeros(ms_s.shape, BF16)
        for g in range(HEADS):
            gs = slice(g * HEAD_D, (g + 1) * HEAD_D)
            mc_s[gs, gs] = mcs_ref[g][:, :HEAD_D].astype(BF16)
            ms_s[gs, gs] = mcs_ref[g][:, HEAD_D:].astype(BF16)

    @pl.when(m == 0)
    def _():
        ri = lax.broadcasted_iota(jnp.int32, (blk, 2 * blk), 0)
        ci = lax.broadcasted_iota(jnp.int32, (blk, 2 * blk), 1)
        rev = jnp.where(((ci < blk) & (ri + ci == blk)) | ((ri == 0) & (ci == blk)), 1.0, 0.0).astype(BF16)
        zero_blk = jnp.zeros((blk, GROUP_W), BF16)

        def reversed_block(ref, base, k, last):
            top = ref[base + (n_blk - 1 - k) * blk:base + (n_blk - k) * blk, :]
            bottom = last if k == 0 else ref[base + (n_blk - k) * blk:base + (n_blk - k + 1) * blk, :]
            return _dot(rev, jnp.concatenate([top, bottom], axis=0))

        for k in range(n_blk):
            low = u_ref[k * blk:(k + 1) * blk, :].astype(F32)
            mirrored = reversed_block(u_ref, h, k, zero_blk)
            yf_s[k * blk:(k + 1) * blk, :] = (low + mirrored).astype(BF16)
            yf_s[h + k * blk:h + (k + 1) * blk, :] = (low - mirrored).astype(BF16)
        ve_s[...] = _dot(yf_s[:h, :], mc_s[...]).astype(BF16)
        vo_s[...] = _dot(yf_s[h:, :], ms_s[...]).astype(BF16)
        sign_row = jnp.where(lax.broadcasted_iota(jnp.int32, (16, l), 1) % 2 == 0, 1.0, -1.0)
        sign_row = jnp.where(lax.broadcasted_iota(jnp.int32, (16, l), 0) == 0, sign_row, 0.0).astype(BF16)
        alt_u = _dot(sign_row, u_ref[...])
        y_nyq = _dot(alt_u.astype(BF16), mc_s[...])[0:1, :]
        v_nyq = _dot(u_ref[h:h + 16, :], mc_s[...])[0:1, :]
        sign_col = jnp.where(lax.broadcasted_iota(jnp.int32, (blk, GROUP_W), 0) % 2 == 0, 1.0, -1.0)
        first_row = lax.broadcasted_iota(jnp.int32, (blk, GROUP_W), 0) == 0
        d_s[h:h + blk, :] = jnp.where(first_row, y_nyq, 0.0).astype(BF16)
        nyq = sign_col * v_nyq
        tall = min(FFN_ROWS, h)
        nyq = jnp.concatenate([nyq] * (tall // blk), axis=0)
        for k in range(h // tall):
            rows = slice(k * tall, (k + 1) * tall)
            a = _dot(ch_ref[rows, :], ve_s[...]) + nyq
            b = _dot(sh_ref[rows, :], vo_s[...])
            yf_s[rows, :] = (a + b).astype(BF16)
            d_s[rows, :] = (a - b).astype(BF16)
        for k in range(n_blk):
            yf_s[h + k * blk:h + (k + 1) * blk, :] = reversed_block(
                d_s, 0, k, d_s[h:h + blk, :]).astype(BF16)

    tm = x_ref.shape[0]
    half = wout_s.shape[0] // 2
    sub = min(FFN_ROWS, tm)
    subs = [slice(sb * sub, (sb + 1) * sub) for sb in range(tm // sub)]
    ys = []
    for sb, rs in enumerate(subs):
        r0 = pl.multiple_of(m * tm + sb * sub, sub)
        yd = (on_ref[rs, :].astype(F32) * z_ref[rs, :].astype(F32)).astype(BF16)
        ys.append(_dot(yf_s[pl.ds(r0, sub), :], wout_s[:half, :]) + _dot(yd, wout_s[half:, :]))
    for rs, y in zip(subs, ys):
        x1 = x_ref[rs, :] + gt_ref[...] * (_rms(y) * gpost_ref[...])
        x1_ref[rs, :] = x1
        h2 = _rms(x1) * gpre_ref[...]
        h2_ref[rs, :] = (h2 * (1.0 + scale_ref[...]) + shift_ref[...]).astype(BF16)


def _mix(pg, on, ch, sh, mcs, w_out, x, g_post, gt, g_pre, shift, scale):
    b, l, d = x.shape
    tm = MIX_TILE
    h = l // 2
    vec = pl.BlockSpec((1, d), lambda i, m: (0, 0))
    mod = pl.BlockSpec((None, 1, d), lambda i, m: (i, 0, 0))
    whole = lambda a: pl.BlockSpec(a.shape, lambda i, m: (0,) * a.ndim)
    return pl.pallas_call(
        _mix_kernel,
        grid=(b, l // tm),
        in_specs=[pl.BlockSpec((None, l, GROUP_W), lambda i, m: (i, 0, 0)),
                  pl.BlockSpec((None, tm, GROUP_W), lambda i, m: (i, m, 0)),
                  pl.BlockSpec((None, tm, GROUP_W), lambda i, m: (i, m, N_GROUPS - 1)),
                  whole(ch), whole(sh), whole(mcs), whole(w_out),
                  pl.BlockSpec((None, tm, d), lambda i, m: (i, m, 0)),
                  vec, mod, vec, mod, mod],
        out_specs=[pl.BlockSpec((None, tm, d), lambda i, m: (i, m, 0)),
                   pl.BlockSpec((None, tm, d), lambda i, m: (i, m, 0))],
        out_shape=[jax.ShapeDtypeStruct((b, l, d), F32),
                   jax.ShapeDtypeStruct((b, l, d), BF16)],
        scratch_shapes=[pltpu.VMEM((l, GROUP_W), BF16),
                        pltpu.VMEM((GROUP_W, GROUP_W), BF16),
                        pltpu.VMEM((GROUP_W, GROUP_W), BF16),
                        pltpu.VMEM(w_out.shape, BF16),
                        pltpu.VMEM((h, GROUP_W), BF16),
                        pltpu.VMEM((h, GROUP_W), BF16),
                        pltpu.VMEM((h + PAIR, GROUP_W), BF16)],
        compiler_params=_params(48, ("arbitrary", "arbitrary")),
        name="mix",
    )(pg, on, pg, ch, sh, mcs, w_out, x, g_post, gt, g_pre, shift, scale)


def _ffn_kernel(h_ref, x1_ref, wv_ref, wg_ref, dwc_ref, wd_ref, gpost_ref, gt_ref,
                o_ref, gate_s, val_s, wvg_s, wd_s):
    ROW_BLOCK = FFN_ROWS
    j = pl.program_id(1)
    l = h_ref.shape[0]
    tf = wv_ref.shape[1]
    n_rb = l // ROW_BLOCK
    pad = GRID_W

    wvg_s[:, :tf] = wv_ref[...].astype(BF16)
    wvg_s[:, tf:] = wg_ref[...].astype(BF16)
    wd_s[...] = wd_ref[...].astype(BF16)
    gate_s[0:pad, :] = jnp.zeros((pad, tf), F32)
    gate_s[l + pad:l + 2 * pad, :] = jnp.zeros((pad, tf), F32)

    @pl.when(j == 0)
    def _():
        o_ref[...] = jnp.zeros(o_ref.shape, F32)

    n_ext = ROW_BLOCK + 2 * pad
    col = lax.broadcasted_iota(jnp.int32, (n_ext, tf), 0) % GRID_W
    not_first = jnp.where(col == 0, 0.0, 1.0)
    not_last = jnp.where(col == GRID_W - 1, 0.0, 1.0)
    dw = dwc_ref[...]

    def up(i):
        r0 = i * ROW_BLOCK
        vg = _dot(h_ref[r0:r0 + ROW_BLOCK, :], wvg_s[...])
        val_s[r0:r0 + ROW_BLOCK, :] = vg[:, :tf]
        gate_s[r0 + pad:r0 + pad + ROW_BLOCK, :] = vg[:, tf:]

    def down(i):
        r0 = i * ROW_BLOCK
        ext = gate_s[r0:r0 + n_ext, :]
        shifted = (pltpu.roll(ext, 1, 0) * not_first, ext, pltpu.roll(ext, n_ext - 1, 0) * not_last)
        conv = None
        for dy in range(3):
            for dx in range(3):
                term = shifted[dx][dy * pad:dy * pad + ROW_BLOCK] * dw[3 * dy + dx:3 * dy + dx + 1]
                conv = term if conv is None else conv + term
        act = (_silu(conv) * val_s[r0:r0 + ROW_BLOCK, :]).astype(BF16)
        o_ref[r0:r0 + ROW_BLOCK, :] += _dot(act, wd_s[...])

    for i in range(min(2, n_rb)):
        up(i)
    for i in range(n_rb):
        if i + 2 < n_rb:
            up(i + 2)
        down(i)

    @pl.when(j == pl.num_programs(1) - 1)
    def _():
        def fin(rb, carry):
            r0 = pl.multiple_of(rb * ROW_BLOCK, ROW_BLOCK)
            y = _rms(o_ref[pl.ds(r0, ROW_BLOCK), :]) * gpost_ref[...]
            o_ref[pl.ds(r0, ROW_BLOCK), :] = x1_ref[pl.ds(r0, ROW_BLOCK), :] + gt_ref[...] * y
            return carry
        lax.fori_loop(0, n_rb, fin, 0)


def _ffn(h2, x1, w_up, dwc, w_down, g_post, gt):
    b, l, d = x1.shape
    d_ff = w_down.shape[0]
    tf = FFN_TILE
    n_f = d_ff // tf
    return pl.pallas_call(
        _ffn_kernel,
        grid=(b, n_f),
        in_specs=[pl.BlockSpec((None, l, d), lambda i, j: (i, 0, 0)),
                  pl.BlockSpec((None, l, d), lambda i, j: (i, 0, 0)),
                  pl.BlockSpec((d, tf), lambda i, j: (0, j)),
                  pl.BlockSpec((d, tf), lambda i, j: (0, n_f + j)),
                  pl.BlockSpec((9, tf), lambda i, j: (0, j)),
                  pl.BlockSpec((tf, d), lambda i, j: (j, 0)),
                  pl.BlockSpec((1, d), lambda i, j: (0, 0)),
                  pl.BlockSpec((None, 1, d), lambda i, j: (i, 0, 0))],
        out_specs=pl.BlockSpec((None, l, d), lambda i, j: (i, 0, 0)),
        out_shape=jax.ShapeDtypeStruct((b, l, d), F32),
        scratch_shapes=[pltpu.VMEM((l + 2 * GRID_W, tf), F32),
                        pltpu.VMEM((l, tf), F32),
                        pltpu.VMEM((d, 2 * tf), BF16),
                        pltpu.VMEM((tf, d), BF16)],
        compiler_params=_params(58, ("arbitrary", "arbitrary")),
        name="ffn",
    )(h2, x1, w_up, w_up, dwc, w_down, g_post, gt)


def kernel(x, c, ctx, c_ctx, w_ada, b_ada, g_pre_mix, g_post_mix, g_pre_ffn, g_post_ffn,
           w_in, w_qkv_conv, a_log, dt_bias, g_gdn, w_fourier, w_out, w_up, w_dwc, w_down):
    b, l, d = x.shape
    assert w_ada.shape[0] == 1, "single-layer stack"
    assert b < 16 and l % ROW_BLOCK == 0 and ctx.shape[1] % ROW_BLOCK == 0
    assert w_in.shape[2] == N_GROUPS * GROUP_W + N_GATES and d == 2 * GROUP_W

    cc = jnp.zeros((16, d), F32).at[:b].set(c).at[b].set(c_ctx)
    mod = _ada(cc, w_ada[0], b_ada)
    sh1, sc1, gt1, sh2, sc2, gt2 = [m[:, None, :] for m in jnp.split(mod[:b], 6, axis=-1)]
    sh1c, sc1c = [m[None, :] for m in jnp.split(mod[b], 6, axis=-1)[:2]]

    wab = jnp.pad(w_in[0][:, N_GROUPS * GROUP_W:], ((0, 0), (0, PAIR - N_GATES)))
    zeros8 = jnp.zeros((8,), F32)
    alog = jnp.concatenate([zeros8, a_log[0].reshape(-1)])[:, None]
    dtb = jnp.concatenate([zeros8, dt_bias[0].reshape(-1)])[:, None]

    pg, kt, gat = _inproj(x, ctx, g_pre_mix, sh1, sc1, sh1c, sc1c, w_in[0], wab, w_qkv_conv[0], alog, dtb)
    on = _gdn(pg, kt, gat, g_gdn, l)

    ch, sh, c_c, s_c = _dft_constants(l, HEAD_D)
    mcs, ch, sh = _fourier_w(c_c, s_c, w_fourier[0], ch, sh)
    x1, h2 = _mix(pg, on, ch, sh, mcs, w_out[0], x, g_post_mix, gt1, g_pre_ffn, sh2, sc2)
    d_ff = w_down.shape[1]
    return _ffn(h2, x1, w_up[0], w_dwc[0].reshape(9, d_ff), w_down[0], g_post_ffn, gt2)
```

```python
import functools
import itertools
import math

import numpy as np
import jax
import jax.numpy as jnp
from jax import lax
from jax.experimental import pallas as pl
from jax.experimental.pallas import tpu as pltpu

F32 = jnp.float32
BF16 = jnp.bfloat16

GRID_W = 64
HEADS = 4
HEAD_D = 128
CHUNK = 64
PAIR = 2 * CHUNK
GROUP_W = HEADS * HEAD_D
N_GROUPS = 5
N_GATES = 16
EPS = 1e-6
ROW_BLOCK = 256
FFN_TILE = 256
MIX_TILE = 1024
FFN_ROWS = 512
MIB = 1024 * 1024

_NT = (((1,), (1,)), ((), ()))


def _dot(a, b):
    return jnp.dot(a, b, preferred_element_type=F32)


def _silu(x):
    return x / (1.0 + jnp.exp(-x))


def _rms(x):
    return x * lax.rsqrt(jnp.mean(x * x, axis=-1, keepdims=True) + EPS)


def _params(vmem_mib, semantics):
    return pltpu.CompilerParams(dimension_semantics=semantics,
                                vmem_limit_bytes=vmem_mib * MIB)


def _ada_kernel(c_ref, w_ref, b_ref, o_ref):
    s = _silu(c_ref[...]).astype(BF16)
    o_ref[...] = _dot(s, w_ref[...].astype(BF16)) + b_ref[...]


def _ada(cc, w, b):
    rows, d = cc.shape
    n = w.shape[1]
    tn = 1024
    return pl.pallas_call(
        _ada_kernel,
        grid=(n // tn,),
        in_specs=[pl.BlockSpec((rows, d), lambda j: (0, 0)),
                  pl.BlockSpec((d, tn), lambda j: (0, j)),
                  pl.BlockSpec((1, tn), lambda j: (0, j))],
        out_specs=pl.BlockSpec((rows, tn), lambda j: (0, j)),
        out_shape=jax.ShapeDtypeStruct((rows, n), F32),
        compiler_params=_params(32, ("arbitrary",)),
        name="ada",
    )(cc, w, b)


def _inproj_kernel(x_ref, ctx_ref, g_ref, sh_ref, sc_ref, shc_ref, scc_ref, w_ref, wab_ref,
                   conv_ref, alog_ref, dtb_ref, pg_ref, kt_ref, gat_ref, h_s, w_s, p_s):
    j = pl.program_id(1)
    l, lc = x_ref.shape[0], ctx_ref.shape[0]
    blocks = [(x_ref, r, r, r + 8, sh_ref, sc_ref, min(FFN_ROWS, l - r)) for r in range(0, l, FFN_ROWS)]
    blocks += [(ctx_ref, r, l + r, l + r + 16, shc_ref, scc_ref, min(FFN_ROWS, lc - r))
               for r in range(0, lc, FFN_ROWS)]
    n_rb = len(blocks)

    w_s[...] = w_ref[...].astype(BF16)

    def out_rows(rb):
        return slice(blocks[rb][2], blocks[rb][2] + blocks[rb][6])

    def project(rb):
        return _dot(h_s[out_rows(rb), :], w_s[...])

    def pipelined(first, second, lead):
        for rb in range(min(lead, n_rb)):
            first(rb)
        for rb in range(n_rb):
            if rb + lead < n_rb:
                first(rb + lead)
            second(rb)

    @pl.when(j == 0)
    def _u():
        for r in (0, l + 8, l + lc + 16):
            p_s[r:r + 8, :] = jnp.zeros((8, GROUP_W), F32)
        wabt = wab_ref[...].T[:N_GATES].astype(BF16)
        row = lax.broadcasted_iota(jnp.int32, (N_GATES, PAIR), 0)

        def prologue(rb):
            src, r, dst, _, shift, scale, n = blocks[rb]
            h = _rms(src[r:r + n, :]) * g_ref[...]
            hb = (h * (1.0 + scale[...]) + shift[...]).astype(BF16)
            h_s[dst:dst + n, :] = hb
            for s in range(n // PAIR):
                ab = lax.dot_general(wabt, hb[s * PAIR:(s + 1) * PAIR], _NT,
                                     preferred_element_type=F32)
                beta = 1.0 / (1.0 + jnp.exp(-ab))
                xs = ab + dtb_ref[...]
                softplus = jnp.maximum(xs, 0.0) + jnp.log1p(jnp.exp(-jnp.abs(xs)))
                gate = -jnp.exp(alog_ref[...]) * softplus
                gat_ref[dst // PAIR + s] = jnp.where(row < 8, beta, gate)

        def store_u(rb):
            pg_ref[out_rows(rb), :] = project(rb).astype(BF16)

        pipelined(prologue, store_u, 2)

    @pl.when(j == N_GROUPS - 1)
    def _z():
        for rb in range(n_rb):
            pg_ref[out_rows(rb), :] = _silu(project(rb)).astype(BF16)

    def conv_group(normalise, transpose):
        cw = conv_ref[...]

        def raw(rb):
            p0, n = blocks[rb][3], blocks[rb][6]
            p_s[p0:p0 + n, :] = project(rb)

        def finish(rb):
            p0, n = blocks[rb][3], blocks[rb][6]
            ext = p_s[p0 - 8:p0 + n + 8, :]
            prev = pltpu.roll(ext, 1, 0)[8:8 + n]
            nxt = pltpu.roll(ext, n + 15, 0)[8:8 + n]
            cur = ext[8:8 + n]
            a = _silu(prev * cw[0:1] + cur * cw[1:2] + nxt * cw[2:3])
            if normalise:
                segs = []
                for hh in range(HEADS):
                    seg = a[:, hh * HEAD_D:(hh + 1) * HEAD_D]
                    segs.append(seg * lax.rsqrt(jnp.sum(seg * seg, axis=-1, keepdims=True) + EPS))
                a = jnp.concatenate(segs, axis=1)
            pg_ref[out_rows(rb), :] = a.astype(BF16)
            if transpose:
                at = a.T.astype(BF16)
                for s in range(n // PAIR):
                    kt_ref[blocks[rb][2] // PAIR + s] = at[:, s * PAIR:(s + 1) * PAIR]

        pipelined(raw, finish, 2)

    @pl.when(j == 1)
    def _q():
        conv_group(True, False)

    @pl.when(j == 2)
    def _k():
        conv_group(True, True)

    @pl.when(j == 3)
    def _v():
        conv_group(False, False)


def _inproj(x, ctx, g, shift, scale, shift_c, scale_c, w_in, wab, conv, alog, dtb):
    b, l, d = x.shape
    lt = l + ctx.shape[1]
    n_pairs = lt // PAIR
    vec = pl.BlockSpec((1, d), lambda i, j: (0, 0))
    mod = pl.BlockSpec((None, 1, d), lambda i, j: (i, 0, 0))
    return pl.pallas_call(
        _inproj_kernel,
        grid=(b, N_GROUPS),
        in_specs=[pl.BlockSpec((None, l, d), lambda i, j: (i, 0, 0)),
                  pl.BlockSpec((None, lt - l, d), lambda i, j: (i, 0, 0)),
                  vec, mod, mod, vec, vec,
                  pl.BlockSpec((d, GROUP_W), lambda i, j: (0, j)),
                  pl.BlockSpec((d, PAIR), lambda i, j: (0, 0)),
                  pl.BlockSpec((3, GROUP_W), lambda i, j: (0, jnp.clip(j - 1, 0, 2))),
                  pl.BlockSpec((N_GATES, 1), lambda i, j: (0, 0)),
                  pl.BlockSpec((N_GATES, 1), lambda i, j: (0, 0))],
        out_specs=[pl.BlockSpec((None, lt, GROUP_W), lambda i, j: (i, 0, j)),
                   pl.BlockSpec((None, n_pairs, GROUP_W, PAIR), lambda i, j: (i, 0, 0, 0)),
                   pl.BlockSpec((None, n_pairs, N_GATES, PAIR), lambda i, j: (i, 0, 0, 0))],
        out_shape=[jax.ShapeDtypeStruct((b, lt, N_GROUPS * GROUP_W), BF16),
                   jax.ShapeDtypeStruct((b, n_pairs, GROUP_W, PAIR), BF16),
                   jax.ShapeDtypeStruct((b, n_pairs, N_GATES, PAIR), F32)],
        scratch_shapes=[pltpu.VMEM((lt, d), BF16),
                        pltpu.VMEM((d, GROUP_W), BF16),
                        pltpu.VMEM((lt + 24, GROUP_W), F32)],
        compiler_params=_params(52, ("arbitrary", "arbitrary")),
        name="inproj",
    )(x, ctx, g, shift, scale, shift_c, scale_c, w_in, wab, conv, alog, dtb)


def _bd2(a, b):
    z = jnp.zeros_like(a)
    return jnp.concatenate([jnp.concatenate([a, z], axis=1),
                            jnp.concatenate([z, b], axis=1)], axis=0)


def _bd_of(x):
    xb = x.astype(BF16)
    return _bd2(xb[:, :HEAD_D], xb[:, HEAD_D:])


def _round_robin(*stage_generators):
    for _ in itertools.zip_longest(*stage_generators):
        pass


def _gdn_kernel(q_ref, k_ref, v_ref, kt_ref, gat_ref, g_ref, y_ref,
                u_s, wq_s, at_s, kdt_s, egl_s, s_s, o_s):
    lt, l = q_ref.shape[0], y_ref.shape[0]
    n_steps = lt // PAIR
    n_ctx = (lt - l) // PAIR
    per_iter = 2
    assert n_steps % per_iter == 0

    ri = lax.broadcasted_iota(jnp.int32, (PAIR, PAIR), 0)
    ci = lax.broadcasted_iota(jnp.int32, (PAIR, PAIR), 1)
    same = (ri // CHUNK) == (ci // CHUNK)
    incl = (jnp.where(same & (ri >= ci), 1.0, 0.0), jnp.where(same & (ri <= ci), 1.0, 0.0))
    strict = (jnp.where(same & (ri > ci), 1.0, 0.0), jnp.where(same & (ri < ci), 1.0, 0.0))
    tri = (incl[1], incl[0])
    eye = jnp.where(ri == ci, 1.0, 0.0)
    eye2 = jnp.concatenate([eye, eye], axis=1)
    off = ([], [])
    for lvl in range(6):
        b = 2 ** lvl
        joined = (ri // (2 * b)) == (ci // (2 * b))
        off[0].append(jnp.where(joined & (ri % (2 * b) >= b) & (ci % (2 * b) < b), 1.0, 0.0))
        off[1].append(jnp.where(joined & (ri % (2 * b) < b) & (ci % (2 * b) >= b), 1.0, 0.0))

    def joining_blocks(a, d, lvl):
        m = off[d][lvl]
        return jnp.concatenate([a[:, :HEAD_D] * m, a[:, HEAD_D:] * m], axis=1)

    lane = lax.broadcasted_iota(jnp.int32, (1, PAIR), 1)
    first_half = lane < CHUNK
    scale = HEAD_D ** -0.5
    hi = lax.Precision.HIGHEST
    lanes = [slice(hp * 2 * HEAD_D, (hp + 1) * 2 * HEAD_D) for hp in range(HEADS // 2)]
    stream_ids = [(hp, d) for hp in range(HEADS // 2) for d in range(2)]
    zeros_half = jnp.zeros((CHUNK, 2 * HEAD_D), BF16)

    def pair_of(step, d):
        if d == 0:
            return lax.rem(step + (n_steps - n_ctx), n_steps)
        return n_steps - 1 - step

    def prepare(slot, step0):
        streams = []
        for kk in range(per_iter):
            for d in range(2):
                p = pair_of(step0 + kk, d)
                t0 = pl.multiple_of(p * PAIR, PAIR)
                gates = gat_ref[p]
                kt_all = kt_ref[p]
                gc = jnp.dot(gates, tri[d], precision=hi, preferred_element_type=F32)
                for hp in range(HEADS // 2):
                    k2 = k_ref[pl.ds(t0, PAIR), lanes[hp]]
                    q2 = q_ref[pl.ds(t0, PAIR), lanes[hp]]
                    v2 = v_ref[pl.ds(t0, PAIR), lanes[hp]]
                    kt2 = kt_all[hp * 2 * HEAD_D:(hp + 1) * 2 * HEAD_D, :]
                    bd_kt = _bd2(kt2[:HEAD_D], kt2[HEAD_D:])
                    streams.append(dict(kk=kk, d=d, hp=hp, gates=gates, gc=gc, k2=k2, q2=q2, v2=v2,
                                        kt2=kt2, kk_=_dot(k2, bd_kt), qk_=_dot(q2, bd_kt)))
        yield
        for st in streams:
            kk, d, hp, gates, gc = st["kk"], st["d"], st["hp"], st["gates"], st["gc"]
            gc_t = gc.T
            a, attn, qd, kdt, erow = [], [], [], [], []
            for hh in range(2):
                h = 2 * hp + hh
                hs = slice(hh * HEAD_D, (hh + 1) * HEAD_D)
                brow = gates[4 * d + h:4 * d + h + 1, :]
                gcr = gc[8 + 4 * d + h:9 + 4 * d + h, :]
                gcc = gc_t[:, 8 + 4 * d + h:9 + 4 * d + h]
                decay = jnp.exp((gcc - gcr) * incl[d]) * incl[d]
                a.append(st["kk_"][:, hs] * decay * strict[d] * brow)
                attn.append(st["qk_"][:, hs] * decay * (brow * scale))
                qd.append(st["q2"][:, hs].astype(F32) * (jnp.exp(gcc) * scale))
                if d == 0:
                    gl0, gl1 = gcr[:, CHUNK - 1:CHUNK], gcr[:, PAIR - 1:PAIR]
                else:
                    gl0, gl1 = gcr[:, 0:1], gcr[:, CHUNK:CHUNK + 1]
                glr = jnp.where(first_half, gl0, gl1)
                kdt.append(st["kt2"][hs, :].astype(F32) * (jnp.exp(glr - gcr) * brow))
                erow.append(jnp.exp(gcr))
                egl_s[slot, d, kk, 0, h:h + 1, :] = jnp.broadcast_to(jnp.exp(gl0), (1, PAIR))
                egl_s[slot, d, kk, 1, h:h + 1, :] = jnp.broadcast_to(jnp.exp(gl1), (1, PAIR))
            st["a"] = jnp.concatenate(a, axis=1)
            st["erow"] = jnp.concatenate(erow, axis=1)
            at_s[slot, d, kk, :, lanes[hp]] = jnp.concatenate(attn, axis=1).astype(BF16)
            kdt_s[slot, d, kk, hp] = jnp.concatenate(kdt, axis=1).astype(BF16)
            st["qd"] = jnp.concatenate(qd, axis=1).astype(BF16)
            st["t"] = eye2 - joining_blocks(st["a"], d, 0)
        yield
        for lvl in range(1, 6):
            for st in streams:
                st["y"] = _dot(joining_blocks(st["a"], st["d"], lvl).astype(BF16), _bd_of(st["t"]))
            yield
            for st in streams:
                st["t"] = st["t"] - _dot(st["t"].astype(BF16), _bd_of(st["y"]))
            yield
        for st in streams:
            t = st["t"]
            st["u"] = _dot(t.astype(BF16), _bd2(st["v2"][:, :HEAD_D], st["v2"][:, HEAD_D:]))
            st["w"] = _dot((t * st["erow"]).astype(BF16), _bd2(st["k2"][:, :HEAD_D], st["k2"][:, HEAD_D:]))
        yield
        for st in streams:
            kk, d, hp = st["kk"], st["d"], st["hp"]
            u_s[slot, d, kk, :, lanes[hp]] = st["u"].astype(BF16)
            wb, qdb = st["w"].astype(BF16), st["qd"]
            wq_s[slot, d, kk, :, lanes[hp]] = jnp.concatenate(
                [wb[:CHUNK], qdb[:CHUNK], wb[CHUNK:], qdb[CHUNK:]], axis=0)
        yield

    def scan(slot, step0):
        state = {(hp, d): s_s[d, :, lanes[hp]] for (hp, d) in stream_ids}
        for kk in range(per_iter):
            for half in range(2):
                e = {0: half, 1: 1 - half}
                res, bd_x = {}, {}
                for (hp, d) in stream_ids:
                    res[(hp, d)] = _dot(wq_s[slot, d, kk, e[d] * PAIR:(e[d] + 1) * PAIR, lanes[hp]],
                                        _bd_of(state[(hp, d)]))
                yield
                for (hp, d) in stream_ids:
                    vn = (u_s[slot, d, kk, e[d] * CHUNK:(e[d] + 1) * CHUNK, lanes[hp]].astype(F32)
                          - res[(hp, d)][:CHUNK])
                    vnb = vn.astype(BF16)
                    x = jnp.concatenate([vnb, zeros_half] if e[d] == 0 else [zeros_half, vnb], axis=0)
                    bd_x[(hp, d)] = _bd2(x[:, :HEAD_D], x[:, HEAD_D:])
                for (hp, d) in stream_ids:
                    both = _dot(jnp.concatenate(
                        [kdt_s[slot, d, kk, hp],
                         at_s[slot, d, kk, e[d] * CHUNK:(e[d] + 1) * CHUNK, lanes[hp]]], axis=0),
                        bd_x[(hp, d)])
                    eg = jnp.concatenate([egl_s[slot, d, kk, e[d], 2 * hp:2 * hp + 1, :],
                                          egl_s[slot, d, kk, e[d], 2 * hp + 1:2 * hp + 2, :]], axis=1)
                    state[(hp, d)] = state[(hp, d)] * eg + both[:HEAD_D]
                    r0 = pl.multiple_of(pair_of(step0 + kk, d) * PAIR + e[d] * CHUNK, CHUNK)
                    o_s[pl.ds(r0, CHUNK), lanes[hp]] += res[(hp, d)][CHUNK:] + both[HEAD_D:]
                yield
        for (hp, d) in stream_ids:
            s_s[d, :, lanes[hp]] = state[(hp, d)]

    s_s[...] = jnp.zeros(s_s.shape, F32)
    o_s[...] = jnp.zeros(o_s.shape, F32)
    n_iter = n_steps // per_iter
    _round_robin(prepare(0, 0))

    def body(it, carry):
        _round_robin(prepare(lax.rem(it + 1, 2), (it + 1) * per_iter), scan(lax.rem(it, 2), it * per_iter))
        return carry

    lax.fori_loop(0, n_iter - 1, body, 0)
    _round_robin(scan((n_iter - 1) % 2, (n_iter - 1) * per_iter))

    def epilogue(rb, carry):
        r0 = pl.multiple_of(rb * ROW_BLOCK, ROW_BLOCK)
        o = o_s[pl.ds(r0, ROW_BLOCK), :]
        segs = [_rms(o[:, h * HEAD_D:(h + 1) * HEAD_D]) * g_ref[...] for h in range(HEADS)]
        y_ref[pl.ds(r0, ROW_BLOCK), :] = jnp.concatenate(segs, axis=1).astype(BF16)
        return carry

    lax.fori_loop(0, l // ROW_BLOCK, epilogue, 0)


def _gdn(pg, kt, gat, g_gdn, l):
    b, lt, _ = pg.shape
    n_pairs = lt // PAIR

    def group(g):
        return pl.BlockSpec((None, lt, GROUP_W), lambda i, g=g: (i, 0, g))

    ring = (2, 2, 2)
    return pl.pallas_call(
        _gdn_kernel,
        grid=(b,),
        in_specs=[group(1), group(2), group(3),
                  pl.BlockSpec((None, n_pairs, GROUP_W, PAIR), lambda i: (i, 0, 0, 0)),
                  pl.BlockSpec((None, n_pairs, N_GATES, PAIR), lambda i: (i, 0, 0, 0)),
                  pl.BlockSpec((1, HEAD_D), lambda i: (0, 0))],
        out_specs=pl.BlockSpec((None, l, GROUP_W), lambda i: (i, 0, 0)),
        out_shape=jax.ShapeDtypeStruct((b, l, GROUP_W), BF16),
        scratch_shapes=[pltpu.VMEM(ring + (PAIR, GROUP_W), BF16),
                        pltpu.VMEM(ring + (2 * PAIR, GROUP_W), BF16),
                        pltpu.VMEM(ring + (PAIR, GROUP_W), BF16),
                        pltpu.VMEM(ring + (HEADS // 2, HEAD_D, 2 * PAIR), BF16),
                        pltpu.VMEM(ring + (2, 8, PAIR), F32),
                        pltpu.VMEM((2, HEAD_D, GROUP_W), F32),
                        pltpu.VMEM((lt, GROUP_W), F32)],
        compiler_params=_params(48, ("arbitrary",)),
        name="gdn",
    )(pg, pg, pg, kt, gat, g_gdn)


def _fourier_w_kernel(cc_ref, sc_ref, w_ref, ch_ref, sh_ref, o_ref, chb_ref, shb_ref):
    hi = lax.Precision.HIGHEST
    for g in range(w_ref.shape[0]):
        w = w_ref[g]
        o_ref[g] = jnp.concatenate(
            [jnp.dot(cc_ref[...], w, precision=hi, preferred_element_type=F32),
             -jnp.dot(sc_ref[...], w, precision=hi, preferred_element_type=F32)], axis=1)
    chb_ref[...] = ch_ref[...].astype(BF16)
    shb_ref[...] = sh_ref[...].astype(BF16)


def _fourier_w(cc, sc, w_f, ch, sh):
    groups, c, d = w_f.shape
    return pl.pallas_call(
        _fourier_w_kernel,
        out_shape=[jax.ShapeDtypeStruct((groups, c, 2 * d), F32),
                   jax.ShapeDtypeStruct(ch.shape, BF16),
                   jax.ShapeDtypeStruct(sh.shape, BF16)],
        compiler_params=pltpu.CompilerParams(vmem_limit_bytes=32 * MIB),
        name="fourier_w",
    )(cc, sc, w_f, ch, sh)


@functools.lru_cache(maxsize=None)
def _dft_constants(length, channels):
    def cos_sin(n, keep):
        idx = np.arange(keep, dtype=np.int64)
        ang = (np.outer(idx, idx) % n).astype(np.float64) * (2.0 * np.pi / n)
        return np.cos(ang), np.sin(ang)
    c_l, s_l = cos_sin(length, length // 2)
    c_c, s_c = cos_sin(channels, channels)
    norm = 1.0 / math.sqrt(length * channels)
    return (jnp.asarray(c_l.astype(np.float32)), jnp.asarray(s_l.astype(np.float32)),
            jnp.asarray((c_c * norm).astype(np.float32)),
            jnp.asarray((s_c * norm).astype(np.float32)))


def _mix_kernel(u_ref, on_ref, z_ref, ch_ref, sh_ref, mcs_ref, wout_ref, x_ref, gpost_ref, gt_ref,
                gpre_ref, shift_ref, scale_ref, x1_ref, h2_ref,
                yf_s, mc_s, ms_s, wout_s, ve_s, vo_s, d_s):
    i, m = pl.program_id(0), pl.program_id(1)
    l = u_ref.shape[0]
    h = l // 2
    blk = PAIR
    n_blk = h // blk

    @pl.when((i == 0) & (m == 0))
    def _():
        wout_s[...] = wout_ref[...].astype(BF16)
        mc_s[...] = jnp.zeros(mc_s.shape, BF16)
        ms_s[...] = jnp.zeros(ms_s.shape, BF16)
        for g in range(HEADS):
            gs = slice(g * HEAD_D, (g + 1) * HEAD_D)
            mc_s[gs, gs] = mcs_ref[g][:, :HEAD_D].astype(BF16)
            ms_s[gs, gs] = mcs_ref[g][:, HEAD_D:].astype(BF16)

    @pl.when(m == 0)
    def _():
        ri = lax.broadcasted_iota(jnp.int32, (blk, 2 * blk), 0)
        ci = lax.broadcasted_iota(jnp.int32, (blk, 2 * blk), 1)
        rev = jnp.where(((ci < blk) & (ri + ci == blk)) | ((ri == 0) & (ci == blk)), 1.0, 0.0).astype(BF16)
        zero_blk = jnp.zeros((blk, GROUP_W), BF16)

        def reversed_block(ref, base, k, last):
            top = ref[base + (n_blk - 1 - k) * blk:base + (n_blk - k) * blk, :]
            bottom = last if k == 0 else ref[base + (n_blk - k) * blk:base + (n_blk - k + 1) * blk, :]
            return _dot(rev, jnp.concatenate([top, bottom], axis=0))

        for k in range(n_blk):
            low = u_ref[k * blk:(k + 1) * blk, :].astype(F32)
            mirrored = reversed_block(u_ref, h, k, zero_blk)
            yf_s[k * blk:(k + 1) * blk, :] = (low + mirrored).astype(BF16)
            yf_s[h + k * blk:h + (k + 1) * blk, :] = (low - mirrored).astype(BF16)
        ve_s[...] = _dot(yf_s[:h, :], mc_s[...]).astype(BF16)
        vo_s[...] = _dot(yf_s[h:, :], ms_s[...]).astype(BF16)
        sign_row = jnp.where(lax.broadcasted_iota(jnp.int32, (16, l), 1) % 2 == 0, 1.0, -1.0)
        sign_row = jnp.where(lax.broadcasted_iota(jnp.int32, (16, l), 0) == 0, sign_row, 0.0).astype(BF16)
        alt_u = _dot(sign_row, u_ref[...])
        y_nyq = _dot(alt_u.astype(BF16), mc_s[...])[0:1, :]
        v_nyq = _dot(u_ref[h:h + 16, :], mc_s[...])[0:1, :]
        sign_col = jnp.where(lax.broadcasted_iota(jnp.int32, (blk, GROUP_W), 0) % 2 == 0, 1.0, -1.0)
        first_row = lax.broadcasted_iota(jnp.int32, (blk, GROUP_W), 0) == 0
        d_s[h:h + blk, :] = jnp.where(first_row, y_nyq, 0.0).astype(BF16)
        nyq = sign_col * v_nyq
        tall = min(FFN_ROWS, h)
        nyq = jnp.concatenate([nyq] * (tall // blk), axis=0)
        for k in range(h // tall):
            rows = slice(k * tall, (k + 1) * tall)
            a = _dot(ch_ref[rows, :], ve_s[...]) + nyq
            b = _dot(sh_ref[rows, :], vo_s[...])
            yf_s[rows, :] = (a + b).astype(BF16)
            d_s[rows, :] = (a - b).astype(BF16)
        for k in range(n_blk):
            yf_s[h + k * blk:h + (k + 1) * blk, :] = reversed_block(
                d_s, 0, k, d_s[h:h + blk, :]).astype(BF16)

    tm = x_ref.shape[0]
    sub = min(FFN_ROWS, tm)
    subs = [slice(sb * sub, (sb + 1) * sub) for sb in range(tm // sub)]
    ys = []
    for sb, rs in enumerate(subs):
        r0 = pl.multiple_of(m * tm + sb * sub, sub)
        yd = (on_ref[rs, :].astype(F32) * z_ref[rs, :].astype(F32)).astype(BF16)
        ys.append(_dot(jnp.concatenate([yf_s[pl.ds(r0, sub), :], yd], axis=1), wout_s[...]))
    for rs, y in zip(subs, ys):
        x1 = x_ref[rs, :] + gt_ref[...] * (_rms(y) * gpost_ref[...])
        x1_ref[rs, :] = x1
        h2 = _rms(x1) * gpre_ref[...]
        h2_ref[rs, :] = (h2 * (1.0 + scale_ref[...]) + shift_ref[...]).astype(BF16)


def _mix(pg, on, ch, sh, mcs, w_out, x, g_post, gt, g_pre, shift, scale):
    b, l, d = x.shape
    tm = MIX_TILE
    h = l // 2
    vec = pl.BlockSpec((1, d), lambda i, m: (0, 0))
    mod = pl.BlockSpec((None, 1, d), lambda i, m: (i, 0, 0))
    whole = lambda a: pl.BlockSpec(a.shape, lambda i, m: (0,) * a.ndim)
    return pl.pallas_call(
        _mix_kernel,
        grid=(b, l // tm),
        in_specs=[pl.BlockSpec((None, l, GROUP_W), lambda i, m: (i, 0, 0)),
                  pl.BlockSpec((None, tm, GROUP_W), lambda i, m: (i, m, 0)),
                  pl.BlockSpec((None, tm, GROUP_W), lambda i, m: (i, m, N_GROUPS - 1)),
                  whole(ch), whole(sh), whole(mcs), whole(w_out),
                  pl.BlockSpec((None, tm, d), lambda i, m: (i, m, 0)),
                  vec, mod, vec, mod, mod],
        out_specs=[pl.BlockSpec((None, tm, d), lambda i, m: (i, m, 0)),
                   pl.BlockSpec((None, tm, d), lambda i, m: (i, m, 0))],
        out_shape=[jax.ShapeDtypeStruct((b, l, d), F32),
                   jax.ShapeDtypeStruct((b, l, d), BF16)],
        scratch_shapes=[pltpu.VMEM((l, GROUP_W), BF16),
                        pltpu.VMEM((GROUP_W, GROUP_W), BF16),
                        pltpu.VMEM((GROUP_W, GROUP_W), BF16),
                        pltpu.VMEM(w_out.shape, BF16),
                        pltpu.VMEM((h, GROUP_W), BF16),
                        pltpu.VMEM((h, GROUP_W), BF16),
                        pltpu.VMEM((h + PAIR, GROUP_W), BF16)],
        compiler_params=_params(48, ("arbitrary", "arbitrary")),
        name="mix",
    )(pg, on, pg, ch, sh, mcs, w_out, x, g_post, gt, g_pre, shift, scale)


def _ffn_kernel(h_ref, x1_ref, wv_ref, wg_ref, dwc_ref, wd_ref, gpost_ref, gt_ref,
                o_ref, gate_s, val_s, wvg_s, wd_s):
    ROW_BLOCK = FFN_ROWS
    j = pl.program_id(1)
    l = h_ref.shape[0]
    tf = wv_ref.shape[1]
    n_rb = l // ROW_BLOCK
    pad = GRID_W

    wvg_s[:, :tf] = wv_ref[...].astype(BF16)
    wvg_s[:, tf:] = wg_ref[...].astype(BF16)
    wd_s[...] = wd_ref[...].astype(BF16)
    gate_s[0:pad, :] = jnp.zeros((pad, tf), F32)
    gate_s[l + pad:l + 2 * pad, :] = jnp.zeros((pad, tf), F32)

    @pl.when(j == 0)
    def _():
        o_ref[...] = jnp.zeros(o_ref.shape, F32)

    n_ext = ROW_BLOCK + 2 * pad
    col = lax.broadcasted_iota(jnp.int32, (n_ext, tf), 0) % GRID_W
    not_first = jnp.where(col == 0, 0.0, 1.0)
    not_last = jnp.where(col == GRID_W - 1, 0.0, 1.0)
    dw = dwc_ref[...]

    def up(i):
        r0 = i * ROW_BLOCK
        vg = _dot(h_ref[r0:r0 + ROW_BLOCK, :], wvg_s[...])
        val_s[r0:r0 + ROW_BLOCK, :] = vg[:, :tf]
        gate_s[r0 + pad:r0 + pad + ROW_BLOCK, :] = vg[:, tf:]

    def down(i):
        r0 = i * ROW_BLOCK
        ext = gate_s[r0:r0 + n_ext, :]
        shifted = (pltpu.roll(ext, 1, 0) * not_first, ext, pltpu.roll(ext, n_ext - 1, 0) * not_last)
        conv = None
        for dy in range(3):
            for dx in range(3):
                term = shifted[dx][dy * pad:dy * pad + ROW_BLOCK] * dw[3 * dy + dx:3 * dy + dx + 1]
                conv = term if conv is None else conv + term
        act = (_silu(conv) * val_s[r0:r0 + ROW_BLOCK, :]).astype(BF16)
        o_ref[r0:r0 + ROW_BLOCK, :] += _dot(act, wd_s[...])

    for i in range(min(2, n_rb)):
        up(i)
    for i in range(n_rb):
        if i + 2 < n_rb:
            up(i + 2)
        down(i)

    @pl.when(j == pl.num_programs(1) - 1)
    def _():
        def fin(rb, carry):
            r0 = pl.multiple_of(rb * ROW_BLOCK, ROW_BLOCK)
            y = _rms(o_ref[pl.ds(r0, ROW_BLOCK), :]) * gpost_ref[...]
            o_ref[pl.ds(r0, ROW_BLOCK), :] = x1_ref[pl.ds(r0, ROW_BLOCK), :] + gt_ref[...] * y
            return carry
        lax.fori_loop(0, n_rb, fin, 0)


def _ffn(h2, x1, w_up, dwc, w_down, g_post, gt):
    b, l, d = x1.shape
    d_ff = w_down.shape[0]
    tf = FFN_TILE
    n_f = d_ff // tf
    return pl.pallas_call(
        _ffn_kernel,
        grid=(b, n_f),
        in_specs=[pl.BlockSpec((None, l, d), lambda i, j: (i, 0, 0)),
                  pl.BlockSpec((None, l, d), lambda i, j: (i, 0, 0)),
                  pl.BlockSpec((d, tf), lambda i, j: (0, j)),
                  pl.BlockSpec((d, tf), lambda i, j: (0, n_f + j)),
                  pl.BlockSpec((9, tf), lambda i, j: (0, j)),
                  pl.BlockSpec((tf, d), lambda i, j: (j, 0)),
                  pl.BlockSpec((1, d), lambda i, j: (0, 0)),
                  pl.BlockSpec((None, 1, d), lambda i, j: (i, 0, 0))],
        out_specs=pl.BlockSpec((None, l, d), lambda i, j: (i, 0, 0)),
        out_shape=jax.ShapeDtypeStruct((b, l, d), F32),
        scratch_shapes=[pltpu.VMEM((l + 2 * GRID_W, tf), F32),
                        pltpu.VMEM((l, tf), F32),
                        pltpu.VMEM((d, 2 * tf), BF16),
                        pltpu.VMEM((tf, d), BF16)],
        compiler_params=_params(58, ("arbitrary", "arbitrary")),
        name="ffn",
    )(h2, x1, w_up, w_up, dwc, w_down, g_post, gt)


def kernel(x, c, ctx, c_ctx, w_ada, b_ada, g_pre_mix, g_post_mix, g_pre_ffn, g_post_ffn,
           w_in, w_qkv_conv, a_log, dt_bias, g_gdn, w_fourier, w_out, w_up, w_dwc, w_down):
    b, l, d = x.shape
    assert w_ada.shape[0] == 1, "single-layer stack"
    assert b < 16 and l % MIX_TILE == 0 and ctx.shape[1] % ROW_BLOCK == 0
    assert w_in.shape[2] == N_GROUPS * GROUP_W + N_GATES and d == 2 * GROUP_W

    cc = jnp.zeros((16, d), F32).at[:b].set(c).at[b].set(c_ctx)
    mod = _ada(cc, w_ada[0], b_ada)
    sh1, sc1, gt1, sh2, sc2, gt2 = [m[:, None, :] for m in jnp.split(mod[:b], 6, axis=-1)]
    sh1c, sc1c = [m[None, :] for m in jnp.split(mod[b], 6, axis=-1)[:2]]

    wab = jnp.pad(w_in[0][:, N_GROUPS * GROUP_W:], ((0, 0), (0, PAIR - N_GATES)))
    zeros8 = jnp.zeros((8,), F32)
    alog = jnp.concatenate([zeros8, a_log[0].reshape(-1)])[:, None]
    dtb = jnp.concatenate([zeros8, dt_bias[0].reshape(-1)])[:, None]

    pg, kt, gat = _inproj(x, ctx, g_pre_mix, sh1, sc1, sh1c, sc1c, w_in[0], wab, w_qkv_conv[0], alog, dtb)
    on = _gdn(pg, kt, gat, g_gdn, l)

    ch, sh, c_c, s_c = _dft_constants(l, HEAD_D)
    mcs, ch, sh = _fourier_w(c_c, s_c, w_fourier[0], ch, sh)
    x1, h2 = _mix(pg, on, ch, sh, mcs, w_out[0], x, g_post_mix, gt1, g_pre_ffn, sh2, sc2)
    d_ff = w_down.shape[1]
    return _ffn(h2, x1, w_up[0], w_dwc[0].reshape(9, d_ff), w_down[0], g_post_ffn, gt2)
```

```python
import functools
import itertools
import math

import numpy as np
import jax
import jax.numpy as jnp
from jax import lax
from jax.experimental import pallas as pl
from jax.experimental.pallas import tpu as pltpu

F32 = jnp.float32
BF16 = jnp.bfloat16

GRID_W = 64
HEADS = 4
HEAD_D = 128
CHUNK = 64
PAIR = 2 * CHUNK
GROUP_W = HEADS * HEAD_D
N_GROUPS = 5
N_GATES = 16
EPS = 1e-6
ROW_BLOCK = 256
FFN_TILE = 256
MIX_TILE = 1024
FFN_ROWS = 512
MIB = 1024 * 1024

_NT = (((1,), (1,)), ((), ()))


def _dot(a, b):
    return jnp.dot(a, b, preferred_element_type=F32)


def _silu(x):
    return x / (1.0 + jnp.exp(-x))


def _rms(x):
    return x * lax.rsqrt(jnp.mean(x * x, axis=-1, keepdims=True) + EPS)


def _params(vmem_mib, semantics):
    return pltpu.CompilerParams(dimension_semantics=semantics,
                                vmem_limit_bytes=vmem_mib * MIB)


def _ada_kernel(c_ref, w_ref, b_ref, o_ref):
    s = _silu(c_ref[...]).astype(BF16)
    o_ref[...] = _dot(s, w_ref[...].astype(BF16)) + b_ref[...]


def _ada(cc, w, b):
    rows, d = cc.shape
    n = w.shape[1]
    tn = 1024
    return pl.pallas_call(
        _ada_kernel,
        grid=(n // tn,),
        in_specs=[pl.BlockSpec((rows, d), lambda j: (0, 0)),
                  pl.BlockSpec((d, tn), lambda j: (0, j)),
                  pl.BlockSpec((1, tn), lambda j: (0, j))],
        out_specs=pl.BlockSpec((rows, tn), lambda j: (0, j)),
        out_shape=jax.ShapeDtypeStruct((rows, n), F32),
        compiler_params=_params(32, ("arbitrary",)),
        name="ada",
    )(cc, w, b)


def _inproj_kernel(x_ref, ctx_ref, g_ref, sh_ref, sc_ref, shc_ref, scc_ref, w_ref, wab_ref,
                   conv_ref, alog_ref, dtb_ref, pg_ref, kt_ref, gat_ref, h_s, w_s, p_s):
    j = pl.program_id(1)
    l, lc = x_ref.shape[0], ctx_ref.shape[0]
    blocks = [(x_ref, r, r, r + 8, sh_ref, sc_ref, min(FFN_ROWS, l - r)) for r in range(0, l, FFN_ROWS)]
    blocks += [(ctx_ref, r, l + r, l + r + 16, shc_ref, scc_ref, min(FFN_ROWS, lc - r))
               for r in range(0, lc, FFN_ROWS)]
    n_rb = len(blocks)

    w_s[...] = w_ref[...].astype(BF16)

    def out_rows(rb):
        return slice(blocks[rb][2], blocks[rb][2] + blocks[rb][6])

    def project(rb):
        return lax.dot_general(h_s[out_rows(rb), :], w_s[...], _NT, preferred_element_type=F32)

    def pipelined(first, second, lead):
        for rb in range(min(lead, n_rb)):
            first(rb)
        for rb in range(n_rb):
            if rb + lead < n_rb:
                first(rb + lead)
            second(rb)

    @pl.when(j == 0)
    def _u():
        for r in (0, l + 8, l + lc + 16):
            p_s[r:r + 8, :] = jnp.zeros((8, GROUP_W), F32)
        wabt = wab_ref[...].astype(BF16)
        row = lax.broadcasted_iota(jnp.int32, (N_GATES, PAIR), 0)

        def prologue(rb):
            src, r, dst, _, shift, scale, n = blocks[rb]
            h = _rms(src[r:r + n, :]) * g_ref[...]
            hb = (h * (1.0 + scale[...]) + shift[...]).astype(BF16)
            h_s[dst:dst + n, :] = hb
            for s in range(n // PAIR):
                ab = lax.dot_general(wabt, hb[s * PAIR:(s + 1) * PAIR], _NT,
                                     preferred_element_type=F32)
                beta = 1.0 / (1.0 + jnp.exp(-ab))
                xs = ab + dtb_ref[...]
                softplus = jnp.maximum(xs, 0.0) + jnp.log1p(jnp.exp(-jnp.abs(xs)))
                gate = -jnp.exp(alog_ref[...]) * softplus
                gat_ref[dst // PAIR + s] = jnp.where(row < 8, beta, gate)

        def store_u(rb):
            pg_ref[out_rows(rb), :] = project(rb).astype(BF16)

        pipelined(prologue, store_u, 2)

    @pl.when(j == N_GROUPS - 1)
    def _z():
        for rb in range(n_rb):
            pg_ref[out_rows(rb), :] = _silu(project(rb)).astype(BF16)

    def conv_group(normalise, transpose):
        cw = conv_ref[...]

        def raw(rb):
            p0, n = blocks[rb][3], blocks[rb][6]
            p_s[p0:p0 + n, :] = project(rb)

        def finish(rb):
            p0, n = blocks[rb][3], blocks[rb][6]
            ext = p_s[p0 - 8:p0 + n + 8, :]
            prev = pltpu.roll(ext, 1, 0)[8:8 + n]
            nxt = pltpu.roll(ext, n + 15, 0)[8:8 + n]
            cur = ext[8:8 + n]
            a = _silu(prev * cw[0:1] + cur * cw[1:2] + nxt * cw[2:3])
            if normalise:
                segs = []
                for hh in range(HEADS):
                    seg = a[:, hh * HEAD_D:(hh + 1) * HEAD_D]
                    segs.append(seg * lax.rsqrt(jnp.sum(seg * seg, axis=-1, keepdims=True) + EPS))
                a = jnp.concatenate(segs, axis=1)
            pg_ref[out_rows(rb), :] = a.astype(BF16)
            if transpose:
                at = a.T.astype(BF16)
                for s in range(n // PAIR):
                    kt_ref[blocks[rb][2] // PAIR + s] = at[:, s * PAIR:(s + 1) * PAIR]

        pipelined(raw, finish, 2)

    @pl.when(j == 1)
    def _q():
        conv_group(True, False)

    @pl.when(j == 2)
    def _k():
        conv_group(True, True)

    @pl.when(j == 3)
    def _v():
        conv_group(False, False)


def _inproj(x, ctx, g, shift, scale, shift_c, scale_c, w_in_t, wabt, conv, alog, dtb):
    b, l, d = x.shape
    lt = l + ctx.shape[1]
    n_pairs = lt // PAIR
    vec = pl.BlockSpec((1, d), lambda i, j: (0, 0))
    mod = pl.BlockSpec((None, 1, d), lambda i, j: (i, 0, 0))
    return pl.pallas_call(
        _inproj_kernel,
        grid=(b, N_GROUPS),
        in_specs=[pl.BlockSpec((None, l, d), lambda i, j: (i, 0, 0)),
                  pl.BlockSpec((None, lt - l, d), lambda i, j: (i, 0, 0)),
                  vec, mod, mod, vec, vec,
                  pl.BlockSpec((GROUP_W, d), lambda i, j: (j, 0)),
                  pl.BlockSpec((N_GATES, d), lambda i, j: (0, 0)),
                  pl.BlockSpec((3, GROUP_W), lambda i, j: (0, jnp.clip(j - 1, 0, 2))),
                  pl.BlockSpec((N_GATES, 1), lambda i, j: (0, 0)),
                  pl.BlockSpec((N_GATES, 1), lambda i, j: (0, 0))],
        out_specs=[pl.BlockSpec((None, lt, GROUP_W), lambda i, j: (i, 0, j)),
                   pl.BlockSpec((None, n_pairs, GROUP_W, PAIR), lambda i, j: (i, 0, 0, 0)),
                   pl.BlockSpec((None, n_pairs, N_GATES, PAIR), lambda i, j: (i, 0, 0, 0))],
        out_shape=[jax.ShapeDtypeStruct((b, lt, N_GROUPS * GROUP_W), BF16),
                   jax.ShapeDtypeStruct((b, n_pairs, GROUP_W, PAIR), BF16),
                   jax.ShapeDtypeStruct((b, n_pairs, N_GATES, PAIR), F32)],
        scratch_shapes=[pltpu.VMEM((lt, d), BF16),
                        pltpu.VMEM((GROUP_W, d), BF16),
                        pltpu.VMEM((lt + 24, GROUP_W), F32)],
        compiler_params=_params(52, ("arbitrary", "arbitrary")),
        name="inproj",
    )(x, ctx, g, shift, scale, shift_c, scale_c, w_in_t, wabt, conv, alog, dtb)


def _bd2(a, b):
    z = jnp.zeros_like(a)
    return jnp.concatenate([jnp.concatenate([a, z], axis=1),
                            jnp.concatenate([z, b], axis=1)], axis=0)


def _bd_of(x):
    xb = x.astype(BF16)
    return _bd2(xb[:, :HEAD_D], xb[:, HEAD_D:])


def _round_robin(*stage_generators):
    for _ in itertools.zip_longest(*stage_generators):
        pass


def _gdn_kernel(q_ref, k_ref, v_ref, kt_ref, gat_ref, g_ref, y_ref,
                u_s, wq_s, at_s, kdt_s, egl_s, s_s, o_s):
    lt, l = q_ref.shape[0], y_ref.shape[0]
    n_steps = lt // PAIR
    n_ctx = (lt - l) // PAIR
    per_iter = 2
    assert n_steps % per_iter == 0

    ri = lax.broadcasted_iota(jnp.int32, (PAIR, PAIR), 0)
    ci = lax.broadcasted_iota(jnp.int32, (PAIR, PAIR), 1)
    same = (ri // CHUNK) == (ci // CHUNK)
    incl = (jnp.where(same & (ri >= ci), 1.0, 0.0), jnp.where(same & (ri <= ci), 1.0, 0.0))
    strict = (jnp.where(same & (ri > ci), 1.0, 0.0), jnp.where(same & (ri < ci), 1.0, 0.0))
    tri = (incl[1], incl[0])
    eye = jnp.where(ri == ci, 1.0, 0.0)
    eye2 = jnp.concatenate([eye, eye], axis=1)
    off = ([], [])
    for lvl in range(6):
        b = 2 ** lvl
        joined = (ri // (2 * b)) == (ci // (2 * b))
        off[0].append(jnp.where(joined & (ri % (2 * b) >= b) & (ci % (2 * b) < b), 1.0, 0.0))
        off[1].append(jnp.where(joined & (ri % (2 * b) < b) & (ci % (2 * b) >= b), 1.0, 0.0))

    def joining_blocks(a, d, lvl):
        m = off[d][lvl]
        return jnp.concatenate([a[:, :HEAD_D] * m, a[:, HEAD_D:] * m], axis=1)

    lane = lax.broadcasted_iota(jnp.int32, (1, PAIR), 1)
    first_half = lane < CHUNK
    scale = HEAD_D ** -0.5
    hi = lax.Precision.HIGHEST
    lanes = [slice(hp * 2 * HEAD_D, (hp + 1) * 2 * HEAD_D) for hp in range(HEADS // 2)]
    stream_ids = [(hp, d) for hp in range(HEADS // 2) for d in range(2)]
    zeros_half = jnp.zeros((CHUNK, 2 * HEAD_D), BF16)

    def pair_of(step, d):
        if d == 0:
            return lax.rem(step + (n_steps - n_ctx), n_steps)
        return n_steps - 1 - step

    def prepare(slot, step0):
        streams = []
        for kk in range(per_iter):
            for d in range(2):
                p = pair_of(step0 + kk, d)
                t0 = pl.multiple_of(p * PAIR, PAIR)
                gates = gat_ref[p]
                kt_all = kt_ref[p]
                gc = jnp.dot(gates, tri[d], precision=hi, preferred_element_type=F32)
                for hp in range(HEADS // 2):
                    k2 = k_ref[pl.ds(t0, PAIR), lanes[hp]]
                    q2 = q_ref[pl.ds(t0, PAIR), lanes[hp]]
                    v2 = v_ref[pl.ds(t0, PAIR), lanes[hp]]
                    kt2 = kt_all[hp * 2 * HEAD_D:(hp + 1) * 2 * HEAD_D, :]
                    bd_kt = _bd2(kt2[:HEAD_D], kt2[HEAD_D:])
                    streams.append(dict(kk=kk, d=d, hp=hp, gates=gates, gc=gc, k2=k2, q2=q2, v2=v2,
                                        kt2=kt2, kk_=_dot(k2, bd_kt), qk_=_dot(q2, bd_kt)))
        yield
        for st in streams:
            kk, d, hp, gates, gc = st["kk"], st["d"], st["hp"], st["gates"], st["gc"]
            gc_t = gc.T
            a, attn, qd, kdt, erow = [], [], [], [], []
            for hh in range(2):
                h = 2 * hp + hh
                hs = slice(hh * HEAD_D, (hh + 1) * HEAD_D)
                brow = gates[4 * d + h:4 * d + h + 1, :]
                gcr = gc[8 + 4 * d + h:9 + 4 * d + h, :]
                gcc = gc_t[:, 8 + 4 * d + h:9 + 4 * d + h]
                decay = jnp.exp((gcc - gcr) * incl[d]) * incl[d]
                a.append(st["kk_"][:, hs] * decay * strict[d] * brow)
                attn.append(st["qk_"][:, hs] * decay * (brow * scale))
                qd.append(st["q2"][:, hs].astype(F32) * (jnp.exp(gcc) * scale))
                if d == 0:
                    gl0, gl1 = gcr[:, CHUNK - 1:CHUNK], gcr[:, PAIR - 1:PAIR]
                else:
                    gl0, gl1 = gcr[:, 0:1], gcr[:, CHUNK:CHUNK + 1]
                glr = jnp.where(first_half, gl0, gl1)
                kdt.append(st["kt2"][hs, :].astype(F32) * (jnp.exp(glr - gcr) * brow))
                erow.append(jnp.exp(gcr))
                egl_s[slot, d, kk, 0, h:h + 1, :] = jnp.broadcast_to(jnp.exp(gl0), (1, PAIR))
                egl_s[slot, d, kk, 1, h:h + 1, :] = jnp.broadcast_to(jnp.exp(gl1), (1, PAIR))
            st["a"] = jnp.concatenate(a, axis=1)
            st["erow"] = jnp.concatenate(erow, axis=1)
            at_s[slot, d, kk, :, lanes[hp]] = jnp.concatenate(attn, axis=1).astype(BF16)
            kdt_s[slot, d, kk, hp] = jnp.concatenate(kdt, axis=1).astype(BF16)
            st["qd"] = jnp.concatenate(qd, axis=1).astype(BF16)
            st["t"] = eye2 - joining_blocks(st["a"], d, 0)
        yield
        for lvl in range(1, 6):
            for st in streams:
                st["y"] = _dot(joining_blocks(st["a"], st["d"], lvl).astype(BF16), _bd_of(st["t"]))
            yield
            for st in streams:
                st["t"] = st["t"] - _dot(st["t"].astype(BF16), _bd_of(st["y"]))
            yield
        for st in streams:
            t = st["t"]
            st["u"] = _dot(t.astype(BF16), _bd2(st["v2"][:, :HEAD_D], st["v2"][:, HEAD_D:]))
            st["w"] = _dot((t * st["erow"]).astype(BF16), _bd2(st["k2"][:, :HEAD_D], st["k2"][:, HEAD_D:]))
        yield
        for st in streams:
            kk, d, hp = st["kk"], st["d"], st["hp"]
            u_s[slot, d, kk, :, lanes[hp]] = st["u"].astype(BF16)
            wb, qdb = st["w"].astype(BF16), st["qd"]
            wq_s[slot, d, kk, :, lanes[hp]] = jnp.concatenate(
                [wb[:CHUNK], qdb[:CHUNK], wb[CHUNK:], qdb[CHUNK:]], axis=0)
        yield

    def scan(slot, step0):
        state = {(hp, d): s_s[d, :, lanes[hp]] for (hp, d) in stream_ids}
        for kk in range(per_iter):
            for half in range(2):
                e = {0: half, 1: 1 - half}
                res, bd_x = {}, {}
                for (hp, d) in stream_ids:
                    res[(hp, d)] = _dot(wq_s[slot, d, kk, e[d] * PAIR:(e[d] + 1) * PAIR, lanes[hp]],
                                        _bd_of(state[(hp, d)]))
                yield
                for (hp, d) in stream_ids:
                    vn = (u_s[slot, d, kk, e[d] * CHUNK:(e[d] + 1) * CHUNK, lanes[hp]].astype(F32)
                          - res[(hp, d)][:CHUNK])
                    vnb = vn.astype(BF16)
                    x = jnp.concatenate([vnb, zeros_half] if e[d] == 0 else [zeros_half, vnb], axis=0)
                    bd_x[(hp, d)] = _bd2(x[:, :HEAD_D], x[:, HEAD_D:])
                for (hp, d) in stream_ids:
                    both = _dot(jnp.concatenate(
                        [kdt_s[slot, d, kk, hp],
                         at_s[slot, d, kk, e[d] * CHUNK:(e[d] + 1) * CHUNK, lanes[hp]]], axis=0),
                        bd_x[(hp, d)])
                    eg = jnp.concatenate([egl_s[slot, d, kk, e[d], 2 * hp:2 * hp + 1, :],
                                          egl_s[slot, d, kk, e[d], 2 * hp + 1:2 * hp + 2, :]], axis=1)
                    state[(hp, d)] = state[(hp, d)] * eg + both[:HEAD_D]
                    r0 = pl.multiple_of(pair_of(step0 + kk, d) * PAIR + e[d] * CHUNK, CHUNK)
                    o_s[pl.ds(r0, CHUNK), lanes[hp]] += res[(hp, d)][CHUNK:] + both[HEAD_D:]
                yield
        for (hp, d) in stream_ids:
            s_s[d, :, lanes[hp]] = state[(hp, d)]

    s_s[...] = jnp.zeros(s_s.shape, F32)
    o_s[...] = jnp.zeros(o_s.shape, F32)
    n_iter = n_steps // per_iter
    _round_robin(prepare(0, 0))

    def body(it, carry):
        _round_robin(prepare(lax.rem(it + 1, 2), (it + 1) * per_iter), scan(lax.rem(it, 2), it * per_iter))
        return carry

    lax.fori_loop(0, n_iter - 1, body, 0)
    _round_robin(scan((n_iter - 1) % 2, (n_iter - 1) * per_iter))

    def epilogue(rb, carry):
        r0 = pl.multiple_of(rb * ROW_BLOCK, ROW_BLOCK)
        o = o_s[pl.ds(r0, ROW_BLOCK), :]
        segs = [_rms(o[:, h * HEAD_D:(h + 1) * HEAD_D]) * g_ref[...] for h in range(HEADS)]
        y_ref[pl.ds(r0, ROW_BLOCK), :] = jnp.concatenate(segs, axis=1).astype(BF16)
        return carry

    lax.fori_loop(0, l // ROW_BLOCK, epilogue, 0)


def _gdn(pg, kt, gat, g_gdn, l):
    b, lt, _ = pg.shape
    n_pairs = lt // PAIR

    def group(g):
        return pl.BlockSpec((None, lt, GROUP_W), lambda i, g=g: (i, 0, g))

    ring = (2, 2, 2)
    return pl.pallas_call(
        _gdn_kernel,
        grid=(b,),
        in_specs=[group(1), group(2), group(3),
                  pl.BlockSpec((None, n_pairs, GROUP_W, PAIR), lambda i: (i, 0, 0, 0)),
                  pl.BlockSpec((None, n_pairs, N_GATES, PAIR), lambda i: (i, 0, 0, 0)),
                  pl.BlockSpec((1, HEAD_D), lambda i: (0, 0))],
        out_specs=pl.BlockSpec((None, l, GROUP_W), lambda i: (i, 0, 0)),
        out_shape=jax.ShapeDtypeStruct((b, l, GROUP_W), BF16),
        scratch_shapes=[pltpu.VMEM(ring + (PAIR, GROUP_W), BF16),
                        pltpu.VMEM(ring + (2 * PAIR, GROUP_W), BF16),
                        pltpu.VMEM(ring + (PAIR, GROUP_W), BF16),
                        pltpu.VMEM(ring + (HEADS // 2, HEAD_D, 2 * PAIR), BF16),
                        pltpu.VMEM(ring + (2, 8, PAIR), F32),
                        pltpu.VMEM((2, HEAD_D, GROUP_W), F32),
                        pltpu.VMEM((lt, GROUP_W), F32)],
        compiler_params=_params(48, ("arbitrary",)),
        name="gdn",
    )(pg, pg, pg, kt, gat, g_gdn)


def _fourier_w_kernel(cc_ref, sc_ref, w_ref, ch_ref, sh_ref, o_ref, chb_ref, shb_ref):
    hi = lax.Precision.HIGHEST
    for g in range(w_ref.shape[0]):
        w = w_ref[g]
        o_ref[g] = jnp.concatenate(
            [jnp.dot(cc_ref[...], w, precision=hi, preferred_element_type=F32),
             -jnp.dot(sc_ref[...], w, precision=hi, preferred_element_type=F32)], axis=1)
    chb_ref[...] = ch_ref[...].astype(BF16)
    shb_ref[...] = sh_ref[...].astype(BF16)


def _fourier_w(cc, sc, w_f, ch, sh):
    groups, c, d = w_f.shape
    return pl.pallas_call(
        _fourier_w_kernel,
        out_shape=[jax.ShapeDtypeStruct((groups, c, 2 * d), F32),
                   jax.ShapeDtypeStruct(ch.shape, BF16),
                   jax.ShapeDtypeStruct(sh.shape, BF16)],
        compiler_params=pltpu.CompilerParams(vmem_limit_bytes=32 * MIB),
        name="fourier_w",
    )(cc, sc, w_f, ch, sh)


@functools.lru_cache(maxsize=None)
def _dft_constants(length, channels):
    def cos_sin(n, keep):
        idx = np.arange(keep, dtype=np.int64)
        ang = (np.outer(idx, idx) % n).astype(np.float64) * (2.0 * np.pi / n)
        return np.cos(ang), np.sin(ang)
    c_l, s_l = cos_sin(length, length // 2)
    c_c, s_c = cos_sin(channels, channels)
    norm = 1.0 / math.sqrt(length * channels)
    return (jnp.asarray(c_l.astype(np.float32)), jnp.asarray(s_l.astype(np.float32)),
            jnp.asarray((c_c * norm).astype(np.float32)),
            jnp.asarray((s_c * norm).astype(np.float32)))


def _mix_kernel(u_ref, on_ref, z_ref, ch_ref, sh_ref, mcs_ref, wout_ref, x_ref, gpost_ref, gt_ref,
                gpre_ref, shift_ref, scale_ref, x1_ref, h2_ref,
                yf_s, mc_s, ms_s, wout_s, ve_s, vo_s, d_s):
    i, m = pl.program_id(0), pl.program_id(1)
    l = u_ref.shape[0]
    h = l // 2
    blk = PAIR
    n_blk = h // blk

    @pl.when((i == 0) & (m == 0))
    def _():
        wout_s[...] = wout_ref[...].astype(BF16)
        mc_s[...] = jnp.zeros(mc_s.shape, BF16)
        ms_s[...] = jnp.zeros(ms_s.shape, BF16)
        for g in range(HEADS):
            gs = slice(g * HEAD_D, (g + 1) * HEAD_D)
            mc_s[gs, gs] = mcs_ref[g][:, :HEAD_D].astype(BF16)
            ms_s[gs, gs] = mcs_ref[g][:, HEAD_D:].astype(BF16)

    @pl.when(m == 0)
    def _():
        ri = lax.broadcasted_iota(jnp.int32, (blk, 2 * blk), 0)
        ci = lax.broadcasted_iota(jnp.int32, (blk, 2 * blk), 1)
        rev = jnp.where(((ci < blk) & (ri + ci == blk)) | ((ri == 0) & (ci == blk)), 1.0, 0.0).astype(BF16)
        zero_blk = jnp.zeros((blk, GROUP_W), BF16)

        def reversed_block(ref, base, k, last):
            top = ref[base + (n_blk - 1 - k) * blk:base + (n_blk - k) * blk, :]
            bottom = last if k == 0 else ref[base + (n_blk - k) * blk:base + (n_blk - k + 1) * blk, :]
            return _dot(rev, jnp.concatenate([top, bottom], axis=0))

        for k in range(n_blk):
            low = u_ref[k * blk:(k + 1) * blk, :].astype(F32)
            mirrored = reversed_block(u_ref, h, k, zero_blk)
            yf_s[k * blk:(k + 1) * blk, :] = (low + mirrored).astype(BF16)
            yf_s[h + k * blk:h + (k + 1) * blk, :] = (low - mirrored).astype(BF16)
        ve_s[...] = _dot(yf_s[:h, :], mc_s[...]).astype(BF16)
        vo_s[...] = _dot(yf_s[h:, :], ms_s[...]).astype(BF16)
        sign_row = jnp.where(lax.broadcasted_iota(jnp.int32, (16, l), 1) % 2 == 0, 1.0, -1.0)
        sign_row = jnp.where(lax.broadcasted_iota(jnp.int32, (16, l), 0) == 0, sign_row, 0.0).astype(BF16)
        alt_u = _dot(sign_row, u_ref[...])
        y_nyq = _dot(alt_u.astype(BF16), mc_s[...])[0:1, :]
        v_nyq = _dot(u_ref[h:h + 16, :], mc_s[...])[0:1, :]
        sign_col = jnp.where(lax.broadcasted_iota(jnp.int32, (blk, GROUP_W), 0) % 2 == 0, 1.0, -1.0)
        first_row = lax.broadcasted_iota(jnp.int32, (blk, GROUP_W), 0) == 0
        d_s[h:h + blk, :] = jnp.where(first_row, y_nyq, 0.0).astype(BF16)
        nyq = sign_col * v_nyq
        tall = min(FFN_ROWS, h)
        nyq = jnp.concatenate([nyq] * (tall // blk), axis=0)
        for k in range(h // tall):
            rows = slice(k * tall, (k + 1) * tall)
            a = _dot(ch_ref[rows, :], ve_s[...]) + nyq
            b = _dot(sh_ref[rows, :], vo_s[...])
            yf_s[rows, :] = (a + b).astype(BF16)
            d_s[rows, :] = (a - b).astype(BF16)
        for k in range(n_blk):
            yf_s[h + k * blk:h + (k + 1) * blk, :] = reversed_block(
                d_s, 0, k, d_s[h:h + blk, :]).astype(BF16)

    tm = x_ref.shape[0]
    sub = min(FFN_ROWS, tm)
    subs = [slice(sb * sub, (sb + 1) * sub) for sb in range(tm // sub)]
    ys = []
    for sb, rs in enumerate(subs):
        r0 = pl.multiple_of(m * tm + sb * sub, sub)
        yd = (on_ref[rs, :].astype(F32) * z_ref[rs, :].astype(F32)).astype(BF16)
        ys.append(_dot(jnp.concatenate([yf_s[pl.ds(r0, sub), :], yd], axis=1), wout_s[...]))
    for rs, y in zip(subs, ys):
        x1 = x_ref[rs, :] + gt_ref[...] * (_rms(y) * gpost_ref[...])
        x1_ref[rs, :] = x1
        h2 = _rms(x1) * gpre_ref[...]
        h2_ref[rs, :] = (h2 * (1.0 + scale_ref[...]) + shift_ref[...]).astype(BF16)


def _mix(pg, on, ch, sh, mcs, w_out, x, g_post, gt, g_pre, shift, scale):
    b, l, d = x.shape
    tm = MIX_TILE
    h = l // 2
    vec = pl.BlockSpec((1, d), lambda i, m: (0, 0))
    mod = pl.BlockSpec((None, 1, d), lambda i, m: (i, 0, 0))
    whole = lambda a: pl.BlockSpec(a.shape, lambda i, m: (0,) * a.ndim)
    return pl.pallas_call(
        _mix_kernel,
        grid=(b, l // tm),
        in_specs=[pl.BlockSpec((None, l, GROUP_W), lambda i, m: (i, 0, 0)),
                  pl.BlockSpec((None, tm, GROUP_W), lambda i, m: (i, m, 0)),
                  pl.BlockSpec((None, tm, GROUP_W), lambda i, m: (i, m, N_GROUPS - 1)),
                  whole(ch), whole(sh), whole(mcs), whole(w_out),
                  pl.BlockSpec((None, tm, d), lambda i, m: (i, m, 0)),
                  vec, mod, vec, mod, mod],
        out_specs=[pl.BlockSpec((None, tm, d), lambda i, m: (i, m, 0)),
                   pl.BlockSpec((None, tm, d), lambda i, m: (i, m, 0))],
        out_shape=[jax.ShapeDtypeStruct((b, l, d), F32),
                   jax.ShapeDtypeStruct((b, l, d), BF16)],
        scratch_shapes=[pltpu.VMEM((l, GROUP_W), BF16),
                        pltpu.VMEM((GROUP_W, GROUP_W), BF16),
                        pltpu.VMEM((GROUP_W, GROUP_W), BF16),
                        pltpu.VMEM(w_out.shape, BF16),
                        pltpu.VMEM((h, GROUP_W), BF16),
                        pltpu.VMEM((h, GROUP_W), BF16),
                        pltpu.VMEM((h + PAIR, GROUP_W), BF16)],
        compiler_params=_params(48, ("arbitrary", "arbitrary")),
        name="mix",
    )(pg, on, pg, ch, sh, mcs, w_out, x, g_post, gt, g_pre, shift, scale)


def _ffn_kernel(h_ref, x1_ref, wv_ref, wg_ref, dwc_ref, wd_ref, gpost_ref, gt_ref,
                o_ref, gate_s, val_s, wvg_s, wd_s):
    ROW_BLOCK = FFN_ROWS
    j = pl.program_id(1)
    l = h_ref.shape[0]
    tf = wv_ref.shape[1]
    n_rb = l // ROW_BLOCK
    pad = GRID_W

    wvg_s[:, :tf] = wv_ref[...].astype(BF16)
    wvg_s[:, tf:] = wg_ref[...].astype(BF16)
    wd_s[...] = wd_ref[...].astype(BF16)
    gate_s[0:pad, :] = jnp.zeros((pad, tf), F32)
    gate_s[l + pad:l + 2 * pad, :] = jnp.zeros((pad, tf), F32)

    @pl.when(j == 0)
    def _():
        o_ref[...] = jnp.zeros(o_ref.shape, F32)

    n_ext = ROW_BLOCK + 2 * pad
    col = lax.broadcasted_iota(jnp.int32, (n_ext, tf), 0) % GRID_W
    not_first = jnp.where(col == 0, 0.0, 1.0)
    not_last = jnp.where(col == GRID_W - 1, 0.0, 1.0)
    dw = dwc_ref[...]

    def up(i):
        r0 = i * ROW_BLOCK
        vg = _dot(h_ref[r0:r0 + ROW_BLOCK, :], wvg_s[...])
        val_s[r0:r0 + ROW_BLOCK, :] = vg[:, :tf]
        gate_s[r0 + pad:r0 + pad + ROW_BLOCK, :] = vg[:, tf:]

    def down(i):
        r0 = i * ROW_BLOCK
        ext = gate_s[r0:r0 + n_ext, :]
        shifted = (pltpu.roll(ext, 1, 0) * not_first, ext, pltpu.roll(ext, n_ext - 1, 0) * not_last)
        conv = None
        for dy in range(3):
            for dx in range(3):
                term = shifted[dx][dy * pad:dy * pad + ROW_BLOCK] * dw[3 * dy + dx:3 * dy + dx + 1]
                conv = term if conv is None else conv + term
        act = (_silu(conv) * val_s[r0:r0 + ROW_BLOCK, :]).astype(BF16)
        o_ref[r0:r0 + ROW_BLOCK, :] += _dot(act, wd_s[...])

    for i in range(min(2, n_rb)):
        up(i)
    for i in range(n_rb):
        if i + 2 < n_rb:
            up(i + 2)
        down(i)

    @pl.when(j == pl.num_programs(1) - 1)
    def _():
        def fin(rb, carry):
            r0 = pl.multiple_of(rb * ROW_BLOCK, ROW_BLOCK)
            y = _rms(o_ref[pl.ds(r0, ROW_BLOCK), :]) * gpost_ref[...]
            o_ref[pl.ds(r0, ROW_BLOCK), :] = x1_ref[pl.ds(r0, ROW_BLOCK), :] + gt_ref[...] * y
            return carry
        lax.fori_loop(0, n_rb, fin, 0)


def _ffn(h2, x1, w_up, dwc, w_down, g_post, gt):
    b, l, d = x1.shape
    d_ff = w_down.shape[0]
    tf = FFN_TILE
    n_f = d_ff // tf
    return pl.pallas_call(
        _ffn_kernel,
        grid=(b, n_f),
        in_specs=[pl.BlockSpec((None, l, d), lambda i, j: (i, 0, 0)),
                  pl.BlockSpec((None, l, d), lambda i, j: (i, 0, 0)),
                  pl.BlockSpec((d, tf), lambda i, j: (0, j)),
                  pl.BlockSpec((d, tf), lambda i, j: (0, n_f + j)),
                  pl.BlockSpec((9, tf), lambda i, j: (0, j)),
                  pl.BlockSpec((tf, d), lambda i, j: (j, 0)),
                  pl.BlockSpec((1, d), lambda i, j: (0, 0)),
                  pl.BlockSpec((None, 1, d), lambda i, j: (i, 0, 0))],
        out_specs=pl.BlockSpec((None, l, d), lambda i, j: (i, 0, 0)),
        out_shape=jax.ShapeDtypeStruct((b, l, d), F32),
        scratch_shapes=[pltpu.VMEM((l + 2 * GRID_W, tf), F32),
                        pltpu.VMEM((l, tf), F32),
                        pltpu.VMEM((d, 2 * tf), BF16),
                        pltpu.VMEM((tf, d), BF16)],
        compiler_params=_params(58, ("arbitrary", "arbitrary")),
        name="ffn",
    )(h2, x1, w_up, w_up, dwc, w_down, g_post, gt)


def kernel(x, c, ctx, c_ctx, w_ada, b_ada, g_pre_mix, g_post_mix, g_pre_ffn, g_post_ffn,
           w_in, w_qkv_conv, a_log, dt_bias, g_gdn, w_fourier, w_out, w_up, w_dwc, w_down):
    b, l, d = x.shape
    assert w_ada.shape[0] == 1, "single-layer stack"
    assert b < 16 and l % MIX_TILE == 0 and ctx.shape[1] % ROW_BLOCK == 0
    assert w_in.shape[2] == N_GROUPS * GROUP_W + N_GATES and d == 2 * GROUP_W

    cc = jnp.zeros((16, d), F32).at[:b].set(c).at[b].set(c_ctx)
    mod = _ada(cc, w_ada[0], b_ada)
    sh1, sc1, gt1, sh2, sc2, gt2 = [m[:, None, :] for m in jnp.split(mod[:b], 6, axis=-1)]
    sh1c, sc1c = [m[None, :] for m in jnp.split(mod[b], 6, axis=-1)[:2]]

    w_in_t = jnp.swapaxes(w_in, 1, 2)[0]
    wabt = w_in_t[N_GROUPS * GROUP_W:]
    zeros8 = jnp.zeros((8,), F32)
    alog = jnp.concatenate([zeros8, a_log[0].reshape(-1)])[:, None]
    dtb = jnp.concatenate([zeros8, dt_bias[0].reshape(-1)])[:, None]

    pg, kt, gat = _inproj(x, ctx, g_pre_mix, sh1, sc1, sh1c, sc1c, w_in_t, wabt, w_qkv_conv[0], alog, dtb)
    on = _gdn(pg, kt, gat, g_gdn, l)

    ch, sh, c_c, s_c = _dft_constants(l, HEAD_D)
    mcs, ch, sh = _fourier_w(c_c, s_c, w_fourier[0], ch, sh)
    x1, h2 = _mix(pg, on, ch, sh, mcs, w_out[0], x, g_post_mix, gt1, g_pre_ffn, sh2, sc2)
    d_ff = w_down.shape[1]
    return _ffn(h2, x1, w_up[0], w_dwc[0].reshape(9, d_ff), w_down[0], g_post_ffn, gt2)
```

```python
import functools
import itertools
import math

import numpy as np
import jax
import jax.numpy as jnp
from jax import lax
from jax.experimental import pallas as pl
from jax.experimental.pallas import tpu as pltpu

F32 = jnp.float32
BF16 = jnp.bfloat16

GRID_W = 64
HEADS = 4
HEAD_D = 128
CHUNK = 64
PAIR = 2 * CHUNK
GROUP_W = HEADS * HEAD_D
N_GROUPS = 5
N_GATES = 16
EPS = 1e-6
ROW_BLOCK = 256
FFN_TILE = 256
MIX_TILE = 1024
FFN_ROWS = 512
GDN_STEPS = 3
INPROJ_ROWS = 512
MIB = 1024 * 1024

_NT = (((1,), (1,)), ((), ()))


def _dot(a, b):
    return jnp.dot(a, b, preferred_element_type=F32)


def _silu(x):
    return x / (1.0 + jnp.exp(-x))


def _rms(x):
    return x * lax.rsqrt(jnp.mean(x * x, axis=-1, keepdims=True) + EPS)


def _params(vmem_mib, semantics):
    return pltpu.CompilerParams(dimension_semantics=semantics,
                                vmem_limit_bytes=vmem_mib * MIB)


def _ada_kernel(c_ref, w_ref, b_ref, o_ref):
    s = _silu(c_ref[...]).astype(BF16)
    o_ref[...] = _dot(s, w_ref[...].astype(BF16)) + b_ref[...]


def _ada(cc, w, b):
    rows, d = cc.shape
    n = w.shape[1]
    tn = 1024
    return pl.pallas_call(
        _ada_kernel,
        grid=(n // tn,),
        in_specs=[pl.BlockSpec((rows, d), lambda j: (0, 0)),
                  pl.BlockSpec((d, tn), lambda j: (0, j)),
                  pl.BlockSpec((1, tn), lambda j: (0, j))],
        out_specs=pl.BlockSpec((rows, tn), lambda j: (0, j)),
        out_shape=jax.ShapeDtypeStruct((rows, n), F32),
        compiler_params=_params(32, ("arbitrary",)),
        name="ada",
    )(cc, w, b)


def _inproj_kernel(x_ref, ctx_ref, g_ref, sh_ref, sc_ref, shc_ref, scc_ref, w_ref, wab_ref,
                   conv_ref, alog_ref, dtb_ref, pg_ref, kt_ref, gat_ref, h_s, w_s, p_s):
    j = pl.program_id(1)
    l, lc = x_ref.shape[0], ctx_ref.shape[0]
    blocks = [(x_ref, r, r, r + 8, sh_ref, sc_ref, min(INPROJ_ROWS, l - r)) for r in range(0, l, INPROJ_ROWS)]
    blocks += [(ctx_ref, r, l + r, l + r + 16, shc_ref, scc_ref, min(INPROJ_ROWS, lc - r))
               for r in range(0, lc, INPROJ_ROWS)]
    n_rb = len(blocks)

    w_s[...] = w_ref[...].astype(BF16)

    def out_rows(rb):
        return slice(blocks[rb][2], blocks[rb][2] + blocks[rb][6])

    def project(rb):
        return lax.dot_general(h_s[out_rows(rb), :], w_s[...], _NT, preferred_element_type=F32)

    def pipelined(first, second, lead):
        for rb in range(min(lead, n_rb)):
            first(rb)
        for rb in range(n_rb):
            if rb + lead < n_rb:
                first(rb + lead)
            second(rb)

    @pl.when(j == 0)
    def _u():
        for r in (0, l + 8, l + lc + 16):
            p_s[r:r + 8, :] = jnp.zeros((8, GROUP_W), F32)
        wabt = wab_ref[...].astype(BF16)
        row = lax.broadcasted_iota(jnp.int32, (N_GATES, PAIR), 0)

        def prologue(rb):
            src, r, dst, _, shift, scale, n = blocks[rb]
            h = _rms(src[r:r + n, :]) * g_ref[...]
            hb = (h * (1.0 + scale[...]) + shift[...]).astype(BF16)
            h_s[dst:dst + n, :] = hb
            for s in range(n // PAIR):
                ab = lax.dot_general(wabt, hb[s * PAIR:(s + 1) * PAIR], _NT,
                                     preferred_element_type=F32)
                beta = 1.0 / (1.0 + jnp.exp(-ab))
                xs = ab + dtb_ref[...]
                softplus = jnp.maximum(xs, 0.0) + jnp.log1p(jnp.exp(-jnp.abs(xs)))
                gate = -jnp.exp(alog_ref[...]) * softplus
                gat_ref[dst // PAIR + s] = jnp.where(row < 8, beta, gate)

        def store_u(rb):
            pg_ref[out_rows(rb), :] = project(rb).astype(BF16)

        pipelined(prologue, store_u, 2)

    @pl.when(j == N_GROUPS - 1)
    def _z():
        for rb in range(n_rb):
            pg_ref[out_rows(rb), :] = _silu(project(rb)).astype(BF16)

    def conv_group(normalise, transpose):
        cw = conv_ref[...]

        def raw(rb):
            p0, n = blocks[rb][3], blocks[rb][6]
            p_s[p0:p0 + n, :] = project(rb)

        def finish(rb):
            p0, n = blocks[rb][3], blocks[rb][6]
            ext = p_s[p0 - 8:p0 + n + 8, :]
            prev = pltpu.roll(ext, 1, 0)[8:8 + n]
            nxt = pltpu.roll(ext, n + 15, 0)[8:8 + n]
            cur = ext[8:8 + n]
            a = _silu(prev * cw[0:1] + cur * cw[1:2] + nxt * cw[2:3])
            if normalise:
                segs = []
                for hh in range(HEADS):
                    seg = a[:, hh * HEAD_D:(hh + 1) * HEAD_D]
                    segs.append(seg * lax.rsqrt(jnp.sum(seg * seg, axis=-1, keepdims=True) + EPS))
                a = jnp.concatenate(segs, axis=1)
            pg_ref[out_rows(rb), :] = a.astype(BF16)
            if transpose:
                at = a.T.astype(BF16)
                for s in range(n // PAIR):
                    kt_ref[blocks[rb][2] // PAIR + s] = at[:, s * PAIR:(s + 1) * PAIR]

        pipelined(raw, finish, 2)

    @pl.when(j == 1)
    def _q():
        conv_group(True, False)

    @pl.when(j == 2)
    def _k():
        conv_group(True, True)

    @pl.when(j == 3)
    def _v():
        conv_group(False, False)


def _inproj(x, ctx, g, shift, scale, shift_c, scale_c, w_in_t, wabt, conv, alog, dtb):
    b, l, d = x.shape
    lt = l + ctx.shape[1]
    n_pairs = lt // PAIR
    vec = pl.BlockSpec((1, d), lambda i, j: (0, 0))
    mod = pl.BlockSpec((None, 1, d), lambda i, j: (i, 0, 0))
    return pl.pallas_call(
        _inproj_kernel,
        grid=(b, N_GROUPS),
        in_specs=[pl.BlockSpec((None, l, d), lambda i, j: (i, 0, 0)),
                  pl.BlockSpec((None, lt - l, d), lambda i, j: (i, 0, 0)),
                  vec, mod, mod, vec, vec,
                  pl.BlockSpec((GROUP_W, d), lambda i, j: (j, 0)),
                  pl.BlockSpec((N_GATES, d), lambda i, j: (0, 0)),
                  pl.BlockSpec((3, GROUP_W), lambda i, j: (0, jnp.clip(j - 1, 0, 2))),
                  pl.BlockSpec((N_GATES, 1), lambda i, j: (0, 0)),
                  pl.BlockSpec((N_GATES, 1), lambda i, j: (0, 0))],
        out_specs=[pl.BlockSpec((None, lt, GROUP_W), lambda i, j: (i, 0, j)),
                   pl.BlockSpec((None, n_pairs, GROUP_W, PAIR), lambda i, j: (i, 0, 0, 0)),
                   pl.BlockSpec((None, n_pairs, N_GATES, PAIR), lambda i, j: (i, 0, 0, 0))],
        out_shape=[jax.ShapeDtypeStruct((b, lt, N_GROUPS * GROUP_W), BF16),
                   jax.ShapeDtypeStruct((b, n_pairs, GROUP_W, PAIR), BF16),
                   jax.ShapeDtypeStruct((b, n_pairs, N_GATES, PAIR), F32)],
        scratch_shapes=[pltpu.VMEM((lt, d), BF16),
                        pltpu.VMEM((GROUP_W, d), BF16),
                        pltpu.VMEM((lt + 24, GROUP_W), F32)],
        compiler_params=_params(52, ("arbitrary", "arbitrary")),
        name="inproj",
    )(x, ctx, g, shift, scale, shift_c, scale_c, w_in_t, wabt, conv, alog, dtb)


def _bd2(a, b):
    z = jnp.zeros_like(a)
    return jnp.concatenate([jnp.concatenate([a, z], axis=1),
                            jnp.concatenate([z, b], axis=1)], axis=0)


def _bd_of(x):
    xb = x.astype(BF16)
    return _bd2(xb[:, :HEAD_D], xb[:, HEAD_D:])


def _round_robin(*stage_generators):
    for _ in itertools.zip_longest(*stage_generators):
        pass


def _gdn_kernel(q_ref, k_ref, v_ref, kt_ref, gat_ref, g_ref, y_ref,
                u_s, wq_s, at_s, kdt_s, egl_s, s_s, o_s):
    lt, l = q_ref.shape[0], y_ref.shape[0]
    n_steps = lt // PAIR
    n_ctx = (lt - l) // PAIR
    per_iter = GDN_STEPS
    assert n_steps % per_iter == 0

    ri = lax.broadcasted_iota(jnp.int32, (PAIR, PAIR), 0)
    ci = lax.broadcasted_iota(jnp.int32, (PAIR, PAIR), 1)
    same = (ri // CHUNK) == (ci // CHUNK)
    incl = (jnp.where(same & (ri >= ci), 1.0, 0.0), jnp.where(same & (ri <= ci), 1.0, 0.0))
    strict = (jnp.where(same & (ri > ci), 1.0, 0.0), jnp.where(same & (ri < ci), 1.0, 0.0))
    tri = (incl[1], incl[0])
    eye = jnp.where(ri == ci, 1.0, 0.0)
    eye2 = jnp.concatenate([eye, eye], axis=1)
    off = ([], [])
    for lvl in range(6):
        b = 2 ** lvl
        joined = (ri // (2 * b)) == (ci // (2 * b))
        off[0].append(jnp.where(joined & (ri % (2 * b) >= b) & (ci % (2 * b) < b), 1.0, 0.0))
        off[1].append(jnp.where(joined & (ri % (2 * b) < b) & (ci % (2 * b) >= b), 1.0, 0.0))

    def joining_blocks(a, d, lvl):
        m = off[d][lvl]
        return jnp.concatenate([a[:, :HEAD_D] * m, a[:, HEAD_D:] * m], axis=1)

    lane = lax.broadcasted_iota(jnp.int32, (1, PAIR), 1)
    first_half = lane < CHUNK
    scale = HEAD_D ** -0.5
    hi = lax.Precision.HIGHEST
    lanes = [slice(hp * 2 * HEAD_D, (hp + 1) * 2 * HEAD_D) for hp in range(HEADS // 2)]
    stream_ids = [(hp, d) for hp in range(HEADS // 2) for d in range(2)]
    zeros_half = jnp.zeros((CHUNK, 2 * HEAD_D), BF16)

    def pair_of(step, d):
        if d == 0:
            return lax.rem(step + (n_steps - n_ctx), n_steps)
        return n_steps - 1 - step

    def prepare(slot, step0):
        streams = []
        for kk in range(per_iter):
            for d in range(2):
                p = pair_of(step0 + kk, d)
                t0 = pl.multiple_of(p * PAIR, PAIR)
                gates = gat_ref[p]
                kt_all = kt_ref[p]
                gc = jnp.dot(gates, tri[d], precision=hi, preferred_element_type=F32)
                for hp in range(HEADS // 2):
                    k2 = k_ref[pl.ds(t0, PAIR), lanes[hp]]
                    q2 = q_ref[pl.ds(t0, PAIR), lanes[hp]]
                    v2 = v_ref[pl.ds(t0, PAIR), lanes[hp]]
                    kt2 = kt_all[hp * 2 * HEAD_D:(hp + 1) * 2 * HEAD_D, :]
                    bd_kt = _bd2(kt2[:HEAD_D], kt2[HEAD_D:])
                    streams.append(dict(kk=kk, d=d, hp=hp, gates=gates, gc=gc, k2=k2, q2=q2, v2=v2,
                                        kt2=kt2, kk_=_dot(k2, bd_kt), qk_=_dot(q2, bd_kt)))
        yield
        for st in streams:
            kk, d, hp, gates, gc = st["kk"], st["d"], st["hp"], st["gates"], st["gc"]
            gc_t = gc.T
            a, attn, qd, kdt, erow = [], [], [], [], []
            for hh in range(2):
                h = 2 * hp + hh
                hs = slice(hh * HEAD_D, (hh + 1) * HEAD_D)
                brow = gates[4 * d + h:4 * d + h + 1, :]
                gcr = gc[8 + 4 * d + h:9 + 4 * d + h, :]
                gcc = gc_t[:, 8 + 4 * d + h:9 + 4 * d + h]
                decay = jnp.exp((gcc - gcr) * incl[d]) * incl[d]
                a.append(st["kk_"][:, hs] * decay * strict[d] * brow)
                attn.append(st["qk_"][:, hs] * decay * (brow * scale))
                qd.append(st["q2"][:, hs].astype(F32) * (jnp.exp(gcc) * scale))
                if d == 0:
                    gl0, gl1 = gcr[:, CHUNK - 1:CHUNK], gcr[:, PAIR - 1:PAIR]
                else:
                    gl0, gl1 = gcr[:, 0:1], gcr[:, CHUNK:CHUNK + 1]
                glr = jnp.where(first_half, gl0, gl1)
                kdt.append(st["kt2"][hs, :].astype(F32) * (jnp.exp(glr - gcr) * brow))
                erow.append(jnp.exp(gcr))
                egl_s[slot, d, kk, 0, h:h + 1, :] = jnp.broadcast_to(jnp.exp(gl0), (1, PAIR))
                egl_s[slot, d, kk, 1, h:h + 1, :] = jnp.broadcast_to(jnp.exp(gl1), (1, PAIR))
            st["a"] = jnp.concatenate(a, axis=1)
            st["erow"] = jnp.concatenate(erow, axis=1)
            at_s[slot, d, kk, :, lanes[hp]] = jnp.concatenate(attn, axis=1).astype(BF16)
            kdt_s[slot, d, kk, hp] = jnp.concatenate(kdt, axis=1).astype(BF16)
            st["qd"] = jnp.concatenate(qd, axis=1).astype(BF16)
            st["t"] = eye2 - joining_blocks(st["a"], d, 0)
        yield
        for lvl in range(1, 6):
            for st in streams:
                st["y"] = _dot(joining_blocks(st["a"], st["d"], lvl).astype(BF16), _bd_of(st["t"]))
            yield
            for st in streams:
                st["t"] = st["t"] - _dot(st["t"].astype(BF16), _bd_of(st["y"]))
            yield
        for st in streams:
            t = st["t"]
            st["u"] = _dot(t.astype(BF16), _bd2(st["v2"][:, :HEAD_D], st["v2"][:, HEAD_D:]))
            st["w"] = _dot((t * st["erow"]).astype(BF16), _bd2(st["k2"][:, :HEAD_D], st["k2"][:, HEAD_D:]))
        yield
        for st in streams:
            kk, d, hp = st["kk"], st["d"], st["hp"]
            u_s[slot, d, kk, :, lanes[hp]] = st["u"].astype(BF16)
            wb, qdb = st["w"].astype(BF16), st["qd"]
            wq_s[slot, d, kk, :, lanes[hp]] = jnp.concatenate(
                [wb[:CHUNK], qdb[:CHUNK], wb[CHUNK:], qdb[CHUNK:]], axis=0)
        yield

    def scan(slot, step0):
        state = {(hp, d): s_s[d, :, lanes[hp]] for (hp, d) in stream_ids}
        for kk in range(per_iter):
            for half in range(2):
                e = {0: half, 1: 1 - half}
                res, bd_x = {}, {}
                for (hp, d) in stream_ids:
                    res[(hp, d)] = _dot(wq_s[slot, d, kk, e[d] * PAIR:(e[d] + 1) * PAIR, lanes[hp]],
                                        _bd_of(state[(hp, d)]))
                yield
                for (hp, d) in stream_ids:
                    vn = (u_s[slot, d, kk, e[d] * CHUNK:(e[d] + 1) * CHUNK, lanes[hp]].astype(F32)
                          - res[(hp, d)][:CHUNK])
                    vnb = vn.astype(BF16)
                    x = jnp.concatenate([vnb, zeros_half] if e[d] == 0 else [zeros_half, vnb], axis=0)
                    bd_x[(hp, d)] = _bd2(x[:, :HEAD_D], x[:, HEAD_D:])
                for (hp, d) in stream_ids:
                    both = _dot(jnp.concatenate(
                        [kdt_s[slot, d, kk, hp],
                         at_s[slot, d, kk, e[d] * CHUNK:(e[d] + 1) * CHUNK, lanes[hp]]], axis=0),
                        bd_x[(hp, d)])
                    eg = jnp.concatenate([egl_s[slot, d, kk, e[d], 2 * hp:2 * hp + 1, :],
                                          egl_s[slot, d, kk, e[d], 2 * hp + 1:2 * hp + 2, :]], axis=1)
                    state[(hp, d)] = state[(hp, d)] * eg + both[:HEAD_D]
                    r0 = pl.multiple_of(pair_of(step0 + kk, d) * PAIR + e[d] * CHUNK, CHUNK)
                    o_s[pl.ds(r0, CHUNK), lanes[hp]] += res[(hp, d)][CHUNK:] + both[HEAD_D:]
                yield
        for (hp, d) in stream_ids:
            s_s[d, :, lanes[hp]] = state[(hp, d)]

    s_s[...] = jnp.zeros(s_s.shape, F32)
    o_s[...] = jnp.zeros(o_s.shape, F32)
    n_iter = n_steps // per_iter
    _round_robin(prepare(0, 0))

    def body(it, carry):
        _round_robin(prepare(lax.rem(it + 1, 2), (it + 1) * per_iter), scan(lax.rem(it, 2), it * per_iter))
        return carry

    lax.fori_loop(0, n_iter - 1, body, 0)
    _round_robin(scan((n_iter - 1) % 2, (n_iter - 1) * per_iter))

    def epilogue(rb, carry):
        r0 = pl.multiple_of(rb * ROW_BLOCK, ROW_BLOCK)
        o = o_s[pl.ds(r0, ROW_BLOCK), :]
        segs = [_rms(o[:, h * HEAD_D:(h + 1) * HEAD_D]) * g_ref[...] for h in range(HEADS)]
        y_ref[pl.ds(r0, ROW_BLOCK), :] = jnp.concatenate(segs, axis=1).astype(BF16)
        return carry

    lax.fori_loop(0, l // ROW_BLOCK, epilogue, 0)


def _gdn(pg, kt, gat, g_gdn, l):
    b, lt, _ = pg.shape
    n_pairs = lt // PAIR

    def group(g):
        return pl.BlockSpec((None, lt, GROUP_W), lambda i, g=g: (i, 0, g))

    ring = (2, 2, GDN_STEPS)
    return pl.pallas_call(
        _gdn_kernel,
        grid=(b,),
        in_specs=[group(1), group(2), group(3),
                  pl.BlockSpec((None, n_pairs, GROUP_W, PAIR), lambda i: (i, 0, 0, 0)),
                  pl.BlockSpec((None, n_pairs, N_GATES, PAIR), lambda i: (i, 0, 0, 0)),
                  pl.BlockSpec((1, HEAD_D), lambda i: (0, 0))],
        out_specs=pl.BlockSpec((None, l, GROUP_W), lambda i: (i, 0, 0)),
        out_shape=jax.ShapeDtypeStruct((b, l, GROUP_W), BF16),
        scratch_shapes=[pltpu.VMEM(ring + (PAIR, GROUP_W), BF16),
                        pltpu.VMEM(ring + (2 * PAIR, GROUP_W), BF16),
                        pltpu.VMEM(ring + (PAIR, GROUP_W), BF16),
                        pltpu.VMEM(ring + (HEADS // 2, HEAD_D, 2 * PAIR), BF16),
                        pltpu.VMEM(ring + (2, 8, PAIR), F32),
                        pltpu.VMEM((2, HEAD_D, GROUP_W), F32),
                        pltpu.VMEM((lt, GROUP_W), F32)],
        compiler_params=_params(48, ("arbitrary",)),
        name="gdn",
    )(pg, pg, pg, kt, gat, g_gdn)


def _fourier_w_kernel(cc_ref, sc_ref, w_ref, ch_ref, sh_ref, o_ref, chb_ref, shb_ref):
    hi = lax.Precision.HIGHEST
    for g in range(w_ref.shape[0]):
        w = w_ref[g]
        o_ref[g] = jnp.concatenate(
            [jnp.dot(cc_ref[...], w, precision=hi, preferred_element_type=F32),
             -jnp.dot(sc_ref[...], w, precision=hi, preferred_element_type=F32)], axis=1)
    chb_ref[...] = ch_ref[...].astype(BF16)
    shb_ref[...] = sh_ref[...].astype(BF16)


def _fourier_w(cc, sc, w_f, ch, sh):
    groups, c, d = w_f.shape
    return pl.pallas_call(
        _fourier_w_kernel,
        out_shape=[jax.ShapeDtypeStruct((groups, c, 2 * d), F32),
                   jax.ShapeDtypeStruct(ch.shape, BF16),
                   jax.ShapeDtypeStruct(sh.shape, BF16)],
        compiler_params=pltpu.CompilerParams(vmem_limit_bytes=32 * MIB),
        name="fourier_w",
    )(cc, sc, w_f, ch, sh)


@functools.lru_cache(maxsize=None)
def _dft_constants(length, channels):
    def cos_sin(n, keep):
        idx = np.arange(keep, dtype=np.int64)
        ang = (np.outer(idx, idx) % n).astype(np.float64) * (2.0 * np.pi / n)
        return np.cos(ang), np.sin(ang)
    c_l, s_l = cos_sin(length, length // 2)
    c_c, s_c = cos_sin(channels, channels)
    norm = 1.0 / math.sqrt(length * channels)
    return (jnp.asarray(c_l.astype(np.float32)), jnp.asarray(s_l.astype(np.float32)),
            jnp.asarray((c_c * norm).astype(np.float32)),
            jnp.asarray((s_c * norm).astype(np.float32)))


def _mix_kernel(u_ref, on_ref, z_ref, ch_ref, sh_ref, mcs_ref, wout_ref, x_ref, gpost_ref, gt_ref,
                gpre_ref, shift_ref, scale_ref, x1_ref, h2_ref,
                yf_s, mc_s, ms_s, wout_s, ve_s, vo_s, d_s):
    i, m = pl.program_id(0), pl.program_id(1)
    l = u_ref.shape[0]
    h = l // 2
    blk = PAIR
    n_blk = h // blk

    @pl.when((i == 0) & (m == 0))
    def _():
        wout_s[...] = wout_ref[...].astype(BF16)
        mc_s[...] = jnp.zeros(mc_s.shape, BF16)
        ms_s[...] = jnp.zeros(ms_s.shape, BF16)
        for g in range(HEADS):
            gs = slice(g * HEAD_D, (g + 1) * HEAD_D)
            mc_s[gs, gs] = mcs_ref[g][:, :HEAD_D].astype(BF16)
            ms_s[gs, gs] = mcs_ref[g][:, HEAD_D:].astype(BF16)

    @pl.when(m == 0)
    def _():
        ri = lax.broadcasted_iota(jnp.int32, (blk, 2 * blk), 0)
        ci = lax.broadcasted_iota(jnp.int32, (blk, 2 * blk), 1)
        rev = jnp.where(((ci < blk) & (ri + ci == blk)) | ((ri == 0) & (ci == blk)), 1.0, 0.0).astype(BF16)
        zero_blk = jnp.zeros((blk, GROUP_W), BF16)

        def reversed_block(ref, base, k, last):
            top = ref[base + (n_blk - 1 - k) * blk:base + (n_blk - k) * blk, :]
            bottom = last if k == 0 else ref[base + (n_blk - k) * blk:base + (n_blk - k + 1) * blk, :]
            return _dot(rev, jnp.concatenate([top, bottom], axis=0))

        for k in range(n_blk):
            low = u_ref[k * blk:(k + 1) * blk, :].astype(F32)
            mirrored = reversed_block(u_ref, h, k, zero_blk)
            yf_s[k * blk:(k + 1) * blk, :] = (low + mirrored).astype(BF16)
            yf_s[h + k * blk:h + (k + 1) * blk, :] = (low - mirrored).astype(BF16)
        ve_s[...] = _dot(yf_s[:h, :], mc_s[...]).astype(BF16)
        vo_s[...] = _dot(yf_s[h:, :], ms_s[...]).astype(BF16)
        sign_row = jnp.where(lax.broadcasted_iota(jnp.int32, (16, l), 1) % 2 == 0, 1.0, -1.0)
        sign_row = jnp.where(lax.broadcasted_iota(jnp.int32, (16, l), 0) == 0, sign_row, 0.0).astype(BF16)
        alt_u = _dot(sign_row, u_ref[...])
        y_nyq = _dot(alt_u.astype(BF16), mc_s[...])[0:1, :]
        v_nyq = _dot(u_ref[h:h + 16, :], mc_s[...])[0:1, :]
        sign_col = jnp.where(lax.broadcasted_iota(jnp.int32, (blk, GROUP_W), 0) % 2 == 0, 1.0, -1.0)
        first_row = lax.broadcasted_iota(jnp.int32, (blk, GROUP_W), 0) == 0
        d_s[h:h + blk, :] = jnp.where(first_row, y_nyq, 0.0).astype(BF16)
        nyq = sign_col * v_nyq
        tall = min(FFN_ROWS, h)
        nyq = jnp.concatenate([nyq] * (tall // blk), axis=0)
        for k in range(h // tall):
            rows = slice(k * tall, (k + 1) * tall)
            a = _dot(ch_ref[rows, :], ve_s[...]) + nyq
            b = _dot(sh_ref[rows, :], vo_s[...])
            yf_s[rows, :] = (a + b).astype(BF16)
            d_s[rows, :] = (a - b).astype(BF16)
        for k in range(n_blk):
            yf_s[h + k * blk:h + (k + 1) * blk, :] = reversed_block(
                d_s, 0, k, d_s[h:h + blk, :]).astype(BF16)

    tm = x_ref.shape[0]
    sub = min(FFN_ROWS, tm)
    subs = [slice(sb * sub, (sb + 1) * sub) for sb in range(tm // sub)]
    ys = []
    for sb, rs in enumerate(subs):
        r0 = pl.multiple_of(m * tm + sb * sub, sub)
        yd = (on_ref[rs, :].astype(F32) * z_ref[rs, :].astype(F32)).astype(BF16)
        ys.append(_dot(jnp.concatenate([yf_s[pl.ds(r0, sub), :], yd], axis=1), wout_s[...]))
    for rs, y in zip(subs, ys):
        x1 = x_ref[rs, :] + gt_ref[...] * (_rms(y) * gpost_ref[...])
        x1_ref[rs, :] = x1
        h2 = _rms(x1) * gpre_ref[...]
        h2_ref[rs, :] = (h2 * (1.0 + scale_ref[...]) + shift_ref[...]).astype(BF16)


def _mix(pg, on, ch, sh, mcs, w_out, x, g_post, gt, g_pre, shift, scale):
    b, l, d = x.shape
    tm = MIX_TILE
    h = l // 2
    vec = pl.BlockSpec((1, d), lambda i, m: (0, 0))
    mod = pl.BlockSpec((None, 1, d), lambda i, m: (i, 0, 0))
    whole = lambda a: pl.BlockSpec(a.shape, lambda i, m: (0,) * a.ndim)
    return pl.pallas_call(
        _mix_kernel,
        grid=(b, l // tm),
        in_specs=[pl.BlockSpec((None, l, GROUP_W), lambda i, m: (i, 0, 0)),
                  pl.BlockSpec((None, tm, GROUP_W), lambda i, m: (i, m, 0)),
                  pl.BlockSpec((None, tm, GROUP_W), lambda i, m: (i, m, N_GROUPS - 1)),
                  whole(ch), whole(sh), whole(mcs), whole(w_out),
                  pl.BlockSpec((None, tm, d), lambda i, m: (i, m, 0)),
                  vec, mod, vec, mod, mod],
        out_specs=[pl.BlockSpec((None, tm, d), lambda i, m: (i, m, 0)),
                   pl.BlockSpec((None, tm, d), lambda i, m: (i, m, 0))],
        out_shape=[jax.ShapeDtypeStruct((b, l, d), F32),
                   jax.ShapeDtypeStruct((b, l, d), BF16)],
        scratch_shapes=[pltpu.VMEM((l, GROUP_W), BF16),
                        pltpu.VMEM((GROUP_W, GROUP_W), BF16),
                        pltpu.VMEM((GROUP_W, GROUP_W), BF16),
                        pltpu.VMEM(w_out.shape, BF16),
                        pltpu.VMEM((h, GROUP_W), BF16),
                        pltpu.VMEM((h, GROUP_W), BF16),
                        pltpu.VMEM((h + PAIR, GROUP_W), BF16)],
        compiler_params=_params(48, ("arbitrary", "arbitrary")),
        name="mix",
    )(pg, on, pg, ch, sh, mcs, w_out, x, g_post, gt, g_pre, shift, scale)


def _ffn_kernel(h_ref, x1_ref, wv_ref, wg_ref, dwc_ref, wd_ref, gpost_ref, gt_ref,
                o_ref, gate_s, val_s, wvg_s, wd_s):
    ROW_BLOCK = FFN_ROWS
    j = pl.program_id(1)
    l = h_ref.shape[0]
    tf = wv_ref.shape[1]
    n_rb = l // ROW_BLOCK
    pad = GRID_W

    wvg_s[:, :tf] = wv_ref[...].astype(BF16)
    wvg_s[:, tf:] = wg_ref[...].astype(BF16)
    wd_s[...] = wd_ref[...].astype(BF16)
    gate_s[0:pad, :] = jnp.zeros((pad, tf), F32)
    gate_s[l + pad:l + 2 * pad, :] = jnp.zeros((pad, tf), F32)

    @pl.when(j == 0)
    def _():
        o_ref[...] = jnp.zeros(o_ref.shape, F32)

    n_ext = ROW_BLOCK + 2 * pad
    col = lax.broadcasted_iota(jnp.int32, (n_ext, tf), 0) % GRID_W
    not_first = jnp.where(col == 0, 0.0, 1.0)
    not_last = jnp.where(col == GRID_W - 1, 0.0, 1.0)
    dw = dwc_ref[...]

    def up(i):
        r0 = i * ROW_BLOCK
        vg = _dot(h_ref[r0:r0 + ROW_BLOCK, :], wvg_s[...])
        val_s[r0:r0 + ROW_BLOCK, :] = vg[:, :tf]
        gate_s[r0 + pad:r0 + pad + ROW_BLOCK, :] = vg[:, tf:]

    def down(i):
        r0 = i * ROW_BLOCK
        ext = gate_s[r0:r0 + n_ext, :]
        shifted = (pltpu.roll(ext, 1, 0) * not_first, ext, pltpu.roll(ext, n_ext - 1, 0) * not_last)
        conv = None
        for dy in range(3):
            for dx in range(3):
                term = shifted[dx][dy * pad:dy * pad + ROW_BLOCK] * dw[3 * dy + dx:3 * dy + dx + 1]
                conv = term if conv is None else conv + term
        act = (_silu(conv) * val_s[r0:r0 + ROW_BLOCK, :]).astype(BF16)
        o_ref[r0:r0 + ROW_BLOCK, :] += _dot(act, wd_s[...])

    for i in range(min(2, n_rb)):
        up(i)
    for i in range(n_rb):
        if i + 2 < n_rb:
            up(i + 2)
        down(i)

    @pl.when(j == pl.num_programs(1) - 1)
    def _():
        def fin(rb, carry):
            r0 = pl.multiple_of(rb * ROW_BLOCK, ROW_BLOCK)
            y = _rms(o_ref[pl.ds(r0, ROW_BLOCK), :]) * gpost_ref[...]
            o_ref[pl.ds(r0, ROW_BLOCK), :] = x1_ref[pl.ds(r0, ROW_BLOCK), :] + gt_ref[...] * y
            return carry
        lax.fori_loop(0, n_rb, fin, 0)


def _ffn(h2, x1, w_up, dwc, w_down, g_post, gt):
    b, l, d = x1.shape
    d_ff = w_down.shape[0]
    tf = FFN_TILE
    n_f = d_ff // tf
    return pl.pallas_call(
        _ffn_kernel,
        grid=(b, n_f),
        in_specs=[pl.BlockSpec((None, l, d), lambda i, j: (i, 0, 0)),
                  pl.BlockSpec((None, l, d), lambda i, j: (i, 0, 0)),
                  pl.BlockSpec((d, tf), lambda i, j: (0, j)),
                  pl.BlockSpec((d, tf), lambda i, j: (0, n_f + j)),
                  pl.BlockSpec((9, tf), lambda i, j: (0, j)),
                  pl.BlockSpec((tf, d), lambda i, j: (j, 0)),
                  pl.BlockSpec((1, d), lambda i, j: (0, 0)),
                  pl.BlockSpec((None, 1, d), lambda i, j: (i, 0, 0))],
        out_specs=pl.BlockSpec((None, l, d), lambda i, j: (i, 0, 0)),
        out_shape=jax.ShapeDtypeStruct((b, l, d), F32),
        scratch_shapes=[pltpu.VMEM((l + 2 * GRID_W, tf), F32),
                        pltpu.VMEM((l, tf), F32),
                        pltpu.VMEM((d, 2 * tf), BF16),
                        pltpu.VMEM((tf, d), BF16)],
        compiler_params=_params(58, ("arbitrary", "arbitrary")),
        name="ffn",
    )(h2, x1, w_up, w_up, dwc, w_down, g_post, gt)


def kernel(x, c, ctx, c_ctx, w_ada, b_ada, g_pre_mix, g_post_mix, g_pre_ffn, g_post_ffn,
           w_in, w_qkv_conv, a_log, dt_bias, g_gdn, w_fourier, w_out, w_up, w_dwc, w_down):
    b, l, d = x.shape
    assert w_ada.shape[0] == 1, "single-layer stack"
    assert b < 16 and l % MIX_TILE == 0 and ctx.shape[1] % ROW_BLOCK == 0
    assert w_in.shape[2] == N_GROUPS * GROUP_W + N_GATES and d == 2 * GROUP_W

    cc = jnp.zeros((16, d), F32).at[:b].set(c).at[b].set(c_ctx)
    mod = _ada(cc, w_ada[0], b_ada)
    sh1, sc1, gt1, sh2, sc2, gt2 = [m[:, None, :] for m in jnp.split(mod[:b], 6, axis=-1)]
    sh1c, sc1c = [m[None, :] for m in jnp.split(mod[b], 6, axis=-1)[:2]]

    w_in_t = jnp.swapaxes(w_in, 1, 2)[0]
    wabt = w_in_t[N_GROUPS * GROUP_W:]
    zeros8 = jnp.zeros((8,), F32)
    alog = jnp.concatenate([zeros8, a_log[0].reshape(-1)])[:, None]
    dtb = jnp.concatenate([zeros8, dt_bias[0].reshape(-1)])[:, None]

    pg, kt, gat = _inproj(x, ctx, g_pre_mix, sh1, sc1, sh1c, sc1c, w_in_t, wabt, w_qkv_conv[0], alog, dtb)
    on = _gdn(pg, kt, gat, g_gdn, l)

    ch, sh, c_c, s_c = _dft_constants(l, HEAD_D)
    mcs, ch, sh = _fourier_w(c_c, s_c, w_fourier[0], ch, sh)
    x1, h2 = _mix(pg, on, ch, sh, mcs, w_out[0], x, g_post_mix, gt1, g_pre_ffn, sh2, sc2)
    d_ff = w_down.shape[1]
    return _ffn(h2, x1, w_up[0], w_dwc[0].reshape(9, d_ff), w_down[0], g_post_ffn, gt2)
```

```python
import functools
import itertools
import math

import numpy as np
import jax
import jax.numpy as jnp
from jax import lax
from jax.experimental import pallas as pl
from jax.experimental.pallas import tpu as pltpu

F32 = jnp.float32
BF16 = jnp.bfloat16

GRID_W = 64
HEADS = 4
HEAD_D = 128
CHUNK = 64
PAIR = 2 * CHUNK
GROUP_W = HEADS * HEAD_D
N_GROUPS = 5
N_GATES = 16
EPS = 1e-6
NEG_LOG2_E = -math.log2(math.e)
ROW_BLOCK = 256
FFN_TILE = 256
MIX_TILE = 1024
FFN_ROWS = 512
GDN_STEPS = 3
INPROJ_ROWS = 512
MIB = 1024 * 1024

_NT = (((1,), (1,)), ((), ()))


def _dot(a, b):
    return jnp.dot(a, b, preferred_element_type=F32)


def _silu(x):
    return x / (1.0 + jnp.exp2(x * NEG_LOG2_E))


def _rms(x):
    return x * lax.rsqrt(jnp.mean(x * x, axis=-1, keepdims=True) + EPS)


def _params(vmem_mib, semantics):
    return pltpu.CompilerParams(dimension_semantics=semantics,
                                vmem_limit_bytes=vmem_mib * MIB)


def _ada_kernel(c_ref, w_ref, b_ref, o_ref):
    s = _silu(c_ref[...]).astype(BF16)
    o_ref[...] = _dot(s, w_ref[...].astype(BF16)) + b_ref[...]


def _ada(cc, w, b):
    rows, d = cc.shape
    n = w.shape[1]
    tn = 1024
    return pl.pallas_call(
        _ada_kernel,
        grid=(n // tn,),
        in_specs=[pl.BlockSpec((rows, d), lambda j: (0, 0)),
                  pl.BlockSpec((d, tn), lambda j: (0, j)),
                  pl.BlockSpec((1, tn), lambda j: (0, j))],
        out_specs=pl.BlockSpec((rows, tn), lambda j: (0, j)),
        out_shape=jax.ShapeDtypeStruct((rows, n), F32),
        compiler_params=_params(32, ("arbitrary",)),
        name="ada",
    )(cc, w, b)


def _inproj_kernel(x_ref, ctx_ref, g_ref, sh_ref, sc_ref, shc_ref, scc_ref, w_ref, wab_ref,
                   conv_ref, alog_ref, dtb_ref, pg_ref, kt_ref, gat_ref, h_s, w_s, p_s):
    j = pl.program_id(1)
    l, lc = x_ref.shape[0], ctx_ref.shape[0]
    blocks = [(x_ref, r, r, r + 8, sh_ref, sc_ref, min(INPROJ_ROWS, l - r)) for r in range(0, l, INPROJ_ROWS)]
    blocks += [(ctx_ref, r, l + r, l + r + 16, shc_ref, scc_ref, min(INPROJ_ROWS, lc - r))
               for r in range(0, lc, INPROJ_ROWS)]
    n_rb = len(blocks)

    w_s[...] = w_ref[...].astype(BF16)

    def out_rows(rb):
        return slice(blocks[rb][2], blocks[rb][2] + blocks[rb][6])

    def project(rb):
        return lax.dot_general(h_s[out_rows(rb), :], w_s[...], _NT, preferred_element_type=F32)

    def pipelined(first, second, lead):
        for rb in range(min(lead, n_rb)):
            first(rb)
        for rb in range(n_rb):
            if rb + lead < n_rb:
                first(rb + lead)
            second(rb)

    @pl.when(j == 0)
    def _u():
        for r in (0, l + 8, l + lc + 16):
            p_s[r:r + 8, :] = jnp.zeros((8, GROUP_W), F32)
        wabt = wab_ref[...].astype(BF16)
        row = lax.broadcasted_iota(jnp.int32, (N_GATES, PAIR), 0)

        def prologue(rb):
            src, r, dst, _, shift, scale, n = blocks[rb]
            gain = g_ref[...] * (1.0 + scale[...])
            hb = (_rms(src[r:r + n, :]) * gain + shift[...]).astype(BF16)
            h_s[dst:dst + n, :] = hb
            for s in range(n // PAIR):
                ab = lax.dot_general(wabt, hb[s * PAIR:(s + 1) * PAIR], _NT,
                                     preferred_element_type=F32)
                beta = 1.0 / (1.0 + jnp.exp(-ab))
                xs = ab + dtb_ref[...]
                softplus = jnp.maximum(xs, 0.0) + jnp.log1p(jnp.exp(-jnp.abs(xs)))
                gate = -jnp.exp(alog_ref[...]) * softplus
                gat_ref[dst // PAIR + s] = jnp.where(row < 8, beta, gate)

        def store_u(rb):
            pg_ref[out_rows(rb), :] = project(rb).astype(BF16)

        pipelined(prologue, store_u, 2)

    @pl.when(j == N_GROUPS - 1)
    def _z():
        for rb in range(n_rb):
            pg_ref[out_rows(rb), :] = _silu(project(rb)).astype(BF16)

    def conv_group(normalise, transpose):
        cw = conv_ref[...]

        def raw(rb):
            p0, n = blocks[rb][3], blocks[rb][6]
            p_s[p0:p0 + n, :] = project(rb)

        def finish(rb):
            p0, n = blocks[rb][3], blocks[rb][6]
            ext = p_s[p0 - 8:p0 + n + 8, :]
            prev = pltpu.roll(ext, 1, 0)[8:8 + n]
            nxt = pltpu.roll(ext, n + 15, 0)[8:8 + n]
            cur = ext[8:8 + n]
            a = _silu(prev * cw[0:1] + cur * cw[1:2] + nxt * cw[2:3])
            if normalise:
                segs = []
                for hh in range(HEADS):
                    seg = a[:, hh * HEAD_D:(hh + 1) * HEAD_D]
                    segs.append(seg * lax.rsqrt(jnp.sum(seg * seg, axis=-1, keepdims=True) + EPS))
                a = jnp.concatenate(segs, axis=1)
            pg_ref[out_rows(rb), :] = a.astype(BF16)
            if transpose:
                at = a.T.astype(BF16)
                for s in range(n // PAIR):
                    kt_ref[blocks[rb][2] // PAIR + s] = at[:, s * PAIR:(s + 1) * PAIR]

        pipelined(raw, finish, 2)

    @pl.when(j == 1)
    def _q():
        conv_group(True, False)

    @pl.when(j == 2)
    def _k():
        conv_group(True, True)

    @pl.when(j == 3)
    def _v():
        conv_group(False, False)


def _inproj(x, ctx, g, shift, scale, shift_c, scale_c, w_in_t, wabt, conv, alog, dtb):
    b, l, d = x.shape
    lt = l + ctx.shape[1]
    n_pairs = lt // PAIR
    vec = pl.BlockSpec((1, d), lambda i, j: (0, 0))
    mod = pl.BlockSpec((None, 1, d), lambda i, j: (i, 0, 0))
    return pl.pallas_call(
        _inproj_kernel,
        grid=(b, N_GROUPS),
        in_specs=[pl.BlockSpec((None, l, d), lambda i, j: (i, 0, 0)),
                  pl.BlockSpec((None, lt - l, d), lambda i, j: (i, 0, 0)),
                  vec, mod, mod, vec, vec,
                  pl.BlockSpec((GROUP_W, d), lambda i, j: (j, 0)),
                  pl.BlockSpec((N_GATES, d), lambda i, j: (0, 0)),
                  pl.BlockSpec((3, GROUP_W), lambda i, j: (0, jnp.clip(j - 1, 0, 2))),
                  pl.BlockSpec((N_GATES, 1), lambda i, j: (0, 0)),
                  pl.BlockSpec((N_GATES, 1), lambda i, j: (0, 0))],
        out_specs=[pl.BlockSpec((None, lt, GROUP_W), lambda i, j: (i, 0, j)),
                   pl.BlockSpec((None, n_pairs, GROUP_W, PAIR), lambda i, j: (i, 0, 0, 0)),
                   pl.BlockSpec((None, n_pairs, N_GATES, PAIR), lambda i, j: (i, 0, 0, 0))],
        out_shape=[jax.ShapeDtypeStruct((b, lt, N_GROUPS * GROUP_W), BF16),
                   jax.ShapeDtypeStruct((b, n_pairs, GROUP_W, PAIR), BF16),
                   jax.ShapeDtypeStruct((b, n_pairs, N_GATES, PAIR), F32)],
        scratch_shapes=[pltpu.VMEM((lt, d), BF16),
                        pltpu.VMEM((GROUP_W, d), BF16),
                        pltpu.VMEM((lt + 24, GROUP_W), F32)],
        compiler_params=_params(52, ("arbitrary", "arbitrary")),
        name="inproj",
    )(x, ctx, g, shift, scale, shift_c, scale_c, w_in_t, wabt, conv, alog, dtb)


def _bd2(a, b):
    z = jnp.zeros_like(a)
    return jnp.concatenate([jnp.concatenate([a, z], axis=1),
                            jnp.concatenate([z, b], axis=1)], axis=0)


def _bd_of(x):
    xb = x.astype(BF16)
    return _bd2(xb[:, :HEAD_D], xb[:, HEAD_D:])


def _round_robin(*stage_generators):
    for _ in itertools.zip_longest(*stage_generators):
        pass


def _gdn_kernel(q_ref, k_ref, v_ref, kt_ref, gat_ref, g_ref, y_ref,
                u_s, wq_s, at_s, kdt_s, egl_s, s_s, o_s):
    lt, l = q_ref.shape[0], y_ref.shape[0]
    n_steps = lt // PAIR
    n_ctx = (lt - l) // PAIR
    per_iter = GDN_STEPS
    assert n_steps % per_iter == 0

    ri = lax.broadcasted_iota(jnp.int32, (PAIR, PAIR), 0)
    ci = lax.broadcasted_iota(jnp.int32, (PAIR, PAIR), 1)
    same = (ri // CHUNK) == (ci // CHUNK)
    incl = (jnp.where(same & (ri >= ci), 1.0, 0.0), jnp.where(same & (ri <= ci), 1.0, 0.0))
    strict = (jnp.where(same & (ri > ci), 1.0, 0.0), jnp.where(same & (ri < ci), 1.0, 0.0))
    tri = (incl[1], incl[0])
    eye = jnp.where(ri == ci, 1.0, 0.0)
    eye2 = jnp.concatenate([eye, eye], axis=1)
    off = ([], [])
    for lvl in range(6):
        b = 2 ** lvl
        joined = (ri // (2 * b)) == (ci // (2 * b))
        off[0].append(jnp.where(joined & (ri % (2 * b) >= b) & (ci % (2 * b) < b), 1.0, 0.0))
        off[1].append(jnp.where(joined & (ri % (2 * b) < b) & (ci % (2 * b) >= b), 1.0, 0.0))

    def joining_blocks(a, d, lvl):
        m = off[d][lvl]
        return jnp.concatenate([a[:, :HEAD_D] * m, a[:, HEAD_D:] * m], axis=1)

    lane = lax.broadcasted_iota(jnp.int32, (1, PAIR), 1)
    first_half = lane < CHUNK
    scale = HEAD_D ** -0.5
    hi = lax.Precision.HIGHEST
    lanes = [slice(hp * 2 * HEAD_D, (hp + 1) * 2 * HEAD_D) for hp in range(HEADS // 2)]
    stream_ids = [(hp, d) for hp in range(HEADS // 2) for d in range(2)]
    zeros_half = jnp.zeros((CHUNK, 2 * HEAD_D), BF16)

    def pair_of(step, d):
        if d == 0:
            return lax.rem(step + (n_steps - n_ctx), n_steps)
        return n_steps - 1 - step

    def prepare(slot, step0):
        streams = []
        for kk in range(per_iter):
            for d in range(2):
                p = pair_of(step0 + kk, d)
                t0 = pl.multiple_of(p * PAIR, PAIR)
                gates = gat_ref[p]
                kt_all = kt_ref[p]
                gc = jnp.dot(gates, tri[d], precision=hi, preferred_element_type=F32)
                for hp in range(HEADS // 2):
                    k2 = k_ref[pl.ds(t0, PAIR), lanes[hp]]
                    q2 = q_ref[pl.ds(t0, PAIR), lanes[hp]]
                    v2 = v_ref[pl.ds(t0, PAIR), lanes[hp]]
                    kt2 = kt_all[hp * 2 * HEAD_D:(hp + 1) * 2 * HEAD_D, :]
                    bd_kt = _bd2(kt2[:HEAD_D], kt2[HEAD_D:])
                    streams.append(dict(kk=kk, d=d, hp=hp, gates=gates, gc=gc, k2=k2, q2=q2, v2=v2,
                                        kt2=kt2, kk_=_dot(k2, bd_kt), qk_=_dot(q2, bd_kt)))
        yield
        for st in streams:
            kk, d, hp, gates, gc = st["kk"], st["d"], st["hp"], st["gates"], st["gc"]
            gc_t = gc.T
            a, attn, qd, kdt, erow = [], [], [], [], []
            for hh in range(2):
                h = 2 * hp + hh
                hs = slice(hh * HEAD_D, (hh + 1) * HEAD_D)
                brow = gates[4 * d + h:4 * d + h + 1, :]
                gcr = gc[8 + 4 * d + h:9 + 4 * d + h, :]
                gcc = gc_t[:, 8 + 4 * d + h:9 + 4 * d + h]
                decay = jnp.exp((gcc - gcr) * incl[d]) * incl[d]
                a.append(st["kk_"][:, hs] * decay * strict[d] * brow)
                attn.append(st["qk_"][:, hs] * decay * (brow * scale))
                qd.append(st["q2"][:, hs].astype(F32) * (jnp.exp(gcc) * scale))
                if d == 0:
                    gl0, gl1 = gcr[:, CHUNK - 1:CHUNK], gcr[:, PAIR - 1:PAIR]
                else:
                    gl0, gl1 = gcr[:, 0:1], gcr[:, CHUNK:CHUNK + 1]
                glr = jnp.where(first_half, gl0, gl1)
                kdt.append(st["kt2"][hs, :].astype(F32) * (jnp.exp(glr - gcr) * brow))
                erow.append(jnp.exp(gcr))
                egl_s[slot, d, kk, 0, h:h + 1, :] = jnp.broadcast_to(jnp.exp(gl0), (1, PAIR))
                egl_s[slot, d, kk, 1, h:h + 1, :] = jnp.broadcast_to(jnp.exp(gl1), (1, PAIR))
            st["a"] = jnp.concatenate(a, axis=1)
            st["erow"] = jnp.concatenate(erow, axis=1)
            at_s[slot, d, kk, :, lanes[hp]] = jnp.concatenate(attn, axis=1).astype(BF16)
            kdt_s[slot, d, kk, hp] = jnp.concatenate(kdt, axis=1).astype(BF16)
            st["qd"] = jnp.concatenate(qd, axis=1).astype(BF16)
            st["t"] = eye2 - joining_blocks(st["a"], d, 0)
        yield
        for lvl in range(1, 6):
            for st in streams:
                st["y"] = _dot(joining_blocks(st["a"], st["d"], lvl).astype(BF16), _bd_of(st["t"]))
            yield
            for st in streams:
                st["t"] = st["t"] - _dot(st["t"].astype(BF16), _bd_of(st["y"]))
            yield
        for st in streams:
            t = st["t"]
            st["u"] = _dot(t.astype(BF16), _bd2(st["v2"][:, :HEAD_D], st["v2"][:, HEAD_D:]))
            st["w"] = _dot((t * st["erow"]).astype(BF16), _bd2(st["k2"][:, :HEAD_D], st["k2"][:, HEAD_D:]))
        yield
        for st in streams:
            kk, d, hp = st["kk"], st["d"], st["hp"]
            u_s[slot, d, kk, :, lanes[hp]] = st["u"].astype(BF16)
            wb, qdb = st["w"].astype(BF16), st["qd"]
            wq_s[slot, d, kk, :, lanes[hp]] = jnp.concatenate(
                [wb[:CHUNK], qdb[:CHUNK], wb[CHUNK:], qdb[CHUNK:]], axis=0)
        yield

    def scan(slot, step0):
        state = {(hp, d): s_s[d, :, lanes[hp]] for (hp, d) in stream_ids}
        for kk in range(per_iter):
            for half in range(2):
                e = {0: half, 1: 1 - half}
                res, bd_x = {}, {}
                for (hp, d) in stream_ids:
                    res[(hp, d)] = _dot(wq_s[slot, d, kk, e[d] * PAIR:(e[d] + 1) * PAIR, lanes[hp]],
                                        _bd_of(state[(hp, d)]))
                yield
                for (hp, d) in stream_ids:
                    vn = (u_s[slot, d, kk, e[d] * CHUNK:(e[d] + 1) * CHUNK, lanes[hp]].astype(F32)
                          - res[(hp, d)][:CHUNK])
                    vnb = vn.astype(BF16)
                    x = jnp.concatenate([vnb, zeros_half] if e[d] == 0 else [zeros_half, vnb], axis=0)
                    bd_x[(hp, d)] = _bd2(x[:, :HEAD_D], x[:, HEAD_D:])
                for (hp, d) in stream_ids:
                    both = _dot(jnp.concatenate(
                        [kdt_s[slot, d, kk, hp],
                         at_s[slot, d, kk, e[d] * CHUNK:(e[d] + 1) * CHUNK, lanes[hp]]], axis=0),
                        bd_x[(hp, d)])
                    eg = jnp.concatenate([egl_s[slot, d, kk, e[d], 2 * hp:2 * hp + 1, :],
                                          egl_s[slot, d, kk, e[d], 2 * hp + 1:2 * hp + 2, :]], axis=1)
                    state[(hp, d)] = state[(hp, d)] * eg + both[:HEAD_D]
                    r0 = pl.multiple_of(pair_of(step0 + kk, d) * PAIR + e[d] * CHUNK, CHUNK)
                    o_s[pl.ds(r0, CHUNK), lanes[hp]] += res[(hp, d)][CHUNK:] + both[HEAD_D:]
                yield
        for (hp, d) in stream_ids:
            s_s[d, :, lanes[hp]] = state[(hp, d)]

    s_s[...] = jnp.zeros(s_s.shape, F32)
    o_s[...] = jnp.zeros(o_s.shape, F32)
    n_iter = n_steps // per_iter
    _round_robin(prepare(0, 0))

    def body(it, carry):
        _round_robin(prepare(lax.rem(it + 1, 2), (it + 1) * per_iter), scan(lax.rem(it, 2), it * per_iter))
        return carry

    lax.fori_loop(0, n_iter - 1, body, 0)
    _round_robin(scan((n_iter - 1) % 2, (n_iter - 1) * per_iter))

    def epilogue(rb, carry):
        r0 = pl.multiple_of(rb * ROW_BLOCK, ROW_BLOCK)
        o = o_s[pl.ds(r0, ROW_BLOCK), :]
        segs = [_rms(o[:, h * HEAD_D:(h + 1) * HEAD_D]) * g_ref[...] for h in range(HEADS)]
        y_ref[pl.ds(r0, ROW_BLOCK), :] = jnp.concatenate(segs, axis=1).astype(BF16)
        return carry

    lax.fori_loop(0, l // ROW_BLOCK, epilogue, 0)


def _gdn(pg, kt, gat, g_gdn, l):
    b, lt, _ = pg.shape
    n_pairs = lt // PAIR

    def group(g):
        return pl.BlockSpec((None, lt, GROUP_W), lambda i, g=g: (i, 0, g))

    ring = (2, 2, GDN_STEPS)
    return pl.pallas_call(
        _gdn_kernel,
        grid=(b,),
        in_specs=[group(1), group(2), group(3),
                  pl.BlockSpec((None, n_pairs, GROUP_W, PAIR), lambda i: (i, 0, 0, 0)),
                  pl.BlockSpec((None, n_pairs, N_GATES, PAIR), lambda i: (i, 0, 0, 0)),
                  pl.BlockSpec((1, HEAD_D), lambda i: (0, 0))],
        out_specs=pl.BlockSpec((None, l, GROUP_W), lambda i: (i, 0, 0)),
        out_shape=jax.ShapeDtypeStruct((b, l, GROUP_W), BF16),
        scratch_shapes=[pltpu.VMEM(ring + (PAIR, GROUP_W), BF16),
                        pltpu.VMEM(ring + (2 * PAIR, GROUP_W), BF16),
                        pltpu.VMEM(ring + (PAIR, GROUP_W), BF16),
                        pltpu.VMEM(ring + (HEADS // 2, HEAD_D, 2 * PAIR), BF16),
                        pltpu.VMEM(ring + (2, 8, PAIR), F32),
                        pltpu.VMEM((2, HEAD_D, GROUP_W), F32),
                        pltpu.VMEM((lt, GROUP_W), F32)],
        compiler_params=_params(48, ("arbitrary",)),
        name="gdn",
    )(pg, pg, pg, kt, gat, g_gdn)


def _fourier_w_kernel(cc_ref, sc_ref, w_ref, ch_ref, sh_ref, o_ref, chb_ref, shb_ref):
    hi = lax.Precision.HIGHEST
    for g in range(w_ref.shape[0]):
        w = w_ref[g]
        o_ref[g] = jnp.concatenate(
            [jnp.dot(cc_ref[...], w, precision=hi, preferred_element_type=F32),
             -jnp.dot(sc_ref[...], w, precision=hi, preferred_element_type=F32)], axis=1)
    chb_ref[...] = ch_ref[...].astype(BF16)
    shb_ref[...] = sh_ref[...].astype(BF16)


def _fourier_w(cc, sc, w_f, ch, sh):
    groups, c, d = w_f.shape
    return pl.pallas_call(
        _fourier_w_kernel,
        out_shape=[jax.ShapeDtypeStruct((groups, c, 2 * d), F32),
                   jax.ShapeDtypeStruct(ch.shape, BF16),
                   jax.ShapeDtypeStruct(sh.shape, BF16)],
        compiler_params=pltpu.CompilerParams(vmem_limit_bytes=32 * MIB),
        name="fourier_w",
    )(cc, sc, w_f, ch, sh)


@functools.lru_cache(maxsize=None)
def _dft_constants(length, channels):
    def cos_sin(n, keep):
        idx = np.arange(keep, dtype=np.int64)
        ang = (np.outer(idx, idx) % n).astype(np.float64) * (2.0 * np.pi / n)
        return np.cos(ang), np.sin(ang)
    c_l, s_l = cos_sin(length, length // 2)
    c_c, s_c = cos_sin(channels, channels)
    norm = 1.0 / math.sqrt(length * channels)
    return (jnp.asarray(c_l.astype(np.float32)), jnp.asarray(s_l.astype(np.float32)),
            jnp.asarray((c_c * norm).astype(np.float32)),
            jnp.asarray((s_c * norm).astype(np.float32)))


def _mix_kernel(u_ref, on_ref, z_ref, ch_ref, sh_ref, mcs_ref, wout_ref, x_ref, gpost_ref, gt_ref,
                gpre_ref, shift_ref, scale_ref, x1_ref, h2_ref,
                yf_s, mc_s, ms_s, wout_s, ve_s, vo_s, d_s):
    i, m = pl.program_id(0), pl.program_id(1)
    l = u_ref.shape[0]
    h = l // 2
    blk = PAIR
    n_blk = h // blk

    @pl.when((i == 0) & (m == 0))
    def _():
        wout_s[...] = wout_ref[...].astype(BF16)
        mc_s[...] = jnp.zeros(mc_s.shape, BF16)
        ms_s[...] = jnp.zeros(ms_s.shape, BF16)
        for g in range(HEADS):
            gs = slice(g * HEAD_D, (g + 1) * HEAD_D)
            mc_s[gs, gs] = mcs_ref[g][:, :HEAD_D].astype(BF16)
            ms_s[gs, gs] = mcs_ref[g][:, HEAD_D:].astype(BF16)

    @pl.when(m == 0)
    def _():
        ri = lax.broadcasted_iota(jnp.int32, (blk, 2 * blk), 0)
        ci = lax.broadcasted_iota(jnp.int32, (blk, 2 * blk), 1)
        rev = jnp.where(((ci < blk) & (ri + ci == blk)) | ((ri == 0) & (ci == blk)), 1.0, 0.0).astype(BF16)
        zero_blk = jnp.zeros((blk, GROUP_W), BF16)

        def reversed_block(ref, base, k, last):
            top = ref[base + (n_blk - 1 - k) * blk:base + (n_blk - k) * blk, :]
            bottom = last if k == 0 else ref[base + (n_blk - k) * blk:base + (n_blk - k + 1) * blk, :]
            return _dot(rev, jnp.concatenate([top, bottom], axis=0))

        for k in range(n_blk):
            low = u_ref[k * blk:(k + 1) * blk, :].astype(F32)
            mirrored = reversed_block(u_ref, h, k, zero_blk)
            yf_s[k * blk:(k + 1) * blk, :] = (low + mirrored).astype(BF16)
            yf_s[h + k * blk:h + (k + 1) * blk, :] = (low - mirrored).astype(BF16)
        ve_s[...] = _dot(yf_s[:h, :], mc_s[...]).astype(BF16)
        vo_s[...] = _dot(yf_s[h:, :], ms_s[...]).astype(BF16)
        sign_row = jnp.where(lax.broadcasted_iota(jnp.int32, (16, l), 1) % 2 == 0, 1.0, -1.0)
        sign_row = jnp.where(lax.broadcasted_iota(jnp.int32, (16, l), 0) == 0, sign_row, 0.0).astype(BF16)
        alt_u = _dot(sign_row, u_ref[...])
        y_nyq = _dot(alt_u.astype(BF16), mc_s[...])[0:1, :]
        v_nyq = _dot(u_ref[h:h + 16, :], mc_s[...])[0:1, :]
        sign_col = jnp.where(lax.broadcasted_iota(jnp.int32, (blk, GROUP_W), 0) % 2 == 0, 1.0, -1.0)
        first_row = lax.broadcasted_iota(jnp.int32, (blk, GROUP_W), 0) == 0
        d_s[h:h + blk, :] = jnp.where(first_row, y_nyq, 0.0).astype(BF16)
        nyq = sign_col * v_nyq
        tall = min(FFN_ROWS, h)
        nyq = jnp.concatenate([nyq] * (tall // blk), axis=0)
        for k in range(h // tall):
            rows = slice(k * tall, (k + 1) * tall)
            a = _dot(ch_ref[rows, :], ve_s[...]) + nyq
            b = _dot(sh_ref[rows, :], vo_s[...])
            yf_s[rows, :] = (a + b).astype(BF16)
            d_s[rows, :] = (a - b).astype(BF16)
        for k in range(n_blk):
            yf_s[h + k * blk:h + (k + 1) * blk, :] = reversed_block(
                d_s, 0, k, d_s[h:h + blk, :]).astype(BF16)

    tm = x_ref.shape[0]
    sub = min(FFN_ROWS, tm)
    subs = [slice(sb * sub, (sb + 1) * sub) for sb in range(tm // sub)]
    ys = []
    for sb, rs in enumerate(subs):
        r0 = pl.multiple_of(m * tm + sb * sub, sub)
        yd = (on_ref[rs, :].astype(F32) * z_ref[rs, :].astype(F32)).astype(BF16)
        ys.append(_dot(jnp.concatenate([yf_s[pl.ds(r0, sub), :], yd], axis=1), wout_s[...]))
    post_gain = gt_ref[...] * gpost_ref[...]
    pre_gain = gpre_ref[...] * (1.0 + scale_ref[...])
    for rs, y in zip(subs, ys):
        x1 = x_ref[rs, :] + _rms(y) * post_gain
        x1_ref[rs, :] = x1
        h2_ref[rs, :] = (_rms(x1) * pre_gain + shift_ref[...]).astype(BF16)


def _mix(pg, on, ch, sh, mcs, w_out, x, g_post, gt, g_pre, shift, scale):
    b, l, d = x.shape
    tm = MIX_TILE
    h = l // 2
    vec = pl.BlockSpec((1, d), lambda i, m: (0, 0))
    mod = pl.BlockSpec((None, 1, d), lambda i, m: (i, 0, 0))
    whole = lambda a: pl.BlockSpec(a.shape, lambda i, m: (0,) * a.ndim)
    return pl.pallas_call(
        _mix_kernel,
        grid=(b, l // tm),
        in_specs=[pl.BlockSpec((None, l, GROUP_W), lambda i, m: (i, 0, 0)),
                  pl.BlockSpec((None, tm, GROUP_W), lambda i, m: (i, m, 0)),
                  pl.BlockSpec((None, tm, GROUP_W), lambda i, m: (i, m, N_GROUPS - 1)),
                  whole(ch), whole(sh), whole(mcs), whole(w_out),
                  pl.BlockSpec((None, tm, d), lambda i, m: (i, m, 0)),
                  vec, mod, vec, mod, mod],
        out_specs=[pl.BlockSpec((None, tm, d), lambda i, m: (i, m, 0)),
                   pl.BlockSpec((None, tm, d), lambda i, m: (i, m, 0))],
        out_shape=[jax.ShapeDtypeStruct((b, l, d), F32),
                   jax.ShapeDtypeStruct((b, l, d), BF16)],
        scratch_shapes=[pltpu.VMEM((l, GROUP_W), BF16),
                        pltpu.VMEM((GROUP_W, GROUP_W), BF16),
                        pltpu.VMEM((GROUP_W, GROUP_W), BF16),
                        pltpu.VMEM(w_out.shape, BF16),
                        pltpu.VMEM((h, GROUP_W), BF16),
                        pltpu.VMEM((h, GROUP_W), BF16),
                        pltpu.VMEM((h + PAIR, GROUP_W), BF16)],
        compiler_params=_params(48, ("arbitrary", "arbitrary")),
        name="mix",
    )(pg, on, pg, ch, sh, mcs, w_out, x, g_post, gt, g_pre, shift, scale)


def _ffn_kernel(h_ref, x1_ref, wv_ref, wg_ref, dwc_ref, wd_ref, gpost_ref, gt_ref,
                o_ref, gate_s, val_s, wvg_s, wd_s):
    ROW_BLOCK = FFN_ROWS
    j = pl.program_id(1)
    l = h_ref.shape[0]
    tf = wv_ref.shape[1]
    n_rb = l // ROW_BLOCK
    pad = GRID_W

    wvg_s[:, :tf] = wv_ref[...].astype(BF16)
    wvg_s[:, tf:] = wg_ref[...].astype(BF16)
    wd_s[...] = wd_ref[...].astype(BF16)
    gate_s[0:pad, :] = jnp.zeros((pad, tf), F32)
    gate_s[l + pad:l + 2 * pad, :] = jnp.zeros((pad, tf), F32)

    @pl.when(j == 0)
    def _():
        o_ref[...] = jnp.zeros(o_ref.shape, F32)

    n_ext = ROW_BLOCK + 2 * pad
    col = lax.broadcasted_iota(jnp.int32, (n_ext, tf), 0) % GRID_W
    not_first = jnp.where(col == 0, 0.0, 1.0)
    not_last = jnp.where(col == GRID_W - 1, 0.0, 1.0)
    dw = dwc_ref[...]

    def up(i):
        r0 = i * ROW_BLOCK
        vg = _dot(h_ref[r0:r0 + ROW_BLOCK, :], wvg_s[...])
        val_s[r0:r0 + ROW_BLOCK, :] = vg[:, :tf]
        gate_s[r0 + pad:r0 + pad + ROW_BLOCK, :] = vg[:, tf:]

    def down(i):
        r0 = i * ROW_BLOCK
        ext = gate_s[r0:r0 + n_ext, :]
        shifted = (pltpu.roll(ext, 1, 0) * not_first, ext, pltpu.roll(ext, n_ext - 1, 0) * not_last)
        conv = None
        for dy in range(3):
            for dx in range(3):
                term = shifted[dx][dy * pad:dy * pad + ROW_BLOCK] * dw[3 * dy + dx:3 * dy + dx + 1]
                conv = term if conv is None else conv + term
        act = (_silu(conv) * val_s[r0:r0 + ROW_BLOCK, :]).astype(BF16)
        o_ref[r0:r0 + ROW_BLOCK, :] += _dot(act, wd_s[...])

    for i in range(min(2, n_rb)):
        up(i)
    for i in range(n_rb):
        if i + 2 < n_rb:
            up(i + 2)
        down(i)

    @pl.when(j == pl.num_programs(1) - 1)
    def _():
        def fin(rb, carry):
            r0 = pl.multiple_of(rb * ROW_BLOCK, ROW_BLOCK)
            y = _rms(o_ref[pl.ds(r0, ROW_BLOCK), :]) * (gt_ref[...] * gpost_ref[...])
            o_ref[pl.ds(r0, ROW_BLOCK), :] = x1_ref[pl.ds(r0, ROW_BLOCK), :] + y
            return carry
        lax.fori_loop(0, n_rb, fin, 0)


def _ffn(h2, x1, w_up, dwc, w_down, g_post, gt):
    b, l, d = x1.shape
    d_ff = w_down.shape[0]
    tf = FFN_TILE
    n_f = d_ff // tf
    return pl.pallas_call(
        _ffn_kernel,
        grid=(b, n_f),
        in_specs=[pl.BlockSpec((None, l, d), lambda i, j: (i, 0, 0)),
                  pl.BlockSpec((None, l, d), lambda i, j: (i, 0, 0)),
                  pl.BlockSpec((d, tf), lambda i, j: (0, j)),
                  pl.BlockSpec((d, tf), lambda i, j: (0, n_f + j)),
                  pl.BlockSpec((9, tf), lambda i, j: (0, j)),
                  pl.BlockSpec((tf, d), lambda i, j: (j, 0)),
                  pl.BlockSpec((1, d), lambda i, j: (0, 0)),
                  pl.BlockSpec((None, 1, d), lambda i, j: (i, 0, 0))],
        out_specs=pl.BlockSpec((None, l, d), lambda i, j: (i, 0, 0)),
        out_shape=jax.ShapeDtypeStruct((b, l, d), F32),
        scratch_shapes=[pltpu.VMEM((l + 2 * GRID_W, tf), F32),
                        pltpu.VMEM((l, tf), F32),
                        pltpu.VMEM((d, 2 * tf), BF16),
                        pltpu.VMEM((tf, d), BF16)],
        compiler_params=_params(58, ("arbitrary", "arbitrary")),
        name="ffn",
    )(h2, x1, w_up, w_up, dwc, w_down, g_post, gt)


def kernel(x, c, ctx, c_ctx, w_ada, b_ada, g_pre_mix, g_post_mix, g_pre_ffn, g_post_ffn,
           w_in, w_qkv_conv, a_log, dt_bias, g_gdn, w_fourier, w_out, w_up, w_dwc, w_down):
    b, l, d = x.shape
    assert w_ada.shape[0] == 1, "single-layer stack"
    assert b < 16 and l % MIX_TILE == 0 and ctx.shape[1] % ROW_BLOCK == 0
    assert w_in.shape[2] == N_GROUPS * GROUP_W + N_GATES and d == 2 * GROUP_W

    cc = jnp.zeros((16, d), F32).at[:b].set(c).at[b].set(c_ctx)
    mod = _ada(cc, w_ada[0], b_ada)
    sh1, sc1, gt1, sh2, sc2, gt2 = [m[:, None, :] for m in jnp.split(mod[:b], 6, axis=-1)]
    sh1c, sc1c = [m[None, :] for m in jnp.split(mod[b], 6, axis=-1)[:2]]

    w_in_t = jnp.swapaxes(w_in, 1, 2)[0]
    wabt = w_in_t[N_GROUPS * GROUP_W:]
    zeros8 = jnp.zeros((8,), F32)
    alog = jnp.concatenate([zeros8, a_log[0].reshape(-1)])[:, None]
    dtb = jnp.concatenate([zeros8, dt_bias[0].reshape(-1)])[:, None]

    pg, kt, gat = _inproj(x, ctx, g_pre_mix, sh1, sc1, sh1c, sc1c, w_in_t, wabt, w_qkv_conv[0], alog, dtb)
    on = _gdn(pg, kt, gat, g_gdn, l)

    ch, sh, c_c, s_c = _dft_constants(l, HEAD_D)
    mcs, ch, sh = _fourier_w(c_c, s_c, w_fourier[0], ch, sh)
    x1, h2 = _mix(pg, on, ch, sh, mcs, w_out[0], x, g_post_mix, gt1, g_pre_ffn, sh2, sc2)
    d_ff = w_down.shape[1]
    return _ffn(h2, x1, w_up[0], w_dwc[0].reshape(9, d_ff), w_down[0], g_post_ffn, gt2)
```

```python
import functools
import itertools
import math

import numpy as np
import jax
import jax.numpy as jnp
from jax import lax
from jax.experimental import pallas as pl
from jax.experimental.pallas import tpu as pltpu

F32 = jnp.float32
BF16 = jnp.bfloat16

GRID_W = 64
HEADS = 4
HEAD_D = 128
CHUNK = 64
PAIR = 2 * CHUNK
GROUP_W = HEADS * HEAD_D
N_GROUPS = 5
N_GATES = 16
EPS = 1e-6
NEG_LOG2_E = -math.log2(math.e)
ROW_BLOCK = 256
FFN_TILE = 256
MIX_TILE = 1024
FFN_ROWS = 512
GDN_STEPS = 3
INPROJ_ROWS = 512
MIB = 1024 * 1024

_NT = (((1,), (1,)), ((), ()))


def _dot(a, b):
    return jnp.dot(a, b, preferred_element_type=F32)


def _silu(x):
    return x / (1.0 + jnp.exp2(x * NEG_LOG2_E))


def _rms(x):
    return x * lax.rsqrt(jnp.mean(x * x, axis=-1, keepdims=True) + EPS)


def _params(vmem_mib, semantics):
    return pltpu.CompilerParams(dimension_semantics=semantics,
                                vmem_limit_bytes=vmem_mib * MIB)


def _ada_kernel(c_ref, w_ref, b_ref, o_ref):
    s = _silu(c_ref[...]).astype(BF16)
    o_ref[...] = _dot(s, w_ref[...].astype(BF16)) + b_ref[...]


def _ada(cc, w, b):
    rows, d = cc.shape
    n = w.shape[1]
    tn = 1024
    return pl.pallas_call(
        _ada_kernel,
        grid=(n // tn,),
        in_specs=[pl.BlockSpec((rows, d), lambda j: (0, 0)),
                  pl.BlockSpec((d, tn), lambda j: (0, j)),
                  pl.BlockSpec((1, tn), lambda j: (0, j))],
        out_specs=pl.BlockSpec((rows, tn), lambda j: (0, j)),
        out_shape=jax.ShapeDtypeStruct((rows, n), F32),
        compiler_params=_params(32, ("arbitrary",)),
        name="ada",
    )(cc, w, b)


def _inproj_kernel(x_ref, ctx_ref, g_ref, sh_ref, sc_ref, shc_ref, scc_ref, w_ref, wab_ref,
                   conv_ref, alog_ref, dtb_ref, pg_ref, kt_ref, gat_ref, h_s, w_s, p_s):
    j = pl.program_id(1)
    l, lc = x_ref.shape[0], ctx_ref.shape[0]
    blocks = [(x_ref, r, r, r + 8, sh_ref, sc_ref, min(INPROJ_ROWS, l - r)) for r in range(0, l, INPROJ_ROWS)]
    blocks += [(ctx_ref, r, l + r, l + r + 16, shc_ref, scc_ref, min(INPROJ_ROWS, lc - r))
               for r in range(0, lc, INPROJ_ROWS)]
    n_rb = len(blocks)

    w_s[...] = w_ref[...].astype(BF16)

    def out_rows(rb):
        return slice(blocks[rb][2], blocks[rb][2] + blocks[rb][6])

    def project(rb):
        return lax.dot_general(h_s[out_rows(rb), :], w_s[...], _NT, preferred_element_type=F32)

    def pipelined(first, second, lead):
        for rb in range(min(lead, n_rb)):
            first(rb)
        for rb in range(n_rb):
            if rb + lead < n_rb:
                first(rb + lead)
            second(rb)

    @pl.when(j == 0)
    def _u():
        for r in (0, l + 8, l + lc + 16):
            p_s[r:r + 8, :] = jnp.zeros((8, GROUP_W), F32)
        wabt = wab_ref[...].astype(BF16)
        row = lax.broadcasted_iota(jnp.int32, (N_GATES, PAIR), 0)

        def prologue(rb):
            src, r, dst, _, shift, scale, n = blocks[rb]
            gain = g_ref[...] * (1.0 + scale[...])
            hb = (_rms(src[r:r + n, :]) * gain + shift[...]).astype(BF16)
            h_s[dst:dst + n, :] = hb
            ab_t = lax.dot_general(hb, wabt, _NT, preferred_element_type=F32).T
            for s in range(n // PAIR):
                ab = ab_t[:N_GATES, s * PAIR:(s + 1) * PAIR]
                beta = 1.0 / (1.0 + jnp.exp(-ab))
                xs = ab + dtb_ref[...]
                softplus = jnp.maximum(xs, 0.0) + jnp.log1p(jnp.exp(-jnp.abs(xs)))
                gate = -jnp.exp(alog_ref[...]) * softplus
                gat_ref[dst // PAIR + s] = jnp.where(row < 8, beta, gate)

        def store_u(rb):
            pg_ref[out_rows(rb), :] = project(rb).astype(BF16)

        pipelined(prologue, store_u, 2)

    @pl.when(j == N_GROUPS - 1)
    def _z():
        for rb in range(n_rb):
            pg_ref[out_rows(rb), :] = _silu(project(rb)).astype(BF16)

    def conv_group(normalise, transpose):
        cw = conv_ref[...]

        def raw(rb):
            p0, n = blocks[rb][3], blocks[rb][6]
            p_s[p0:p0 + n, :] = project(rb)

        def finish(rb):
            p0, n = blocks[rb][3], blocks[rb][6]
            ext = p_s[p0 - 8:p0 + n + 8, :]
            prev = pltpu.roll(ext, 1, 0)[8:8 + n]
            nxt = pltpu.roll(ext, n + 15, 0)[8:8 + n]
            cur = ext[8:8 + n]
            a = _silu(prev * cw[0:1] + cur * cw[1:2] + nxt * cw[2:3])
            if normalise:
                segs = []
                for hh in range(HEADS):
                    seg = a[:, hh * HEAD_D:(hh + 1) * HEAD_D]
                    segs.append(seg * lax.rsqrt(jnp.sum(seg * seg, axis=-1, keepdims=True) + EPS))
                a = jnp.concatenate(segs, axis=1)
            pg_ref[out_rows(rb), :] = a.astype(BF16)
            if transpose:
                at = a.T.astype(BF16)
                for s in range(n // PAIR):
                    kt_ref[blocks[rb][2] // PAIR + s] = at[:, s * PAIR:(s + 1) * PAIR]

        pipelined(raw, finish, 2)

    @pl.when(j == 1)
    def _q():
        conv_group(True, False)

    @pl.when(j == 2)
    def _k():
        conv_group(True, True)

    @pl.when(j == 3)
    def _v():
        conv_group(False, False)


def _inproj(x, ctx, g, shift, scale, shift_c, scale_c, w_in_t, wabt, conv, alog, dtb):
    b, l, d = x.shape
    lt = l + ctx.shape[1]
    n_pairs = lt // PAIR
    vec = pl.BlockSpec((1, d), lambda i, j: (0, 0))
    mod = pl.BlockSpec((None, 1, d), lambda i, j: (i, 0, 0))
    return pl.pallas_call(
        _inproj_kernel,
        grid=(b, N_GROUPS),
        in_specs=[pl.BlockSpec((None, l, d), lambda i, j: (i, 0, 0)),
                  pl.BlockSpec((None, lt - l, d), lambda i, j: (i, 0, 0)),
                  vec, mod, mod, vec, vec,
                  pl.BlockSpec((GROUP_W, d), lambda i, j: (j, 0)),
                  pl.BlockSpec((PAIR, d), lambda i, j: (0, 0)),
                  pl.BlockSpec((3, GROUP_W), lambda i, j: (0, jnp.clip(j - 1, 0, 2))),
                  pl.BlockSpec((N_GATES, 1), lambda i, j: (0, 0)),
                  pl.BlockSpec((N_GATES, 1), lambda i, j: (0, 0))],
        out_specs=[pl.BlockSpec((None, lt, GROUP_W), lambda i, j: (i, 0, j)),
                   pl.BlockSpec((None, n_pairs, GROUP_W, PAIR), lambda i, j: (i, 0, 0, 0)),
                   pl.BlockSpec((None, n_pairs, N_GATES, PAIR), lambda i, j: (i, 0, 0, 0))],
        out_shape=[jax.ShapeDtypeStruct((b, lt, N_GROUPS * GROUP_W), BF16),
                   jax.ShapeDtypeStruct((b, n_pairs, GROUP_W, PAIR), BF16),
                   jax.ShapeDtypeStruct((b, n_pairs, N_GATES, PAIR), F32)],
        scratch_shapes=[pltpu.VMEM((lt, d), BF16),
                        pltpu.VMEM((GROUP_W, d), BF16),
                        pltpu.VMEM((lt + 24, GROUP_W), F32)],
        compiler_params=_params(52, ("arbitrary", "arbitrary")),
        name="inproj",
    )(x, ctx, g, shift, scale, shift_c, scale_c, w_in_t, wabt, conv, alog, dtb)


def _bd2(a, b):
    z = jnp.zeros_like(a)
    return jnp.concatenate([jnp.concatenate([a, z], axis=1),
                            jnp.concatenate([z, b], axis=1)], axis=0)


def _bd_of(x):
    xb = x.astype(BF16)
    return _bd2(xb[:, :HEAD_D], xb[:, HEAD_D:])


def _round_robin(*stage_generators):
    for _ in itertools.zip_longest(*stage_generators):
        pass


def _gdn_kernel(q_ref, k_ref, v_ref, kt_ref, gat_ref, g_ref, y_ref,
                u_s, wq_s, at_s, kdt_s, egl_s, s_s, o_s):
    lt, l = q_ref.shape[0], y_ref.shape[0]
    n_steps = lt // PAIR
    n_ctx = (lt - l) // PAIR
    per_iter = GDN_STEPS
    assert n_steps % per_iter == 0

    ri = lax.broadcasted_iota(jnp.int32, (PAIR, PAIR), 0)
    ci = lax.broadcasted_iota(jnp.int32, (PAIR, PAIR), 1)
    same = (ri // CHUNK) == (ci // CHUNK)
    incl = (jnp.where(same & (ri >= ci), 1.0, 0.0), jnp.where(same & (ri <= ci), 1.0, 0.0))
    strict = (jnp.where(same & (ri > ci), 1.0, 0.0), jnp.where(same & (ri < ci), 1.0, 0.0))
    tri = (incl[1].astype(BF16), incl[0].astype(BF16))

    def chunk_cumsum(x, d):
        p1 = x.astype(BF16)
        r1 = x - p1.astype(F32)
        p2 = r1.astype(BF16)
        p3 = (r1 - p2.astype(F32)).astype(BF16)
        n = x.shape[0]
        s = _dot(jnp.concatenate([p1, p2, p3], axis=0), tri[d])
        return s[:n] + s[n:2 * n] + s[2 * n:]
    eye = jnp.where(ri == ci, 1.0, 0.0)
    eye2 = jnp.concatenate([eye, eye], axis=1)
    off = ([], [])
    for lvl in range(6):
        b = 2 ** lvl
        joined = (ri // (2 * b)) == (ci // (2 * b))
        off[0].append(jnp.where(joined & (ri % (2 * b) >= b) & (ci % (2 * b) < b), 1.0, 0.0))
        off[1].append(jnp.where(joined & (ri % (2 * b) < b) & (ci % (2 * b) >= b), 1.0, 0.0))

    def joining_blocks(a, d, lvl):
        m = off[d][lvl]
        return jnp.concatenate([a[:, :HEAD_D] * m, a[:, HEAD_D:] * m], axis=1)

    lane = lax.broadcasted_iota(jnp.int32, (1, PAIR), 1)
    first_half = lane < CHUNK
    scale = HEAD_D ** -0.5
    lanes = [slice(hp * 2 * HEAD_D, (hp + 1) * 2 * HEAD_D) for hp in range(HEADS // 2)]
    stream_ids = [(hp, d) for hp in range(HEADS // 2) for d in range(2)]
    zeros_half = jnp.zeros((CHUNK, 2 * HEAD_D), BF16)

    def pair_of(step, d):
        if d == 0:
            return lax.rem(step + (n_steps - n_ctx), n_steps)
        return n_steps - 1 - step

    def prepare(slot, step0):
        streams = []
        for kk in range(per_iter):
            for d in range(2):
                p = pair_of(step0 + kk, d)
                t0 = pl.multiple_of(p * PAIR, PAIR)
                gates = gat_ref[p]
                kt_all = kt_ref[p]
                gc = chunk_cumsum(gates, d)
                for hp in range(HEADS // 2):
                    k2 = k_ref[pl.ds(t0, PAIR), lanes[hp]]
                    q2 = q_ref[pl.ds(t0, PAIR), lanes[hp]]
                    v2 = v_ref[pl.ds(t0, PAIR), lanes[hp]]
                    kt2 = kt_all[hp * 2 * HEAD_D:(hp + 1) * 2 * HEAD_D, :]
                    bd_kt = _bd2(kt2[:HEAD_D], kt2[HEAD_D:])
                    streams.append(dict(kk=kk, d=d, hp=hp, gates=gates, gc=gc, k2=k2, q2=q2, v2=v2,
                                        kt2=kt2, kk_=_dot(k2, bd_kt), qk_=_dot(q2, bd_kt)))
        yield
        for st in streams:
            kk, d, hp, gates, gc = st["kk"], st["d"], st["hp"], st["gates"], st["gc"]
            gc_t = gc.T
            a, attn, qd, kdt, erow = [], [], [], [], []
            for hh in range(2):
                h = 2 * hp + hh
                hs = slice(hh * HEAD_D, (hh + 1) * HEAD_D)
                brow = gates[4 * d + h:4 * d + h + 1, :]
                gcr = gc[8 + 4 * d + h:9 + 4 * d + h, :]
                gcc = gc_t[:, 8 + 4 * d + h:9 + 4 * d + h]
                decay = jnp.exp((gcc - gcr) * incl[d]) * incl[d]
                a.append(st["kk_"][:, hs] * decay * strict[d] * brow)
                attn.append(st["qk_"][:, hs] * decay * (brow * scale))
                qd.append(st["q2"][:, hs].astype(F32) * (jnp.exp(gcc) * scale))
                if d == 0:
                    gl0, gl1 = gcr[:, CHUNK - 1:CHUNK], gcr[:, PAIR - 1:PAIR]
                else:
                    gl0, gl1 = gcr[:, 0:1], gcr[:, CHUNK:CHUNK + 1]
                glr = jnp.where(first_half, gl0, gl1)
                kdt.append(st["kt2"][hs, :].astype(F32) * (jnp.exp(glr - gcr) * brow))
                erow.append(jnp.exp(gcr))
                egl_s[slot, d, kk, 0, h:h + 1, :] = jnp.broadcast_to(jnp.exp(gl0), (1, PAIR))
                egl_s[slot, d, kk, 1, h:h + 1, :] = jnp.broadcast_to(jnp.exp(gl1), (1, PAIR))
            st["a"] = jnp.concatenate(a, axis=1)
            st["erow"] = jnp.concatenate(erow, axis=1)
            at_s[slot, d, kk, :, lanes[hp]] = jnp.concatenate(attn, axis=1).astype(BF16)
            kdt_s[slot, d, kk, hp] = jnp.concatenate(kdt, axis=1).astype(BF16)
            st["qd"] = jnp.concatenate(qd, axis=1).astype(BF16)
            st["t"] = eye2 - joining_blocks(st["a"], d, 0)
        yield
        for lvl in range(1, 6):
            for st in streams:
                st["y"] = _dot(joining_blocks(st["a"], st["d"], lvl).astype(BF16), _bd_of(st["t"]))
            yield
            for st in streams:
                st["t"] = st["t"] - _dot(st["t"].astype(BF16), _bd_of(st["y"]))
            yield
        for st in streams:
            t = st["t"]
            st["u"] = _dot(t.astype(BF16), _bd2(st["v2"][:, :HEAD_D], st["v2"][:, HEAD_D:]))
            st["w"] = _dot((t * st["erow"]).astype(BF16), _bd2(st["k2"][:, :HEAD_D], st["k2"][:, HEAD_D:]))
        yield
        for st in streams:
            kk, d, hp = st["kk"], st["d"], st["hp"]
            u_s[slot, d, kk, :, lanes[hp]] = st["u"].astype(BF16)
            wb, qdb = st["w"].astype(BF16), st["qd"]
            wq_s[slot, d, kk, :, lanes[hp]] = jnp.concatenate(
                [wb[:CHUNK], qdb[:CHUNK], wb[CHUNK:], qdb[CHUNK:]], axis=0)
        yield

    def scan(slot, step0):
        state = {(hp, d): s_s[d, :, lanes[hp]] for (hp, d) in stream_ids}
        for kk in range(per_iter):
            for half in range(2):
                e = {0: half, 1: 1 - half}
                res, bd_x = {}, {}
                for (hp, d) in stream_ids:
                    res[(hp, d)] = _dot(wq_s[slot, d, kk, e[d] * PAIR:(e[d] + 1) * PAIR, lanes[hp]],
                                        _bd_of(state[(hp, d)]))
                yield
                for (hp, d) in stream_ids:
                    vn = (u_s[slot, d, kk, e[d] * CHUNK:(e[d] + 1) * CHUNK, lanes[hp]].astype(F32)
                          - res[(hp, d)][:CHUNK])
                    vnb = vn.astype(BF16)
                    x = jnp.concatenate([vnb, zeros_half] if e[d] == 0 else [zeros_half, vnb], axis=0)
                    bd_x[(hp, d)] = _bd2(x[:, :HEAD_D], x[:, HEAD_D:])
                for (hp, d) in stream_ids:
                    both = _dot(jnp.concatenate(
                        [kdt_s[slot, d, kk, hp],
                         at_s[slot, d, kk, e[d] * CHUNK:(e[d] + 1) * CHUNK, lanes[hp]]], axis=0),
                        bd_x[(hp, d)])
                    eg = jnp.concatenate([egl_s[slot, d, kk, e[d], 2 * hp:2 * hp + 1, :],
                                          egl_s[slot, d, kk, e[d], 2 * hp + 1:2 * hp + 2, :]], axis=1)
                    state[(hp, d)] = state[(hp, d)] * eg + both[:HEAD_D]
                    r0 = pl.multiple_of(pair_of(step0 + kk, d) * PAIR + e[d] * CHUNK, CHUNK)
                    o_s[pl.ds(r0, CHUNK), lanes[hp]] += res[(hp, d)][CHUNK:] + both[HEAD_D:]
                yield
        for (hp, d) in stream_ids:
            s_s[d, :, lanes[hp]] = state[(hp, d)]

    s_s[...] = jnp.zeros(s_s.shape, F32)
    o_s[...] = jnp.zeros(o_s.shape, F32)
    n_iter = n_steps // per_iter
    _round_robin(prepare(0, 0))

    def body(it, carry):
        _round_robin(prepare(lax.rem(it + 1, 2), (it + 1) * per_iter), scan(lax.rem(it, 2), it * per_iter))
        return carry

    lax.fori_loop(0, n_iter - 1, body, 0)
    _round_robin(scan((n_iter - 1) % 2, (n_iter - 1) * per_iter))

    def epilogue(rb, carry):
        r0 = pl.multiple_of(rb * ROW_BLOCK, ROW_BLOCK)
        o = o_s[pl.ds(r0, ROW_BLOCK), :]
        segs = [_rms(o[:, h * HEAD_D:(h + 1) * HEAD_D]) * g_ref[...] for h in range(HEADS)]
        y_ref[pl.ds(r0, ROW_BLOCK), :] = jnp.concatenate(segs, axis=1).astype(BF16)
        return carry

    lax.fori_loop(0, l // ROW_BLOCK, epilogue, 0)


def _gdn(pg, kt, gat, g_gdn, l):
    b, lt, _ = pg.shape
    n_pairs = lt // PAIR

    def group(g):
        return pl.BlockSpec((None, lt, GROUP_W), lambda i, g=g: (i, 0, g))

    ring = (2, 2, GDN_STEPS)
    return pl.pallas_call(
        _gdn_kernel,
        grid=(b,),
        in_specs=[group(1), group(2), group(3),
                  pl.BlockSpec((None, n_pairs, GROUP_W, PAIR), lambda i: (i, 0, 0, 0)),
                  pl.BlockSpec((None, n_pairs, N_GATES, PAIR), lambda i: (i, 0, 0, 0)),
                  pl.BlockSpec((1, HEAD_D), lambda i: (0, 0))],
        out_specs=pl.BlockSpec((None, l, GROUP_W), lambda i: (i, 0, 0)),
        out_shape=jax.ShapeDtypeStruct((b, l, GROUP_W), BF16),
        scratch_shapes=[pltpu.VMEM(ring + (PAIR, GROUP_W), BF16),
                        pltpu.VMEM(ring + (2 * PAIR, GROUP_W), BF16),
                        pltpu.VMEM(ring + (PAIR, GROUP_W), BF16),
                        pltpu.VMEM(ring + (HEADS // 2, HEAD_D, 2 * PAIR), BF16),
                        pltpu.VMEM(ring + (2, 8, PAIR), F32),
                        pltpu.VMEM((2, HEAD_D, GROUP_W), F32),
                        pltpu.VMEM((lt, GROUP_W), F32)],
        compiler_params=_params(48, ("arbitrary",)),
        name="gdn",
    )(pg, pg, pg, kt, gat, g_gdn)


def _fourier_w_kernel(cc_ref, sc_ref, w_ref, ch_ref, sh_ref, o_ref, chb_ref, shb_ref):
    hi = lax.Precision.HIGHEST
    for g in range(w_ref.shape[0]):
        w = w_ref[g]
        o_ref[g] = jnp.concatenate(
            [jnp.dot(cc_ref[...], w, precision=hi, preferred_element_type=F32),
             -jnp.dot(sc_ref[...], w, precision=hi, preferred_element_type=F32)], axis=1)
    chb_ref[...] = ch_ref[...].astype(BF16)
    shb_ref[...] = sh_ref[...].astype(BF16)


def _fourier_w(cc, sc, w_f, ch, sh):
    groups, c, d = w_f.shape
    return pl.pallas_call(
        _fourier_w_kernel,
        out_shape=[jax.ShapeDtypeStruct((groups, c, 2 * d), F32),
                   jax.ShapeDtypeStruct(ch.shape, BF16),
                   jax.ShapeDtypeStruct(sh.shape, BF16)],
        compiler_params=pltpu.CompilerParams(vmem_limit_bytes=32 * MIB),
        name="fourier_w",
    )(cc, sc, w_f, ch, sh)


@functools.lru_cache(maxsize=None)
def _dft_constants(length, channels):
    def cos_sin(n, keep):
        idx = np.arange(keep, dtype=np.int64)
        ang = (np.outer(idx, idx) % n).astype(np.float64) * (2.0 * np.pi / n)
        return np.cos(ang), np.sin(ang)
    c_l, s_l = cos_sin(length, length // 2)
    c_c, s_c = cos_sin(channels, channels)
    norm = 1.0 / math.sqrt(length * channels)
    return (jnp.asarray(c_l.astype(np.float32)), jnp.asarray(s_l.astype(np.float32)),
            jnp.asarray((c_c * norm).astype(np.float32)),
            jnp.asarray((s_c * norm).astype(np.float32)))


def _mix_kernel(u_ref, on_ref, z_ref, ch_ref, sh_ref, mcs_ref, wout_ref, x_ref, gpost_ref, gt_ref,
                gpre_ref, shift_ref, scale_ref, x1_ref, h2_ref,
                yf_s, mc_s, ms_s, wout_s, ve_s, vo_s, d_s):
    i, m = pl.program_id(0), pl.program_id(1)
    l = u_ref.shape[0]
    h = l // 2
    blk = PAIR
    n_blk = h // blk

    @pl.when((i == 0) & (m == 0))
    def _():
        wout_s[...] = wout_ref[...].astype(BF16)
        mc_s[...] = jnp.zeros(mc_s.shape, BF16)
        ms_s[...] = jnp.zeros(ms_s.shape, BF16)
        for g in range(HEADS):
            gs = slice(g * HEAD_D, (g + 1) * HEAD_D)
            mc_s[gs, gs] = mcs_ref[g][:, :HEAD_D].astype(BF16)
            ms_s[gs, gs] = mcs_ref[g][:, HEAD_D:].astype(BF16)

    @pl.when(m == 0)
    def _():
        ri = lax.broadcasted_iota(jnp.int32, (blk, 2 * blk), 0)
        ci = lax.broadcasted_iota(jnp.int32, (blk, 2 * blk), 1)
        rev = jnp.where(((ci < blk) & (ri + ci == blk)) | ((ri == 0) & (ci == blk)), 1.0, 0.0).astype(BF16)
        zero_blk = jnp.zeros((blk, GROUP_W), BF16)

        def reversed_block(ref, base, k, last):
            top = ref[base + (n_blk - 1 - k) * blk:base + (n_blk - k) * blk, :]
            bottom = last if k == 0 else ref[base + (n_blk - k) * blk:base + (n_blk - k + 1) * blk, :]
            return _dot(rev, jnp.concatenate([top, bottom], axis=0))

        for k in range(n_blk):
            low = u_ref[k * blk:(k + 1) * blk, :].astype(F32)
            mirrored = reversed_block(u_ref, h, k, zero_blk)
            yf_s[k * blk:(k + 1) * blk, :] = (low + mirrored).astype(BF16)
            yf_s[h + k * blk:h + (k + 1) * blk, :] = (low - mirrored).astype(BF16)
        ve_s[...] = _dot(yf_s[:h, :], mc_s[...]).astype(BF16)
        vo_s[...] = _dot(yf_s[h:, :], ms_s[...]).astype(BF16)
        sign_row = jnp.where(lax.broadcasted_iota(jnp.int32, (16, l), 1) % 2 == 0, 1.0, -1.0)
        sign_row = jnp.where(lax.broadcasted_iota(jnp.int32, (16, l), 0) == 0, sign_row, 0.0).astype(BF16)
        alt_u = _dot(sign_row, u_ref[...])
        y_nyq = _dot(alt_u.astype(BF16), mc_s[...])[0:1, :]
        v_nyq = _dot(u_ref[h:h + 16, :], mc_s[...])[0:1, :]
        sign_col = jnp.where(lax.broadcasted_iota(jnp.int32, (blk, GROUP_W), 0) % 2 == 0, 1.0, -1.0)
        first_row = lax.broadcasted_iota(jnp.int32, (blk, GROUP_W), 0) == 0
        d_s[h:h + blk, :] = jnp.where(first_row, y_nyq, 0.0).astype(BF16)
        nyq = sign_col * v_nyq
        tall = min(FFN_ROWS, h)
        nyq = jnp.concatenate([nyq] * (tall // blk), axis=0)
        for k in range(h // tall):
            rows = slice(k * tall, (k + 1) * tall)
            a = _dot(ch_ref[rows, :], ve_s[...]) + nyq
            b = _dot(sh_ref[rows, :], vo_s[...])
            yf_s[rows, :] = (a + b).astype(BF16)
            d_s[rows, :] = (a - b).astype(BF16)
        for k in range(n_blk):
            yf_s[h + k * blk:h + (k + 1) * blk, :] = reversed_block(
                d_s, 0, k, d_s[h:h + blk, :]).astype(BF16)

    tm = x_ref.shape[0]
    sub = min(FFN_ROWS, tm)
    subs = [slice(sb * sub, (sb + 1) * sub) for sb in range(tm // sub)]
    ys = []
    for sb, rs in enumerate(subs):
        r0 = pl.multiple_of(m * tm + sb * sub, sub)
        yd = (on_ref[rs, :].astype(F32) * z_ref[rs, :].astype(F32)).astype(BF16)
        ys.append(_dot(jnp.concatenate([yf_s[pl.ds(r0, sub), :], yd], axis=1), wout_s[...]))
    post_gain = gt_ref[...] * gpost_ref[...]
    pre_gain = gpre_ref[...] * (1.0 + scale_ref[...])
    for rs, y in zip(subs, ys):
        x1 = x_ref[rs, :] + _rms(y) * post_gain
        x1_ref[rs, :] = x1
        h2_ref[rs, :] = (_rms(x1) * pre_gain + shift_ref[...]).astype(BF16)


def _mix(pg, on, ch, sh, mcs, w_out, x, g_post, gt, g_pre, shift, scale):
    b, l, d = x.shape
    tm = MIX_TILE
    h = l // 2
    vec = pl.BlockSpec((1, d), lambda i, m: (0, 0))
    mod = pl.BlockSpec((None, 1, d), lambda i, m: (i, 0, 0))
    whole = lambda a: pl.BlockSpec(a.shape, lambda i, m: (0,) * a.ndim)
    return pl.pallas_call(
        _mix_kernel,
        grid=(b, l // tm),
        in_specs=[pl.BlockSpec((None, l, GROUP_W), lambda i, m: (i, 0, 0)),
                  pl.BlockSpec((None, tm, GROUP_W), lambda i, m: (i, m, 0)),
                  pl.BlockSpec((None, tm, GROUP_W), lambda i, m: (i, m, N_GROUPS - 1)),
                  whole(ch), whole(sh), whole(mcs), whole(w_out),
                  pl.BlockSpec((None, tm, d), lambda i, m: (i, m, 0)),
                  vec, mod, vec, mod, mod],
        out_specs=[pl.BlockSpec((None, tm, d), lambda i, m: (i, m, 0)),
                   pl.BlockSpec((None, tm, d), lambda i, m: (i, m, 0))],
        out_shape=[jax.ShapeDtypeStruct((b, l, d), F32),
                   jax.ShapeDtypeStruct((b, l, d), BF16)],
        scratch_shapes=[pltpu.VMEM((l, GROUP_W), BF16),
                        pltpu.VMEM((GROUP_W, GROUP_W), BF16),
                        pltpu.VMEM((GROUP_W, GROUP_W), BF16),
                        pltpu.VMEM(w_out.shape, BF16),
                        pltpu.VMEM((h, GROUP_W), BF16),
                        pltpu.VMEM((h, GROUP_W), BF16),
                        pltpu.VMEM((h + PAIR, GROUP_W), BF16)],
        compiler_params=_params(48, ("arbitrary", "arbitrary")),
        name="mix",
    )(pg, on, pg, ch, sh, mcs, w_out, x, g_post, gt, g_pre, shift, scale)


def _ffn_kernel(h_ref, x1_ref, wv_ref, wg_ref, dwc_ref, wd_ref, gpost_ref, gt_ref,
                o_ref, gate_s, val_s, wvg_s, wd_s):
    ROW_BLOCK = FFN_ROWS
    j = pl.program_id(1)
    l = h_ref.shape[0]
    tf = wv_ref.shape[1]
    n_rb = l // ROW_BLOCK
    pad = GRID_W

    wvg_s[:, :tf] = wv_ref[...].astype(BF16)
    wvg_s[:, tf:] = wg_ref[...].astype(BF16)
    wd_s[...] = wd_ref[...].astype(BF16)
    gate_s[0:pad, :] = jnp.zeros((pad, tf), F32)
    gate_s[l + pad:l + 2 * pad, :] = jnp.zeros((pad, tf), F32)

    @pl.when(j == 0)
    def _():
        o_ref[...] = jnp.zeros(o_ref.shape, F32)

    n_ext = ROW_BLOCK + 2 * pad
    col = lax.broadcasted_iota(jnp.int32, (n_ext, tf), 0) % GRID_W
    not_first = jnp.where(col == 0, 0.0, 1.0)
    not_last = jnp.where(col == GRID_W - 1, 0.0, 1.0)
    dw = dwc_ref[...]

    def up(i):
        r0 = i * ROW_BLOCK
        vg = _dot(h_ref[r0:r0 + ROW_BLOCK, :], wvg_s[...])
        val_s[r0:r0 + ROW_BLOCK, :] = vg[:, :tf]
        gate_s[r0 + pad:r0 + pad + ROW_BLOCK, :] = vg[:, tf:]

    def down(i):
        r0 = i * ROW_BLOCK
        ext = gate_s[r0:r0 + n_ext, :]
        shifted = (pltpu.roll(ext, 1, 0) * not_first, ext, pltpu.roll(ext, n_ext - 1, 0) * not_last)
        conv = None
        for dy in range(3):
            for dx in range(3):
                term = shifted[dx][dy * pad:dy * pad + ROW_BLOCK] * dw[3 * dy + dx:3 * dy + dx + 1]
                conv = term if conv is None else conv + term
        act = (_silu(conv) * val_s[r0:r0 + ROW_BLOCK, :]).astype(BF16)
        o_ref[r0:r0 + ROW_BLOCK, :] += _dot(act, wd_s[...])

    for i in range(min(2, n_rb)):
        up(i)
    for i in range(n_rb):
        if i + 2 < n_rb:
            up(i + 2)
        down(i)

    @pl.when(j == pl.num_programs(1) - 1)
    def _():
        def fin(rb, carry):
            r0 = pl.multiple_of(rb * ROW_BLOCK, ROW_BLOCK)
            y = _rms(o_ref[pl.ds(r0, ROW_BLOCK), :]) * (gt_ref[...] * gpost_ref[...])
            o_ref[pl.ds(r0, ROW_BLOCK), :] = x1_ref[pl.ds(r0, ROW_BLOCK), :] + y
            return carry
        lax.fori_loop(0, n_rb, fin, 0)


def _ffn(h2, x1, w_up, dwc, w_down, g_post, gt):
    b, l, d = x1.shape
    d_ff = w_down.shape[0]
    tf = FFN_TILE
    n_f = d_ff // tf
    return pl.pallas_call(
        _ffn_kernel,
        grid=(b, n_f),
        in_specs=[pl.BlockSpec((None, l, d), lambda i, j: (i, 0, 0)),
                  pl.BlockSpec((None, l, d), lambda i, j: (i, 0, 0)),
                  pl.BlockSpec((d, tf), lambda i, j: (0, j)),
                  pl.BlockSpec((d, tf), lambda i, j: (0, n_f + j)),
                  pl.BlockSpec((9, tf), lambda i, j: (0, j)),
                  pl.BlockSpec((tf, d), lambda i, j: (j, 0)),
                  pl.BlockSpec((1, d), lambda i, j: (0, 0)),
                  pl.BlockSpec((None, 1, d), lambda i, j: (i, 0, 0))],
        out_specs=pl.BlockSpec((None, l, d), lambda i, j: (i, 0, 0)),
        out_shape=jax.ShapeDtypeStruct((b, l, d), F32),
        scratch_shapes=[pltpu.VMEM((l + 2 * GRID_W, tf), F32),
                        pltpu.VMEM((l, tf), F32),
                        pltpu.VMEM((d, 2 * tf), BF16),
                        pltpu.VMEM((tf, d), BF16)],
        compiler_params=_params(58, ("arbitrary", "arbitrary")),
        name="ffn",
    )(h2, x1, w_up, w_up, dwc, w_down, g_post, gt)


def kernel(x, c, ctx, c_ctx, w_ada, b_ada, g_pre_mix, g_post_mix, g_pre_ffn, g_post_ffn,
           w_in, w_qkv_conv, a_log, dt_bias, g_gdn, w_fourier, w_out, w_up, w_dwc, w_down):
    b, l, d = x.shape
    assert w_ada.shape[0] == 1, "single-layer stack"
    assert b < 16 and l % MIX_TILE == 0 and ctx.shape[1] % ROW_BLOCK == 0
    assert w_in.shape[2] == N_GROUPS * GROUP_W + N_GATES and d == 2 * GROUP_W

    cc = jnp.zeros((16, d), F32).at[:b].set(c).at[b].set(c_ctx)
    mod = _ada(cc, w_ada[0], b_ada)
    sh1, sc1, gt1, sh2, sc2, gt2 = [m[:, None, :] for m in jnp.split(mod[:b], 6, axis=-1)]
    sh1c, sc1c = [m[None, :] for m in jnp.split(mod[b], 6, axis=-1)[:2]]

    w_in_t = jnp.swapaxes(w_in, 1, 2)[0]
    wabt = jnp.pad(w_in_t[N_GROUPS * GROUP_W:], ((0, PAIR - N_GATES), (0, 0)))
    zeros8 = jnp.zeros((8,), F32)
    alog = jnp.concatenate([zeros8, a_log[0].reshape(-1)])[:, None]
    dtb = jnp.concatenate([zeros8, dt_bias[0].reshape(-1)])[:, None]

    pg, kt, gat = _inproj(x, ctx, g_pre_mix, sh1, sc1, sh1c, sc1c, w_in_t, wabt, w_qkv_conv[0], alog, dtb)
    on = _gdn(pg, kt, gat, g_gdn, l)

    ch, sh, c_c, s_c = _dft_constants(l, HEAD_D)
    mcs, ch, sh = _fourier_w(c_c, s_c, w_fourier[0], ch, sh)
    x1, h2 = _mix(pg, on, ch, sh, mcs, w_out[0], x, g_post_mix, gt1, g_pre_ffn, sh2, sc2)
    d_ff = w_down.shape[1]
    return _ffn(h2, x1, w_up[0], w_dwc[0].reshape(9, d_ff), w_down[0], g_post_ffn, gt2)
```

```python
import functools
import itertools
import math

import numpy as np
import jax
import jax.numpy as jnp
from jax import lax
from jax.experimental import pallas as pl
from jax.experimental.pallas import tpu as pltpu

F32 = jnp.float32
BF16 = jnp.bfloat16

GRID_W = 64
HEADS = 4
HEAD_D = 128
CHUNK = 64
PAIR = 2 * CHUNK
GROUP_W = HEADS * HEAD_D
N_GROUPS = 5
N_GATES = 16
EPS = 1e-6
NEG_LOG2_E = -math.log2(math.e)
ROW_BLOCK = 256
FFN_TILE = 256
MIX_TILE = 1024
FFN_ROWS = 512
GDN_STEPS = 3
INPROJ_ROWS = 512
MIB = 1024 * 1024

_NT = (((1,), (1,)), ((), ()))


def _dot(a, b):
    return jnp.dot(a, b, preferred_element_type=F32)


def _silu(x):
    return x / (1.0 + jnp.exp2(x * NEG_LOG2_E))


def _rms(x):
    return x * lax.rsqrt(jnp.mean(x * x, axis=-1, keepdims=True) + EPS)


def _params(vmem_mib, semantics):
    return pltpu.CompilerParams(dimension_semantics=semantics,
                                vmem_limit_bytes=vmem_mib * MIB)


def _ada_kernel(c_ref, w_ref, b_ref, o_ref):
    s = _silu(c_ref[...]).astype(BF16)
    o_ref[...] = _dot(s, w_ref[...].astype(BF16)) + b_ref[...]


def _ada(cc, w, b):
    rows, d = cc.shape
    n = w.shape[1]
    tn = 1024
    return pl.pallas_call(
        _ada_kernel,
        grid=(n // tn,),
        in_specs=[pl.BlockSpec((rows, d), lambda j: (0, 0)),
                  pl.BlockSpec((d, tn), lambda j: (0, j)),
                  pl.BlockSpec((1, tn), lambda j: (0, j))],
        out_specs=pl.BlockSpec((rows, tn), lambda j: (0, j)),
        out_shape=jax.ShapeDtypeStruct((rows, n), F32),
        compiler_params=_params(32, ("arbitrary",)),
        name="ada",
    )(cc, w, b)


def _inproj_kernel(x_ref, ctx_ref, g_ref, sh_ref, sc_ref, shc_ref, scc_ref, w_ref, wab_ref,
                   conv_ref, alog_ref, dtb_ref, pg_ref, kt_ref, gat_ref, h_s, w_s, p_s):
    j = pl.program_id(1)
    l, lc = x_ref.shape[0], ctx_ref.shape[0]
    blocks = [(x_ref, r, r, r + 8, sh_ref, sc_ref, min(INPROJ_ROWS, l - r)) for r in range(0, l, INPROJ_ROWS)]
    blocks += [(ctx_ref, r, l + r, l + r + 16, shc_ref, scc_ref, min(INPROJ_ROWS, lc - r))
               for r in range(0, lc, INPROJ_ROWS)]
    n_rb = len(blocks)

    w_s[...] = w_ref[...].astype(BF16)

    def out_rows(rb):
        return slice(blocks[rb][2], blocks[rb][2] + blocks[rb][6])

    def project(rb):
        return lax.dot_general(h_s[out_rows(rb), :], w_s[...], _NT, preferred_element_type=F32)

    def pipelined(first, second, lead):
        for rb in range(min(lead, n_rb)):
            first(rb)
        for rb in range(n_rb):
            if rb + lead < n_rb:
                first(rb + lead)
            second(rb)

    @pl.when(j == 0)
    def _u():
        for r in (0, l + 8, l + lc + 16):
            p_s[r:r + 8, :] = jnp.zeros((8, GROUP_W), F32)
        wabt = wab_ref[...].astype(BF16)
        row = lax.broadcasted_iota(jnp.int32, (N_GATES, PAIR), 0)

        def prologue(rb):
            src, r, dst, _, shift, scale, n = blocks[rb]
            gain = g_ref[...] * (1.0 + scale[...])
            hb = (_rms(src[r:r + n, :]) * gain + shift[...]).astype(BF16)
            h_s[dst:dst + n, :] = hb
            ab_t = lax.dot_general(hb, wabt, _NT, preferred_element_type=F32).T
            for s in range(n // PAIR):
                ab = ab_t[:N_GATES, s * PAIR:(s + 1) * PAIR]
                beta = 1.0 / (1.0 + jnp.exp(-ab))
                xs = ab + dtb_ref[...]
                softplus = jnp.maximum(xs, 0.0) + jnp.log1p(jnp.exp(-jnp.abs(xs)))
                gate = -jnp.exp(alog_ref[...]) * softplus
                gat_ref[dst // PAIR + s] = jnp.where(row < 8, beta, gate)

        def store_u(rb):
            pg_ref[out_rows(rb), :] = project(rb).astype(BF16)

        pipelined(prologue, store_u, 2)

    @pl.when(j == N_GROUPS - 1)
    def _z():
        for rb in range(n_rb):
            pg_ref[out_rows(rb), :] = _silu(project(rb)).astype(BF16)

    def conv_group(normalise, transpose):
        cw = conv_ref[...]

        def raw(rb):
            p0, n = blocks[rb][3], blocks[rb][6]
            p_s[p0:p0 + n, :] = project(rb)

        def finish(rb):
            p0, n = blocks[rb][3], blocks[rb][6]
            ext = p_s[p0 - 8:p0 + n + 8, :]
            prev = pltpu.roll(ext, 1, 0)[8:8 + n]
            nxt = pltpu.roll(ext, n + 15, 0)[8:8 + n]
            cur = ext[8:8 + n]
            a = _silu(prev * cw[0:1] + cur * cw[1:2] + nxt * cw[2:3])
            if normalise:
                segs = []
                for hh in range(HEADS):
                    seg = a[:, hh * HEAD_D:(hh + 1) * HEAD_D]
                    segs.append(seg * lax.rsqrt(jnp.sum(seg * seg, axis=-1, keepdims=True) + EPS))
                a = jnp.concatenate(segs, axis=1)
            pg_ref[out_rows(rb), :] = a.astype(BF16)
            if transpose:
                at = a.T.astype(BF16)
                for s in range(n // PAIR):
                    kt_ref[blocks[rb][2] // PAIR + s] = at[:, s * PAIR:(s + 1) * PAIR]

        pipelined(raw, finish, 2)

    @pl.when(j == 1)
    def _q():
        conv_group(True, False)

    @pl.when(j == 2)
    def _k():
        conv_group(True, True)

    @pl.when(j == 3)
    def _v():
        conv_group(False, False)


def _inproj(x, ctx, g, shift, scale, shift_c, scale_c, w_in_t, wabt, conv, alog, dtb):
    b, l, d = x.shape
    lt = l + ctx.shape[1]
    n_pairs = lt // PAIR
    vec = pl.BlockSpec((1, d), lambda i, j: (0, 0))
    mod = pl.BlockSpec((None, 1, d), lambda i, j: (i, 0, 0))
    return pl.pallas_call(
        _inproj_kernel,
        grid=(b, N_GROUPS),
        in_specs=[pl.BlockSpec((None, l, d), lambda i, j: (i, 0, 0)),
                  pl.BlockSpec((None, lt - l, d), lambda i, j: (i, 0, 0)),
                  vec, mod, mod, vec, vec,
                  pl.BlockSpec((GROUP_W, d), lambda i, j: (j, 0)),
                  pl.BlockSpec((PAIR, d), lambda i, j: (0, 0)),
                  pl.BlockSpec((3, GROUP_W), lambda i, j: (0, jnp.clip(j - 1, 0, 2))),
                  pl.BlockSpec((N_GATES, 1), lambda i, j: (0, 0)),
                  pl.BlockSpec((N_GATES, 1), lambda i, j: (0, 0))],
        out_specs=[pl.BlockSpec((None, lt, GROUP_W), lambda i, j: (i, 0, j)),
                   pl.BlockSpec((None, n_pairs, GROUP_W, PAIR), lambda i, j: (i, 0, 0, 0)),
                   pl.BlockSpec((None, n_pairs, N_GATES, PAIR), lambda i, j: (i, 0, 0, 0))],
        out_shape=[jax.ShapeDtypeStruct((b, lt, N_GROUPS * GROUP_W), BF16),
                   jax.ShapeDtypeStruct((b, n_pairs, GROUP_W, PAIR), BF16),
                   jax.ShapeDtypeStruct((b, n_pairs, N_GATES, PAIR), F32)],
        scratch_shapes=[pltpu.VMEM((lt, d), BF16),
                        pltpu.VMEM((GROUP_W, d), BF16),
                        pltpu.VMEM((lt + 24, GROUP_W), F32)],
        compiler_params=_params(52, ("arbitrary", "arbitrary")),
        name="inproj",
    )(x, ctx, g, shift, scale, shift_c, scale_c, w_in_t, wabt, conv, alog, dtb)


def _bd2(a, b):
    z = jnp.zeros_like(a)
    return jnp.concatenate([jnp.concatenate([a, z], axis=1),
                            jnp.concatenate([z, b], axis=1)], axis=0)


def _bd_of(x):
    xb = x.astype(BF16)
    return _bd2(xb[:, :HEAD_D], xb[:, HEAD_D:])


def _round_robin(*stage_generators):
    for _ in itertools.zip_longest(*stage_generators):
        pass


def _gdn_kernel(q_ref, k_ref, v_ref, kt_ref, gat_ref, g_ref, y_ref,
                u_s, wq_s, at_s, kdt_s, egl_s, s_s, o_s):
    lt, l = q_ref.shape[0], y_ref.shape[0]
    n_steps = lt // PAIR
    n_ctx = (lt - l) // PAIR
    per_iter = GDN_STEPS
    assert n_steps % per_iter == 0

    ri = lax.broadcasted_iota(jnp.int32, (PAIR, PAIR), 0)
    ci = lax.broadcasted_iota(jnp.int32, (PAIR, PAIR), 1)
    same = (ri // CHUNK) == (ci // CHUNK)
    incl = (jnp.where(same & (ri >= ci), 1.0, 0.0), jnp.where(same & (ri <= ci), 1.0, 0.0))
    strict = (jnp.where(same & (ri > ci), 1.0, 0.0), jnp.where(same & (ri < ci), 1.0, 0.0))
    tri = (incl[1].astype(BF16), incl[0].astype(BF16))

    def chunk_cumsum(x, d):
        p1 = x.astype(BF16)
        r1 = x - p1.astype(F32)
        p2 = r1.astype(BF16)
        p3 = (r1 - p2.astype(F32)).astype(BF16)
        n = x.shape[0]
        s = _dot(jnp.concatenate([p1, p2, p3], axis=0), tri[d])
        return s[:n] + s[n:2 * n] + s[2 * n:]
    eye = jnp.where(ri == ci, 1.0, 0.0)
    eye2 = jnp.concatenate([eye, eye], axis=1)
    off = ([], [])
    for lvl in range(6):
        b = 2 ** lvl
        joined = (ri // (2 * b)) == (ci // (2 * b))
        off[0].append(jnp.where(joined & (ri % (2 * b) >= b) & (ci % (2 * b) < b), 1.0, 0.0))
        off[1].append(jnp.where(joined & (ri % (2 * b) < b) & (ci % (2 * b) >= b), 1.0, 0.0))

    def joining_blocks(a, d, lvl):
        m = off[d][lvl]
        return jnp.concatenate([a[:, :HEAD_D] * m, a[:, HEAD_D:] * m], axis=1)

    lane = lax.broadcasted_iota(jnp.int32, (1, PAIR), 1)
    first_half = lane < CHUNK
    scale = HEAD_D ** -0.5
    lanes = [slice(hp * 2 * HEAD_D, (hp + 1) * 2 * HEAD_D) for hp in range(HEADS // 2)]
    stream_ids = [(hp, d) for hp in range(HEADS // 2) for d in range(2)]
    zeros_half = jnp.zeros((CHUNK, 2 * HEAD_D), BF16)

    def pair_of(step, d):
        if d == 0:
            return lax.rem(step + (n_steps - n_ctx), n_steps)
        return n_steps - 1 - step

    def prepare(slot, step0):
        streams = []
        for kk in range(per_iter):
            for d in range(2):
                p = pair_of(step0 + kk, d)
                t0 = pl.multiple_of(p * PAIR, PAIR)
                gates = gat_ref[p]
                kt_all = kt_ref[p]
                gc = chunk_cumsum(gates, d)
                for hp in range(HEADS // 2):
                    k2 = k_ref[pl.ds(t0, PAIR), lanes[hp]]
                    q2 = q_ref[pl.ds(t0, PAIR), lanes[hp]]
                    v2 = v_ref[pl.ds(t0, PAIR), lanes[hp]]
                    kt2 = kt_all[hp * 2 * HEAD_D:(hp + 1) * 2 * HEAD_D, :]
                    bd_kt = _bd2(kt2[:HEAD_D], kt2[HEAD_D:])
                    streams.append(dict(kk=kk, d=d, hp=hp, gates=gates, gc=gc, k2=k2, q2=q2, v2=v2,
                                        kt2=kt2, kk_=_dot(k2, bd_kt), qk_=_dot(q2, bd_kt)))
        yield
        for st in streams:
            kk, d, hp, gates, gc = st["kk"], st["d"], st["hp"], st["gates"], st["gc"]
            gc_t = gc.T
            a, attn, qd, kdt, erow = [], [], [], [], []
            for hh in range(2):
                h = 2 * hp + hh
                hs = slice(hh * HEAD_D, (hh + 1) * HEAD_D)
                brow = gates[4 * d + h:4 * d + h + 1, :]
                gcr = gc[8 + 4 * d + h:9 + 4 * d + h, :]
                gcc = gc_t[:, 8 + 4 * d + h:9 + 4 * d + h]
                decay = jnp.exp((gcc - gcr) * incl[d]) * incl[d]
                a.append(st["kk_"][:, hs] * decay * strict[d] * brow)
                attn.append(st["qk_"][:, hs] * decay * (brow * scale))
                qd.append(st["q2"][:, hs].astype(F32) * (jnp.exp(gcc) * scale))
                if d == 0:
                    gl0, gl1 = gcr[:, CHUNK - 1:CHUNK], gcr[:, PAIR - 1:PAIR]
                else:
                    gl0, gl1 = gcr[:, 0:1], gcr[:, CHUNK:CHUNK + 1]
                glr = jnp.where(first_half, gl0, gl1)
                kdt.append(st["kt2"][hs, :].astype(F32) * (jnp.exp(glr - gcr) * brow))
                erow.append(jnp.exp(gcr))
                egl_s[slot, d, kk, 0, h:h + 1, :] = jnp.broadcast_to(jnp.exp(gl0), (1, PAIR))
                egl_s[slot, d, kk, 1, h:h + 1, :] = jnp.broadcast_to(jnp.exp(gl1), (1, PAIR))
            st["a"] = jnp.concatenate(a, axis=1)
            st["erow"] = jnp.concatenate(erow, axis=1)
            at_s[slot, d, kk, :, lanes[hp]] = jnp.concatenate(attn, axis=1).astype(BF16)
            kdt_s[slot, d, kk, hp] = jnp.concatenate(kdt, axis=1).astype(BF16)
            st["qd"] = jnp.concatenate(qd, axis=1).astype(BF16)
            st["t"] = eye2 - joining_blocks(st["a"], d, 0)
        yield
        for lvl in range(1, 6):
            for st in streams:
                st["y"] = _dot(joining_blocks(st["a"], st["d"], lvl).astype(BF16), _bd_of(st["t"]))
            yield
            for st in streams:
                st["t"] = st["t"] - _dot(st["t"].astype(BF16), _bd_of(st["y"]))
            yield
        for st in streams:
            t = st["t"]
            st["u"] = _dot(t.astype(BF16), _bd2(st["v2"][:, :HEAD_D], st["v2"][:, HEAD_D:]))
            st["w"] = _dot((t * st["erow"]).astype(BF16), _bd2(st["k2"][:, :HEAD_D], st["k2"][:, HEAD_D:]))
        yield
        for st in streams:
            kk, d, hp = st["kk"], st["d"], st["hp"]
            u_s[slot, d, kk, :, lanes[hp]] = st["u"].astype(BF16)
            wb, qdb = st["w"].astype(BF16), st["qd"]
            wq_s[slot, d, kk, :, lanes[hp]] = jnp.concatenate(
                [wb[:CHUNK], qdb[:CHUNK], wb[CHUNK:], qdb[CHUNK:]], axis=0)
        yield

    def scan(slot, step0):
        state = {(hp, d): s_s[d, :, lanes[hp]] for (hp, d) in stream_ids}
        for kk in range(per_iter):
            for half in range(2):
                e = {0: half, 1: 1 - half}
                res, bd_x = {}, {}
                for (hp, d) in stream_ids:
                    res[(hp, d)] = _dot(wq_s[slot, d, kk, e[d] * PAIR:(e[d] + 1) * PAIR, lanes[hp]],
                                        _bd_of(state[(hp, d)]))
                yield
                for (hp, d) in stream_ids:
                    vn = (u_s[slot, d, kk, e[d] * CHUNK:(e[d] + 1) * CHUNK, lanes[hp]].astype(F32)
                          - res[(hp, d)][:CHUNK])
                    vnb = vn.astype(BF16)
                    x = jnp.concatenate([vnb, zeros_half] if e[d] == 0 else [zeros_half, vnb], axis=0)
                    bd_x[(hp, d)] = _bd2(x[:, :HEAD_D], x[:, HEAD_D:])
                for (hp, d) in stream_ids:
                    both = _dot(jnp.concatenate(
                        [kdt_s[slot, d, kk, hp],
                         at_s[slot, d, kk, e[d] * CHUNK:(e[d] + 1) * CHUNK, lanes[hp]]], axis=0),
                        bd_x[(hp, d)])
                    eg = jnp.concatenate([egl_s[slot, d, kk, e[d], 2 * hp:2 * hp + 1, :],
                                          egl_s[slot, d, kk, e[d], 2 * hp + 1:2 * hp + 2, :]], axis=1)
                    state[(hp, d)] = state[(hp, d)] * eg + both[:HEAD_D]
                    r0 = pl.multiple_of(pair_of(step0 + kk, d) * PAIR + e[d] * CHUNK, CHUNK)
                    o_s[pl.ds(r0, CHUNK), lanes[hp]] += res[(hp, d)][CHUNK:] + both[HEAD_D:]
                yield
        for (hp, d) in stream_ids:
            s_s[d, :, lanes[hp]] = state[(hp, d)]

    s_s[...] = jnp.zeros(s_s.shape, F32)
    o_s[...] = jnp.zeros(o_s.shape, F32)
    n_iter = n_steps // per_iter
    _round_robin(prepare(0, 0))

    def body(it, carry):
        _round_robin(prepare(lax.rem(it + 1, 2), (it + 1) * per_iter), scan(lax.rem(it, 2), it * per_iter))
        return carry

    lax.fori_loop(0, n_iter - 1, body, 0)
    _round_robin(scan((n_iter - 1) % 2, (n_iter - 1) * per_iter))

    def epilogue(rb, carry):
        r0 = pl.multiple_of(rb * ROW_BLOCK, ROW_BLOCK)
        o = o_s[pl.ds(r0, ROW_BLOCK), :]
        segs = [_rms(o[:, h * HEAD_D:(h + 1) * HEAD_D]) * g_ref[...] for h in range(HEADS)]
        y_ref[pl.ds(r0, ROW_BLOCK), :] = jnp.concatenate(segs, axis=1).astype(BF16)
        return carry

    lax.fori_loop(0, l // ROW_BLOCK, epilogue, 0)


def _gdn(pg, kt, gat, g_gdn, l):
    b, lt, _ = pg.shape
    n_pairs = lt // PAIR

    def group(g):
        return pl.BlockSpec((None, lt, GROUP_W), lambda i, g=g: (i, 0, g))

    ring = (2, 2, GDN_STEPS)
    return pl.pallas_call(
        _gdn_kernel,
        grid=(b,),
        in_specs=[group(1), group(2), group(3),
                  pl.BlockSpec((None, n_pairs, GROUP_W, PAIR), lambda i: (i, 0, 0, 0)),
                  pl.BlockSpec((None, n_pairs, N_GATES, PAIR), lambda i: (i, 0, 0, 0)),
                  pl.BlockSpec((1, HEAD_D), lambda i: (0, 0))],
        out_specs=pl.BlockSpec((None, l, GROUP_W), lambda i: (i, 0, 0)),
        out_shape=jax.ShapeDtypeStruct((b, l, GROUP_W), BF16),
        scratch_shapes=[pltpu.VMEM(ring + (PAIR, GROUP_W), BF16),
                        pltpu.VMEM(ring + (2 * PAIR, GROUP_W), BF16),
                        pltpu.VMEM(ring + (PAIR, GROUP_W), BF16),
                        pltpu.VMEM(ring + (HEADS // 2, HEAD_D, 2 * PAIR), BF16),
                        pltpu.VMEM(ring + (2, 8, PAIR), F32),
                        pltpu.VMEM((2, HEAD_D, GROUP_W), F32),
                        pltpu.VMEM((lt, GROUP_W), F32)],
        compiler_params=_params(48, ("arbitrary",)),
        name="gdn",
    )(pg, pg, pg, kt, gat, g_gdn)


def _fourier_w_kernel(cc_ref, sc_ref, w_ref, ch_ref, sh_ref, o_ref, chb_ref, shb_ref):
    hi = lax.Precision.HIGHEST
    for g in range(w_ref.shape[0]):
        w = w_ref[g]
        o_ref[g] = jnp.concatenate(
            [jnp.dot(cc_ref[...], w, precision=hi, preferred_element_type=F32),
             -jnp.dot(sc_ref[...], w, precision=hi, preferred_element_type=F32)], axis=1)
    chb_ref[...] = ch_ref[...].astype(BF16)
    shb_ref[...] = sh_ref[...].astype(BF16)


def _fourier_w(cc, sc, w_f, ch, sh):
    groups, c, d = w_f.shape
    return pl.pallas_call(
        _fourier_w_kernel,
        out_shape=[jax.ShapeDtypeStruct((groups, c, 2 * d), F32),
                   jax.ShapeDtypeStruct(ch.shape, BF16),
                   jax.ShapeDtypeStruct(sh.shape, BF16)],
        compiler_params=pltpu.CompilerParams(vmem_limit_bytes=32 * MIB),
        name="fourier_w",
    )(cc, sc, w_f, ch, sh)


@functools.lru_cache(maxsize=None)
def _dft_constants(length, channels):
    def cos_sin(n, keep):
        idx = np.arange(keep, dtype=np.int64)
        ang = (np.outer(idx, idx) % n).astype(np.float64) * (2.0 * np.pi / n)
        return np.cos(ang), np.sin(ang)
    c_l, s_l = cos_sin(length, length // 2)
    c_c, s_c = cos_sin(channels, channels)
    norm = 1.0 / math.sqrt(length * channels)
    return (jnp.asarray(c_l.astype(np.float32)), jnp.asarray(s_l.astype(np.float32)),
            jnp.asarray((c_c * norm).astype(np.float32)),
            jnp.asarray((s_c * norm).astype(np.float32)))


def _mix_kernel(u_ref, on_ref, z_ref, ch_ref, sh_ref, mcs_ref, wout_ref, x_ref, gpost_ref, gt_ref,
                gpre_ref, shift_ref, scale_ref, x1_ref, h2_ref,
                yf_s, mc_s, ms_s, wout_s, ve_s, vo_s, d_s):
    i, m = pl.program_id(0), pl.program_id(1)
    l = u_ref.shape[0]
    h = l // 2
    blk = PAIR
    n_blk = h // blk

    @pl.when((i == 0) & (m == 0))
    def _():
        wout_s[...] = wout_ref[...].astype(BF16)
        mc_s[...] = jnp.zeros(mc_s.shape, BF16)
        ms_s[...] = jnp.zeros(ms_s.shape, BF16)
        for g in range(HEADS):
            gs = slice(g * HEAD_D, (g + 1) * HEAD_D)
            mc_s[gs, gs] = mcs_ref[g][:, :HEAD_D].astype(BF16)
            ms_s[gs, gs] = mcs_ref[g][:, HEAD_D:].astype(BF16)

    @pl.when(m == 0)
    def _():
        ri = lax.broadcasted_iota(jnp.int32, (blk, 2 * blk), 0)
        ci = lax.broadcasted_iota(jnp.int32, (blk, 2 * blk), 1)
        rev = jnp.where(((ci < blk) & (ri + ci == blk)) | ((ri == 0) & (ci == blk)), 1.0, 0.0).astype(BF16)
        zero_blk = jnp.zeros((blk, GROUP_W), BF16)

        def reversed_block(ref, base, k, last):
            top = ref[base + (n_blk - 1 - k) * blk:base + (n_blk - k) * blk, :]
            bottom = last if k == 0 else ref[base + (n_blk - k) * blk:base + (n_blk - k + 1) * blk, :]
            return _dot(rev, jnp.concatenate([top, bottom], axis=0))

        for k in range(n_blk):
            low = u_ref[k * blk:(k + 1) * blk, :].astype(F32)
            mirrored = reversed_block(u_ref, h, k, zero_blk)
            yf_s[k * blk:(k + 1) * blk, :] = (low + mirrored).astype(BF16)
            yf_s[h + k * blk:h + (k + 1) * blk, :] = (low - mirrored).astype(BF16)
        ve_s[...] = _dot(yf_s[:h, :], mc_s[...]).astype(BF16)
        vo_s[...] = _dot(yf_s[h:, :], ms_s[...]).astype(BF16)
        sign_row = jnp.where(lax.broadcasted_iota(jnp.int32, (16, l), 1) % 2 == 0, 1.0, -1.0)
        sign_row = jnp.where(lax.broadcasted_iota(jnp.int32, (16, l), 0) == 0, sign_row, 0.0).astype(BF16)
        alt_u = _dot(sign_row, u_ref[...])
        y_nyq = _dot(alt_u.astype(BF16), mc_s[...])[0:1, :]
        v_nyq = _dot(u_ref[h:h + 16, :], mc_s[...])[0:1, :]
        sign_col = jnp.where(lax.broadcasted_iota(jnp.int32, (blk, GROUP_W), 0) % 2 == 0, 1.0, -1.0)
        first_row = lax.broadcasted_iota(jnp.int32, (blk, GROUP_W), 0) == 0
        d_s[h:h + blk, :] = jnp.where(first_row, y_nyq, 0.0).astype(BF16)
        nyq = sign_col * v_nyq
        tall = min(FFN_ROWS, h)
        nyq = jnp.concatenate([nyq] * (tall // blk), axis=0)
        for k in range(h // tall):
            rows = slice(k * tall, (k + 1) * tall)
            a = _dot(ch_ref[rows, :], ve_s[...]) + nyq
            b = _dot(sh_ref[rows, :], vo_s[...])
            yf_s[rows, :] = (a + b).astype(BF16)
            d_s[rows, :] = (a - b).astype(BF16)
        for k in range(n_blk):
            yf_s[h + k * blk:h + (k + 1) * blk, :] = reversed_block(
                d_s, 0, k, d_s[h:h + blk, :]).astype(BF16)

    tm = x_ref.shape[0]
    sub = min(FFN_ROWS, tm)
    subs = [slice(sb * sub, (sb + 1) * sub) for sb in range(tm // sub)]
    ys = []
    for sb, rs in enumerate(subs):
        r0 = pl.multiple_of(m * tm + sb * sub, sub)
        yd = (on_ref[rs, :].astype(F32) * z_ref[rs, :].astype(F32)).astype(BF16)
        ys.append(_dot(jnp.concatenate([yf_s[pl.ds(r0, sub), :], yd], axis=1), wout_s[...]))
    post_gain = gt_ref[...] * gpost_ref[...]
    pre_gain = gpre_ref[...] * (1.0 + scale_ref[...])
    for rs, y in zip(subs, ys):
        x1 = x_ref[rs, :] + _rms(y) * post_gain
        x1_ref[rs, :] = x1
        h2_ref[rs, :] = (_rms(x1) * pre_gain + shift_ref[...]).astype(BF16)


def _mix(pg, on, ch, sh, mcs, w_out, x, g_post, gt, g_pre, shift, scale):
    b, l, d = x.shape
    tm = MIX_TILE
    h = l // 2
    vec = pl.BlockSpec((1, d), lambda i, m: (0, 0))
    mod = pl.BlockSpec((None, 1, d), lambda i, m: (i, 0, 0))
    whole = lambda a: pl.BlockSpec(a.shape, lambda i, m: (0,) * a.ndim)
    return pl.pallas_call(
        _mix_kernel,
        grid=(b, l // tm),
        in_specs=[pl.BlockSpec((None, l, GROUP_W), lambda i, m: (i, 0, 0)),
                  pl.BlockSpec((None, tm, GROUP_W), lambda i, m: (i, m, 0)),
                  pl.BlockSpec((None, tm, GROUP_W), lambda i, m: (i, m, N_GROUPS - 1)),
                  whole(ch), whole(sh), whole(mcs), whole(w_out),
                  pl.BlockSpec((None, tm, d), lambda i, m: (i, m, 0)),
                  vec, mod, vec, mod, mod],
        out_specs=[pl.BlockSpec((None, tm, d), lambda i, m: (i, m, 0)),
                   pl.BlockSpec((None, tm, d), lambda i, m: (i, m, 0))],
        out_shape=[jax.ShapeDtypeStruct((b, l, d), F32),
                   jax.ShapeDtypeStruct((b, l, d), BF16)],
        scratch_shapes=[pltpu.VMEM((l, GROUP_W), BF16),
                        pltpu.VMEM((GROUP_W, GROUP_W), BF16),
                        pltpu.VMEM((GROUP_W, GROUP_W), BF16),
                        pltpu.VMEM(w_out.shape, BF16),
                        pltpu.VMEM((h, GROUP_W), BF16),
                        pltpu.VMEM((h, GROUP_W), BF16),
                        pltpu.VMEM((h + PAIR, GROUP_W), BF16)],
        compiler_params=_params(48, ("arbitrary", "arbitrary")),
        name="mix",
    )(pg, on, pg, ch, sh, mcs, w_out, x, g_post, gt, g_pre, shift, scale)


def _ffn_kernel(h_ref, x1_ref, wv_ref, wg_ref, dwc_ref, wd_ref, gpost_ref, gt_ref,
                o_ref, gate_s, val_s, wvg_s, wd_s):
    ROW_BLOCK = FFN_ROWS
    j = pl.program_id(1)
    l = h_ref.shape[0]
    tf = wv_ref.shape[1]
    n_rb = l // ROW_BLOCK
    pad = GRID_W

    wvg_s[:, :tf] = wv_ref[...].astype(BF16)
    wvg_s[:, tf:] = wg_ref[...].astype(BF16)
    wd_s[...] = wd_ref[...].astype(BF16)
    gate_s[0:pad, :] = jnp.zeros((pad, tf), F32)
    gate_s[l + pad:l + 2 * pad, :] = jnp.zeros((pad, tf), F32)

    n_ext = ROW_BLOCK + 2 * pad
    col = lax.broadcasted_iota(jnp.int32, (n_ext, tf), 0) % GRID_W
    not_first = jnp.where(col == 0, 0.0, 1.0)
    not_last = jnp.where(col == GRID_W - 1, 0.0, 1.0)
    dw = dwc_ref[...]

    def up(i):
        r0 = i * ROW_BLOCK
        vg = _dot(h_ref[r0:r0 + ROW_BLOCK, :], wvg_s[...])
        val_s[r0:r0 + ROW_BLOCK, :] = vg[:, :tf]
        gate_s[r0 + pad:r0 + pad + ROW_BLOCK, :] = vg[:, tf:]

    def down(i, first_tile):
        r0 = i * ROW_BLOCK
        ext = gate_s[r0:r0 + n_ext, :]
        shifted = (pltpu.roll(ext, 1, 0) * not_first, ext, pltpu.roll(ext, n_ext - 1, 0) * not_last)
        conv = None
        for dy in range(3):
            for dx in range(3):
                term = shifted[dx][dy * pad:dy * pad + ROW_BLOCK] * dw[3 * dy + dx:3 * dy + dx + 1]
                conv = term if conv is None else conv + term
        act = (_silu(conv) * val_s[r0:r0 + ROW_BLOCK, :]).astype(BF16)
        if first_tile:
            o_ref[r0:r0 + ROW_BLOCK, :] = _dot(act, wd_s[...])
        else:
            o_ref[r0:r0 + ROW_BLOCK, :] += _dot(act, wd_s[...])

    def tile(first_tile):
        for i in range(min(2, n_rb)):
            up(i)
        for i in range(n_rb):
            if i + 2 < n_rb:
                up(i + 2)
            down(i, first_tile)

    @pl.when(j == 0)
    def _():
        tile(True)

    @pl.when(j > 0)
    def _():
        tile(False)

    @pl.when(j == pl.num_programs(1) - 1)
    def _():
        def fin(rb, carry):
            r0 = pl.multiple_of(rb * ROW_BLOCK, ROW_BLOCK)
            y = _rms(o_ref[pl.ds(r0, ROW_BLOCK), :]) * (gt_ref[...] * gpost_ref[...])
            o_ref[pl.ds(r0, ROW_BLOCK), :] = x1_ref[pl.ds(r0, ROW_BLOCK), :] + y
            return carry
        lax.fori_loop(0, n_rb, fin, 0)


def _ffn(h2, x1, w_up, dwc, w_down, g_post, gt):
    b, l, d = x1.shape
    d_ff = w_down.shape[0]
    tf = FFN_TILE
    n_f = d_ff // tf
    return pl.pallas_call(
        _ffn_kernel,
        grid=(b, n_f),
        in_specs=[pl.BlockSpec((None, l, d), lambda i, j: (i, 0, 0)),
                  pl.BlockSpec((None, l, d), lambda i, j: (i, 0, 0)),
                  pl.BlockSpec((d, tf), lambda i, j: (0, j)),
                  pl.BlockSpec((d, tf), lambda i, j: (0, n_f + j)),
                  pl.BlockSpec((9, tf), lambda i, j: (0, j)),
                  pl.BlockSpec((tf, d), lambda i, j: (j, 0)),
                  pl.BlockSpec((1, d), lambda i, j: (0, 0)),
                  pl.BlockSpec((None, 1, d), lambda i, j: (i, 0, 0))],
        out_specs=pl.BlockSpec((None, l, d), lambda i, j: (i, 0, 0)),
        out_shape=jax.ShapeDtypeStruct((b, l, d), F32),
        scratch_shapes=[pltpu.VMEM((l + 2 * GRID_W, tf), F32),
                        pltpu.VMEM((l, tf), F32),
                        pltpu.VMEM((d, 2 * tf), BF16),
                        pltpu.VMEM((tf, d), BF16)],
        compiler_params=_params(58, ("arbitrary", "arbitrary")),
        name="ffn",
    )(h2, x1, w_up, w_up, dwc, w_down, g_post, gt)


def kernel(x, c, ctx, c_ctx, w_ada, b_ada, g_pre_mix, g_post_mix, g_pre_ffn, g_post_ffn,
           w_in, w_qkv_conv, a_log, dt_bias, g_gdn, w_fourier, w_out, w_up, w_dwc, w_down):
    b, l, d = x.shape
    assert w_ada.shape[0] == 1, "single-layer stack"
    assert b < 16 and l % MIX_TILE == 0 and ctx.shape[1] % ROW_BLOCK == 0
    assert w_in.shape[2] == N_GROUPS * GROUP_W + N_GATES and d == 2 * GROUP_W

    cc = jnp.zeros((16, d), F32).at[:b].set(c).at[b].set(c_ctx)
    mod = _ada(cc, w_ada[0], b_ada)
    sh1, sc1, gt1, sh2, sc2, gt2 = [m[:, None, :] for m in jnp.split(mod[:b], 6, axis=-1)]
    sh1c, sc1c = [m[None, :] for m in jnp.split(mod[b], 6, axis=-1)[:2]]

    w_in_t = jnp.swapaxes(w_in, 1, 2)[0]
    wabt = jnp.pad(w_in_t[N_GROUPS * GROUP_W:], ((0, PAIR - N_GATES), (0, 0)))
    zeros8 = jnp.zeros((8,), F32)
    alog = jnp.concatenate([zeros8, a_log[0].reshape(-1)])[:, None]
    dtb = jnp.concatenate([zeros8, dt_bias[0].reshape(-1)])[:, None]

    pg, kt, gat = _inproj(x, ctx, g_pre_mix, sh1, sc1, sh1c, sc1c, w_in_t, wabt, w_qkv_conv[0], alog, dtb)
    on = _gdn(pg, kt, gat, g_gdn, l)

    ch, sh, c_c, s_c = _dft_constants(l, HEAD_D)
    mcs, ch, sh = _fourier_w(c_c, s_c, w_fourier[0], ch, sh)
    x1, h2 = _mix(pg, on, ch, sh, mcs, w_out[0], x, g_post_mix, gt1, g_pre_ffn, sh2, sc2)
    d_ff = w_down.shape[1]
    return _ffn(h2, x1, w_up[0], w_dwc[0].reshape(9, d_ff), w_down[0], g_post_ffn, gt2)
```

```python
import functools
import itertools
import math

import numpy as np
import jax
import jax.numpy as jnp
from jax import lax
from jax.experimental import pallas as pl
from jax.experimental.pallas import tpu as pltpu

F32 = jnp.float32
BF16 = jnp.bfloat16

GRID_W = 64
HEADS = 4
HEAD_D = 128
CHUNK = 64
PAIR = 2 * CHUNK
GROUP_W = HEADS * HEAD_D
N_GROUPS = 5
N_GATES = 16
EPS = 1e-6
NEG_LOG2_E = -math.log2(math.e)
ROW_BLOCK = 256
FFN_TILE = 256
MIX_TILE = 1024
FFN_ROWS = 512
GDN_STEPS = 3
INPROJ_ROWS = 512
MIB = 1024 * 1024

_NT = (((1,), (1,)), ((), ()))


def _dot(a, b):
    return jnp.dot(a, b, preferred_element_type=F32)


def _silu(x):
    return x / (1.0 + jnp.exp2(x * NEG_LOG2_E))


def _rms(x):
    return x * lax.rsqrt(jnp.mean(x * x, axis=-1, keepdims=True) + EPS)


def _params(vmem_mib, semantics):
    return pltpu.CompilerParams(dimension_semantics=semantics,
                                vmem_limit_bytes=vmem_mib * MIB)


def _ada_kernel(c_ref, w_ref, b_ref, o_ref):
    s = _silu(c_ref[...]).astype(BF16)
    o_ref[...] = _dot(s, w_ref[...].astype(BF16)) + b_ref[...]


def _ada(cc, w, b):
    rows, d = cc.shape
    n = w.shape[1]
    tn = 1024
    return pl.pallas_call(
        _ada_kernel,
        grid=(n // tn,),
        in_specs=[pl.BlockSpec((rows, d), lambda j: (0, 0)),
                  pl.BlockSpec((d, tn), lambda j: (0, j)),
                  pl.BlockSpec((1, tn), lambda j: (0, j))],
        out_specs=pl.BlockSpec((rows, tn), lambda j: (0, j)),
        out_shape=jax.ShapeDtypeStruct((rows, n), F32),
        compiler_params=_params(32, ("arbitrary",)),
        name="ada",
    )(cc, w, b)


def _inproj_kernel(n_x, *refs):
    x_refs = refs[:n_x]
    (ctx_ref, g_ref, sh_ref, sc_ref, shc_ref, scc_ref, w_ref, wab_ref,
     conv_ref, alog_ref, dtb_ref, pg_ref, kt_ref, gat_ref, h_s, w_s, p_s) = refs[n_x:]
    j = pl.program_id(1)
    l, lc = n_x * INPROJ_ROWS, ctx_ref.shape[0]
    blocks = [(x_refs[k], 0, k * INPROJ_ROWS, k * INPROJ_ROWS + 8, sh_ref, sc_ref, INPROJ_ROWS) for k in range(n_x)]
    blocks += [(ctx_ref, r, l + r, l + r + 16, shc_ref, scc_ref, min(INPROJ_ROWS, lc - r))
               for r in range(0, lc, INPROJ_ROWS)]
    n_rb = len(blocks)

    w_s[...] = w_ref[...].astype(BF16)

    def out_rows(rb):
        return slice(blocks[rb][2], blocks[rb][2] + blocks[rb][6])

    def project(rb):
        return lax.dot_general(h_s[out_rows(rb), :], w_s[...], _NT, preferred_element_type=F32)

    def pipelined(first, second, lead):
        for rb in range(min(lead, n_rb)):
            first(rb)
        for rb in range(n_rb):
            if rb + lead < n_rb:
                first(rb + lead)
            second(rb)

    @pl.when(j == 0)
    def _u():
        for r in (0, l + 8, l + lc + 16):
            p_s[r:r + 8, :] = jnp.zeros((8, GROUP_W), F32)
        wabt = wab_ref[...].astype(BF16)
        row = lax.broadcasted_iota(jnp.int32, (N_GATES, PAIR), 0)

        def prologue(rb):
            src, r, dst, _, shift, scale, n = blocks[rb]
            gain = g_ref[...] * (1.0 + scale[...])
            hb = (_rms(src[r:r + n, :]) * gain + shift[...]).astype(BF16)
            h_s[dst:dst + n, :] = hb
            ab_t = lax.dot_general(hb, wabt, _NT, preferred_element_type=F32).T
            for s in range(n // PAIR):
                ab = ab_t[:N_GATES, s * PAIR:(s + 1) * PAIR]
                beta = 1.0 / (1.0 + jnp.exp(-ab))
                xs = ab + dtb_ref[...]
                softplus = jnp.maximum(xs, 0.0) + jnp.log1p(jnp.exp(-jnp.abs(xs)))
                gate = -jnp.exp(alog_ref[...]) * softplus
                gat_ref[dst // PAIR + s] = jnp.where(row < 8, beta, gate)

        def store_u(rb):
            pg_ref[out_rows(rb), :] = project(rb).astype(BF16)

        pipelined(prologue, store_u, 2)

    @pl.when(j == N_GROUPS - 1)
    def _z():
        for rb in range(n_rb):
            pg_ref[out_rows(rb), :] = _silu(project(rb)).astype(BF16)

    def conv_group(normalise, transpose):
        cw = conv_ref[...]

        def raw(rb):
            p0, n = blocks[rb][3], blocks[rb][6]
            p_s[p0:p0 + n, :] = project(rb)

        def finish(rb):
            p0, n = blocks[rb][3], blocks[rb][6]
            ext = p_s[p0 - 8:p0 + n + 8, :]
            prev = pltpu.roll(ext, 1, 0)[8:8 + n]
            nxt = pltpu.roll(ext, n + 15, 0)[8:8 + n]
            cur = ext[8:8 + n]
            a = _silu(prev * cw[0:1] + cur * cw[1:2] + nxt * cw[2:3])
            if normalise:
                segs = []
                for hh in range(HEADS):
                    seg = a[:, hh * HEAD_D:(hh + 1) * HEAD_D]
                    segs.append(seg * lax.rsqrt(jnp.sum(seg * seg, axis=-1, keepdims=True) + EPS))
                a = jnp.concatenate(segs, axis=1)
            pg_ref[out_rows(rb), :] = a.astype(BF16)
            if transpose:
                at = a.T.astype(BF16)
                for s in range(n // PAIR):
                    kt_ref[blocks[rb][2] // PAIR + s] = at[:, s * PAIR:(s + 1) * PAIR]

        pipelined(raw, finish, 2)

    @pl.when(j == 1)
    def _q():
        conv_group(True, False)

    @pl.when(j == 2)
    def _k():
        conv_group(True, True)

    @pl.when(j == 3)
    def _v():
        conv_group(False, False)


def _inproj(x, ctx, g, shift, scale, shift_c, scale_c, w_in_t, wabt, conv, alog, dtb):
    b, l, d = x.shape
    lt = l + ctx.shape[1]
    n_pairs = lt // PAIR
    vec = pl.BlockSpec((1, d), lambda i, j: (0, 0))
    mod = pl.BlockSpec((None, 1, d), lambda i, j: (i, 0, 0))
    n_x = l // INPROJ_ROWS

    def x_block(q):
        turn = min(q, N_GROUPS - 2)
        return pl.BlockSpec((None, INPROJ_ROWS, d),
                            lambda i, j: (jnp.minimum(i + (j > turn).astype(jnp.int32), b - 1), q, 0))

    return pl.pallas_call(
        functools.partial(_inproj_kernel, n_x),
        grid=(b, N_GROUPS),
        in_specs=[x_block(q) for q in range(n_x)] + [
                  pl.BlockSpec((None, lt - l, d), lambda i, j: (i, 0, 0)),
                  vec, mod, mod, vec, vec,
                  pl.BlockSpec((GROUP_W, d), lambda i, j: (j, 0)),
                  pl.BlockSpec((PAIR, d), lambda i, j: (0, 0)),
                  pl.BlockSpec((3, GROUP_W), lambda i, j: (0, jnp.clip(j - 1, 0, 2))),
                  pl.BlockSpec((N_GATES, 1), lambda i, j: (0, 0)),
                  pl.BlockSpec((N_GATES, 1), lambda i, j: (0, 0))],
        out_specs=[pl.BlockSpec((None, lt, GROUP_W), lambda i, j: (i, 0, j)),
                   pl.BlockSpec((None, n_pairs, GROUP_W, PAIR), lambda i, j: (i, 0, 0, 0)),
                   pl.BlockSpec((None, n_pairs, N_GATES, PAIR), lambda i, j: (i, 0, 0, 0))],
        out_shape=[jax.ShapeDtypeStruct((b, lt, N_GROUPS * GROUP_W), BF16),
                   jax.ShapeDtypeStruct((b, n_pairs, GROUP_W, PAIR), BF16),
                   jax.ShapeDtypeStruct((b, n_pairs, N_GATES, PAIR), F32)],
        scratch_shapes=[pltpu.VMEM((lt, d), BF16),
                        pltpu.VMEM((GROUP_W, d), BF16),
                        pltpu.VMEM((lt + 24, GROUP_W), F32)],
        compiler_params=_params(52, ("arbitrary", "arbitrary")),
        name="inproj",
    )(*([x] * n_x), ctx, g, shift, scale, shift_c, scale_c, w_in_t, wabt, conv, alog, dtb)


def _bd2(a, b):
    z = jnp.zeros_like(a)
    return jnp.concatenate([jnp.concatenate([a, z], axis=1),
                            jnp.concatenate([z, b], axis=1)], axis=0)


def _bd_of(x):
    xb = x.astype(BF16)
    return _bd2(xb[:, :HEAD_D], xb[:, HEAD_D:])


def _round_robin(*stage_generators):
    for _ in itertools.zip_longest(*stage_generators):
        pass


def _gdn_kernel(q_ref, k_ref, v_ref, kt_ref, gat_ref, g_ref, y_ref,
                u_s, wq_s, at_s, kdt_s, egl_s, s_s, o_s):
    lt, l = q_ref.shape[0], y_ref.shape[0]
    n_steps = lt // PAIR
    n_ctx = (lt - l) // PAIR
    per_iter = GDN_STEPS
    assert n_steps % per_iter == 0

    ri = lax.broadcasted_iota(jnp.int32, (PAIR, PAIR), 0)
    ci = lax.broadcasted_iota(jnp.int32, (PAIR, PAIR), 1)
    same = (ri // CHUNK) == (ci // CHUNK)
    incl = (jnp.where(same & (ri >= ci), 1.0, 0.0), jnp.where(same & (ri <= ci), 1.0, 0.0))
    strict = (jnp.where(same & (ri > ci), 1.0, 0.0), jnp.where(same & (ri < ci), 1.0, 0.0))
    tri = (incl[1].astype(BF16), incl[0].astype(BF16))

    def chunk_cumsum(x, d):
        p1 = x.astype(BF16)
        r1 = x - p1.astype(F32)
        p2 = r1.astype(BF16)
        p3 = (r1 - p2.astype(F32)).astype(BF16)
        n = x.shape[0]
        s = _dot(jnp.concatenate([p1, p2, p3], axis=0), tri[d])
        return s[:n] + s[n:2 * n] + s[2 * n:]
    eye = jnp.where(ri == ci, 1.0, 0.0)
    eye2 = jnp.concatenate([eye, eye], axis=1)
    off = ([], [])
    for lvl in range(6):
        b = 2 ** lvl
        joined = (ri // (2 * b)) == (ci // (2 * b))
        off[0].append(jnp.where(joined & (ri % (2 * b) >= b) & (ci % (2 * b) < b), 1.0, 0.0))
        off[1].append(jnp.where(joined & (ri % (2 * b) < b) & (ci % (2 * b) >= b), 1.0, 0.0))

    def joining_blocks(a, d, lvl):
        m = off[d][lvl]
        return jnp.concatenate([a[:, :HEAD_D] * m, a[:, HEAD_D:] * m], axis=1)

    lane = lax.broadcasted_iota(jnp.int32, (1, PAIR), 1)
    first_half = lane < CHUNK
    scale = HEAD_D ** -0.5
    lanes = [slice(hp * 2 * HEAD_D, (hp + 1) * 2 * HEAD_D) for hp in range(HEADS // 2)]
    stream_ids = [(hp, d) for hp in range(HEADS // 2) for d in range(2)]
    zeros_half = jnp.zeros((CHUNK, 2 * HEAD_D), BF16)

    def pair_of(step, d):
        if d == 0:
            return lax.rem(step + (n_steps - n_ctx), n_steps)
        return n_steps - 1 - step

    def prepare(slot, step0):
        streams = []
        for kk in range(per_iter):
            for d in range(2):
                p = pair_of(step0 + kk, d)
                t0 = pl.multiple_of(p * PAIR, PAIR)
                gates = gat_ref[p]
                kt_all = kt_ref[p]
                gc = chunk_cumsum(gates, d)
                for hp in range(HEADS // 2):
                    k2 = k_ref[pl.ds(t0, PAIR), lanes[hp]]
                    q2 = q_ref[pl.ds(t0, PAIR), lanes[hp]]
                    v2 = v_ref[pl.ds(t0, PAIR), lanes[hp]]
                    kt2 = kt_all[hp * 2 * HEAD_D:(hp + 1) * 2 * HEAD_D, :]
                    bd_kt = _bd2(kt2[:HEAD_D], kt2[HEAD_D:])
                    streams.append(dict(kk=kk, d=d, hp=hp, gates=gates, gc=gc, k2=k2, q2=q2, v2=v2,
                                        kt2=kt2, kk_=_dot(k2, bd_kt), qk_=_dot(q2, bd_kt)))
        yield
        for st in streams:
            kk, d, hp, gates, gc = st["kk"], st["d"], st["hp"], st["gates"], st["gc"]
            gc_t = gc.T
            a, attn, qd, kdt, erow = [], [], [], [], []
            for hh in range(2):
                h = 2 * hp + hh
                hs = slice(hh * HEAD_D, (hh + 1) * HEAD_D)
                brow = gates[4 * d + h:4 * d + h + 1, :]
                gcr = gc[8 + 4 * d + h:9 + 4 * d + h, :]
                gcc = gc_t[:, 8 + 4 * d + h:9 + 4 * d + h]
                decay = jnp.exp((gcc - gcr) * incl[d]) * incl[d]
                a.append(st["kk_"][:, hs] * decay * strict[d] * brow)
                attn.append(st["qk_"][:, hs] * decay * (brow * scale))
                qd.append(st["q2"][:, hs].astype(F32) * (jnp.exp(gcc) * scale))
                if d == 0:
                    gl0, gl1 = gcr[:, CHUNK - 1:CHUNK], gcr[:, PAIR - 1:PAIR]
                else:
                    gl0, gl1 = gcr[:, 0:1], gcr[:, CHUNK:CHUNK + 1]
                glr = jnp.where(first_half, gl0, gl1)
                kdt.append(st["kt2"][hs, :].astype(F32) * (jnp.exp(glr - gcr) * brow))
                erow.append(jnp.exp(gcr))
                egl_s[slot, d, kk, 0, h:h + 1, :] = jnp.broadcast_to(jnp.exp(gl0), (1, PAIR))
                egl_s[slot, d, kk, 1, h:h + 1, :] = jnp.broadcast_to(jnp.exp(gl1), (1, PAIR))
            st["a"] = jnp.concatenate(a, axis=1)
            st["erow"] = jnp.concatenate(erow, axis=1)
            at_s[slot, d, kk, :, lanes[hp]] = jnp.concatenate(attn, axis=1).astype(BF16)
            kdt_s[slot, d, kk, hp] = jnp.concatenate(kdt, axis=1).astype(BF16)
            st["qd"] = jnp.concatenate(qd, axis=1).astype(BF16)
            st["t"] = eye2 - joining_blocks(st["a"], d, 0)
        yield
        for lvl in range(1, 6):
            for st in streams:
                st["y"] = _dot(joining_blocks(st["a"], st["d"], lvl).astype(BF16), _bd_of(st["t"]))
            yield
            for st in streams:
                st["t"] = st["t"] - _dot(st["t"].astype(BF16), _bd_of(st["y"]))
            yield
        for st in streams:
            t = st["t"]
            st["u"] = _dot(t.astype(BF16), _bd2(st["v2"][:, :HEAD_D], st["v2"][:, HEAD_D:]))
            st["w"] = _dot((t * st["erow"]).astype(BF16), _bd2(st["k2"][:, :HEAD_D], st["k2"][:, HEAD_D:]))
        yield
        for st in streams:
            kk, d, hp = st["kk"], st["d"], st["hp"]
            u_s[slot, d, kk, :, lanes[hp]] = st["u"].astype(BF16)
            wb, qdb = st["w"].astype(BF16), st["qd"]
            wq_s[slot, d, kk, :, lanes[hp]] = jnp.concatenate(
                [wb[:CHUNK], qdb[:CHUNK], wb[CHUNK:], qdb[CHUNK:]], axis=0)
        yield

    def scan(slot, step0):
        state = {(hp, d): s_s[d, :, lanes[hp]] for (hp, d) in stream_ids}
        for kk in range(per_iter):
            for half in range(2):
                e = {0: half, 1: 1 - half}
                res, bd_x = {}, {}
                for (hp, d) in stream_ids:
                    res[(hp, d)] = _dot(wq_s[slot, d, kk, e[d] * PAIR:(e[d] + 1) * PAIR, lanes[hp]],
                                        _bd_of(state[(hp, d)]))
                yield
                for (hp, d) in stream_ids:
                    vn = (u_s[slot, d, kk, e[d] * CHUNK:(e[d] + 1) * CHUNK, lanes[hp]].astype(F32)
                          - res[(hp, d)][:CHUNK])
                    vnb = vn.astype(BF16)
                    x = jnp.concatenate([vnb, zeros_half] if e[d] == 0 else [zeros_half, vnb], axis=0)
                    bd_x[(hp, d)] = _bd2(x[:, :HEAD_D], x[:, HEAD_D:])
                for (hp, d) in stream_ids:
                    both = _dot(jnp.concatenate(
                        [kdt_s[slot, d, kk, hp],
                         at_s[slot, d, kk, e[d] * CHUNK:(e[d] + 1) * CHUNK, lanes[hp]]], axis=0),
                        bd_x[(hp, d)])
                    eg = jnp.concatenate([egl_s[slot, d, kk, e[d], 2 * hp:2 * hp + 1, :],
                                          egl_s[slot, d, kk, e[d], 2 * hp + 1:2 * hp + 2, :]], axis=1)
                    state[(hp, d)] = state[(hp, d)] * eg + both[:HEAD_D]
                    r0 = pl.multiple_of(pair_of(step0 + kk, d) * PAIR + e[d] * CHUNK, CHUNK)
                    o_s[pl.ds(r0, CHUNK), lanes[hp]] += res[(hp, d)][CHUNK:] + both[HEAD_D:]
                yield
        for (hp, d) in stream_ids:
            s_s[d, :, lanes[hp]] = state[(hp, d)]

    s_s[...] = jnp.zeros(s_s.shape, F32)
    o_s[...] = jnp.zeros(o_s.shape, F32)
    n_iter = n_steps // per_iter
    _round_robin(prepare(0, 0))

    def body(it, carry):
        _round_robin(prepare(lax.rem(it + 1, 2), (it + 1) * per_iter), scan(lax.rem(it, 2), it * per_iter))
        return carry

    lax.fori_loop(0, n_iter - 1, body, 0)
    _round_robin(scan((n_iter - 1) % 2, (n_iter - 1) * per_iter))

    def epilogue(rb, carry):
        r0 = pl.multiple_of(rb * ROW_BLOCK, ROW_BLOCK)
        o = o_s[pl.ds(r0, ROW_BLOCK), :]
        segs = [_rms(o[:, h * HEAD_D:(h + 1) * HEAD_D]) * g_ref[...] for h in range(HEADS)]
        y_ref[pl.ds(r0, ROW_BLOCK), :] = jnp.concatenate(segs, axis=1).astype(BF16)
        return carry

    lax.fori_loop(0, l // ROW_BLOCK, epilogue, 0)


def _gdn(pg, kt, gat, g_gdn, l):
    b, lt, _ = pg.shape
    n_pairs = lt // PAIR

    def group(g):
        return pl.BlockSpec((None, lt, GROUP_W), lambda i, g=g: (i, 0, g))

    ring = (2, 2, GDN_STEPS)
    return pl.pallas_call(
        _gdn_kernel,
        grid=(b,),
        in_specs=[group(1), group(2), group(3),
                  pl.BlockSpec((None, n_pairs, GROUP_W, PAIR), lambda i: (i, 0, 0, 0)),
                  pl.BlockSpec((None, n_pairs, N_GATES, PAIR), lambda i: (i, 0, 0, 0)),
                  pl.BlockSpec((1, HEAD_D), lambda i: (0, 0))],
        out_specs=pl.BlockSpec((None, l, GROUP_W), lambda i: (i, 0, 0)),
        out_shape=jax.ShapeDtypeStruct((b, l, GROUP_W), BF16),
        scratch_shapes=[pltpu.VMEM(ring + (PAIR, GROUP_W), BF16),
                        pltpu.VMEM(ring + (2 * PAIR, GROUP_W), BF16),
                        pltpu.VMEM(ring + (PAIR, GROUP_W), BF16),
                        pltpu.VMEM(ring + (HEADS // 2, HEAD_D, 2 * PAIR), BF16),
                        pltpu.VMEM(ring + (2, 8, PAIR), F32),
                        pltpu.VMEM((2, HEAD_D, GROUP_W), F32),
                        pltpu.VMEM((lt, GROUP_W), F32)],
        compiler_params=_params(48, ("arbitrary",)),
        name="gdn",
    )(pg, pg, pg, kt, gat, g_gdn)


def _fourier_w_kernel(cc_ref, sc_ref, w_ref, ch_ref, sh_ref, o_ref, chb_ref, shb_ref):
    hi = lax.Precision.HIGHEST
    for g in range(w_ref.shape[0]):
        w = w_ref[g]
        o_ref[g] = jnp.concatenate(
            [jnp.dot(cc_ref[...], w, precision=hi, preferred_element_type=F32),
             -jnp.dot(sc_ref[...], w, precision=hi, preferred_element_type=F32)], axis=1)
    chb_ref[...] = ch_ref[...].astype(BF16)
    shb_ref[...] = sh_ref[...].astype(BF16)


def _fourier_w(cc, sc, w_f, ch, sh):
    groups, c, d = w_f.shape
    return pl.pallas_call(
        _fourier_w_kernel,
        out_shape=[jax.ShapeDtypeStruct((groups, c, 2 * d), F32),
                   jax.ShapeDtypeStruct(ch.shape, BF16),
                   jax.ShapeDtypeStruct(sh.shape, BF16)],
        compiler_params=pltpu.CompilerParams(vmem_limit_bytes=32 * MIB),
        name="fourier_w",
    )(cc, sc, w_f, ch, sh)


@functools.lru_cache(maxsize=None)
def _dft_constants(length, channels):
    def cos_sin(n, keep):
        idx = np.arange(keep, dtype=np.int64)
        ang = (np.outer(idx, idx) % n).astype(np.float64) * (2.0 * np.pi / n)
        return np.cos(ang), np.sin(ang)
    c_l, s_l = cos_sin(length, length // 2)
    c_c, s_c = cos_sin(channels, channels)
    norm = 1.0 / math.sqrt(length * channels)
    return (jnp.asarray(c_l.astype(np.float32)), jnp.asarray(s_l.astype(np.float32)),
            jnp.asarray((c_c * norm).astype(np.float32)),
            jnp.asarray((s_c * norm).astype(np.float32)))


def _mix_kernel(u_ref, on_ref, z_ref, ch_ref, sh_ref, mcs_ref, wout_ref, x_ref, gpost_ref, gt_ref,
                gpre_ref, shift_ref, scale_ref, x1_ref, h2_ref,
                yf_s, mc_s, ms_s, wout_s, ve_s, vo_s, d_s):
    i, m = pl.program_id(0), pl.program_id(1)
    l = u_ref.shape[0]
    h = l // 2
    blk = PAIR
    n_blk = h // blk

    @pl.when((i == 0) & (m == 0))
    def _():
        wout_s[...] = wout_ref[...].astype(BF16)
        mc_s[...] = jnp.zeros(mc_s.shape, BF16)
        ms_s[...] = jnp.zeros(ms_s.shape, BF16)
        for g in range(HEADS):
            gs = slice(g * HEAD_D, (g + 1) * HEAD_D)
            mc_s[gs, gs] = mcs_ref[g][:, :HEAD_D].astype(BF16)
            ms_s[gs, gs] = mcs_ref[g][:, HEAD_D:].astype(BF16)

    @pl.when(m == 0)
    def _():
        ri = lax.broadcasted_iota(jnp.int32, (blk, 2 * blk), 0)
        ci = lax.broadcasted_iota(jnp.int32, (blk, 2 * blk), 1)
        rev = jnp.where(((ci < blk) & (ri + ci == blk)) | ((ri == 0) & (ci == blk)), 1.0, 0.0).astype(BF16)
        zero_blk = jnp.zeros((blk, GROUP_W), BF16)

        def reversed_block(ref, base, k, last):
            top = ref[base + (n_blk - 1 - k) * blk:base + (n_blk - k) * blk, :]
            bottom = last if k == 0 else ref[base + (n_blk - k) * blk:base + (n_blk - k + 1) * blk, :]
            return _dot(rev, jnp.concatenate([top, bottom], axis=0))

        for k in range(n_blk):
            low = u_ref[k * blk:(k + 1) * blk, :].astype(F32)
            mirrored = reversed_block(u_ref, h, k, zero_blk)
            yf_s[k * blk:(k + 1) * blk, :] = (low + mirrored).astype(BF16)
            yf_s[h + k * blk:h + (k + 1) * blk, :] = (low - mirrored).astype(BF16)
        ve_s[...] = _dot(yf_s[:h, :], mc_s[...]).astype(BF16)
        vo_s[...] = _dot(yf_s[h:, :], ms_s[...]).astype(BF16)
        sign_row = jnp.where(lax.broadcasted_iota(jnp.int32, (16, l), 1) % 2 == 0, 1.0, -1.0)
        sign_row = jnp.where(lax.broadcasted_iota(jnp.int32, (16, l), 0) == 0, sign_row, 0.0).astype(BF16)
        alt_u = _dot(sign_row, u_ref[...])
        y_nyq = _dot(alt_u.astype(BF16), mc_s[...])[0:1, :]
        v_nyq = _dot(u_ref[h:h + 16, :], mc_s[...])[0:1, :]
        sign_col = jnp.where(lax.broadcasted_iota(jnp.int32, (blk, GROUP_W), 0) % 2 == 0, 1.0, -1.0)
        first_row = lax.broadcasted_iota(jnp.int32, (blk, GROUP_W), 0) == 0
        d_s[h:h + blk, :] = jnp.where(first_row, y_nyq, 0.0).astype(BF16)
        nyq = sign_col * v_nyq
        tall = min(FFN_ROWS, h)
        nyq = jnp.concatenate([nyq] * (tall // blk), axis=0)
        for k in range(h // tall):
            rows = slice(k * tall, (k + 1) * tall)
            a = _dot(ch_ref[rows, :], ve_s[...]) + nyq
            b = _dot(sh_ref[rows, :], vo_s[...])
            yf_s[rows, :] = (a + b).astype(BF16)
            d_s[rows, :] = (a - b).astype(BF16)
        for k in range(n_blk):
            yf_s[h + k * blk:h + (k + 1) * blk, :] = reversed_block(
                d_s, 0, k, d_s[h:h + blk, :]).astype(BF16)

    tm = x_ref.shape[0]
    sub = min(FFN_ROWS, tm)
    subs = [slice(sb * sub, (sb + 1) * sub) for sb in range(tm // sub)]
    ys = []
    for sb, rs in enumerate(subs):
        r0 = pl.multiple_of(m * tm + sb * sub, sub)
        yd = (on_ref[rs, :].astype(F32) * z_ref[rs, :].astype(F32)).astype(BF16)
        ys.append(_dot(jnp.concatenate([yf_s[pl.ds(r0, sub), :], yd], axis=1), wout_s[...]))
    post_gain = gt_ref[...] * gpost_ref[...]
    pre_gain = gpre_ref[...] * (1.0 + scale_ref[...])
    for rs, y in zip(subs, ys):
        x1 = x_ref[rs, :] + _rms(y) * post_gain
        x1_ref[rs, :] = x1
        h2_ref[rs, :] = (_rms(x1) * pre_gain + shift_ref[...]).astype(BF16)


def _mix(pg, on, ch, sh, mcs, w_out, x, g_post, gt, g_pre, shift, scale):
    b, l, d = x.shape
    tm = MIX_TILE
    h = l // 2
    vec = pl.BlockSpec((1, d), lambda i, m: (0, 0))
    mod = pl.BlockSpec((None, 1, d), lambda i, m: (i, 0, 0))
    whole = lambda a: pl.BlockSpec(a.shape, lambda i, m: (0,) * a.ndim)
    return pl.pallas_call(
        _mix_kernel,
        grid=(b, l // tm),
        in_specs=[pl.BlockSpec((None, l, GROUP_W), lambda i, m: (i, 0, 0)),
                  pl.BlockSpec((None, tm, GROUP_W), lambda i, m: (i, m, 0)),
                  pl.BlockSpec((None, tm, GROUP_W), lambda i, m: (i, m, N_GROUPS - 1)),
                  whole(ch), whole(sh), whole(mcs), whole(w_out),
                  pl.BlockSpec((None, tm, d), lambda i, m: (i, m, 0)),
                  vec, mod, vec, mod, mod],
        out_specs=[pl.BlockSpec((None, tm, d), lambda i, m: (i, m, 0)),
                   pl.BlockSpec((None, tm, d), lambda i, m: (i, m, 0))],
        out_shape=[jax.ShapeDtypeStruct((b, l, d), F32),
                   jax.ShapeDtypeStruct((b, l, d), BF16)],
        scratch_shapes=[pltpu.VMEM((l, GROUP_W), BF16),
                        pltpu.VMEM((GROUP_W, GROUP_W), BF16),
                        pltpu.VMEM((GROUP_W, GROUP_W), BF16),
                        pltpu.VMEM(w_out.shape, BF16),
                        pltpu.VMEM((h, GROUP_W), BF16),
                        pltpu.VMEM((h, GROUP_W), BF16),
                        pltpu.VMEM((h + PAIR, GROUP_W), BF16)],
        compiler_params=_params(48, ("arbitrary", "arbitrary")),
        name="mix",
    )(pg, on, pg, ch, sh, mcs, w_out, x, g_post, gt, g_pre, shift, scale)


def _ffn_kernel(h_ref, x1_ref, wv_ref, wg_ref, dwc_ref, wd_ref, gpost_ref, gt_ref,
                o_ref, gate_s, val_s, wvg_s, wd_s):
    ROW_BLOCK = FFN_ROWS
    j = pl.program_id(1)
    l = h_ref.shape[0]
    tf = wv_ref.shape[1]
    n_rb = l // ROW_BLOCK
    pad = GRID_W

    wvg_s[:, :tf] = wv_ref[...].astype(BF16)
    wvg_s[:, tf:] = wg_ref[...].astype(BF16)
    wd_s[...] = wd_ref[...].astype(BF16)
    gate_s[0:pad, :] = jnp.zeros((pad, tf), F32)
    gate_s[l + pad:l + 2 * pad, :] = jnp.zeros((pad, tf), F32)

    n_ext = ROW_BLOCK + 2 * pad
    col = lax.broadcasted_iota(jnp.int32, (n_ext, tf), 0) % GRID_W
    not_first = jnp.where(col == 0, 0.0, 1.0)
    not_last = jnp.where(col == GRID_W - 1, 0.0, 1.0)
    dw = dwc_ref[...]

    def up(i):
        r0 = i * ROW_BLOCK
        vg = _dot(h_ref[r0:r0 + ROW_BLOCK, :], wvg_s[...])
        val_s[r0:r0 + ROW_BLOCK, :] = vg[:, :tf]
        gate_s[r0 + pad:r0 + pad + ROW_BLOCK, :] = vg[:, tf:]

    def down(i, first_tile):
        r0 = i * ROW_BLOCK
        ext = gate_s[r0:r0 + n_ext, :]
        shifted = (pltpu.roll(ext, 1, 0) * not_first, ext, pltpu.roll(ext, n_ext - 1, 0) * not_last)
        conv = None
        for dy in range(3):
            for dx in range(3):
                term = shifted[dx][dy * pad:dy * pad + ROW_BLOCK] * dw[3 * dy + dx:3 * dy + dx + 1]
                conv = term if conv is None else conv + term
        act = (_silu(conv) * val_s[r0:r0 + ROW_BLOCK, :]).astype(BF16)
        if first_tile:
            o_ref[r0:r0 + ROW_BLOCK, :] = _dot(act, wd_s[...])
        else:
            o_ref[r0:r0 + ROW_BLOCK, :] += _dot(act, wd_s[...])

    def tile(first_tile):
        for i in range(min(2, n_rb)):
            up(i)
        for i in range(n_rb):
            if i + 2 < n_rb:
                up(i + 2)
            down(i, first_tile)

    @pl.when(j == 0)
    def _():
        tile(True)

    @pl.when(j > 0)
    def _():
        tile(False)

    @pl.when(j == pl.num_programs(1) - 1)
    def _():
        def fin(rb, carry):
            r0 = pl.multiple_of(rb * ROW_BLOCK, ROW_BLOCK)
            y = _rms(o_ref[pl.ds(r0, ROW_BLOCK), :]) * (gt_ref[...] * gpost_ref[...])
            o_ref[pl.ds(r0, ROW_BLOCK), :] = x1_ref[pl.ds(r0, ROW_BLOCK), :] + y
            return carry
        lax.fori_loop(0, n_rb, fin, 0)


def _ffn(h2, x1, w_up, dwc, w_down, g_post, gt):
    b, l, d = x1.shape
    d_ff = w_down.shape[0]
    tf = FFN_TILE
    n_f = d_ff // tf
    return pl.pallas_call(
        _ffn_kernel,
        grid=(b, n_f),
        in_specs=[pl.BlockSpec((None, l, d), lambda i, j: (i, 0, 0)),
                  pl.BlockSpec((None, l, d), lambda i, j: (i, 0, 0)),
                  pl.BlockSpec((d, tf), lambda i, j: (0, j)),
                  pl.BlockSpec((d, tf), lambda i, j: (0, n_f + j)),
                  pl.BlockSpec((9, tf), lambda i, j: (0, j)),
                  pl.BlockSpec((tf, d), lambda i, j: (j, 0)),
                  pl.BlockSpec((1, d), lambda i, j: (0, 0)),
                  pl.BlockSpec((None, 1, d), lambda i, j: (i, 0, 0))],
        out_specs=pl.BlockSpec((None, l, d), lambda i, j: (i, 0, 0)),
        out_shape=jax.ShapeDtypeStruct((b, l, d), F32),
        scratch_shapes=[pltpu.VMEM((l + 2 * GRID_W, tf), F32),
                        pltpu.VMEM((l, tf), F32),
                        pltpu.VMEM((d, 2 * tf), BF16),
                        pltpu.VMEM((tf, d), BF16)],
        compiler_params=_params(58, ("arbitrary", "arbitrary")),
        name="ffn",
    )(h2, x1, w_up, w_up, dwc, w_down, g_post, gt)


def kernel(x, c, ctx, c_ctx, w_ada, b_ada, g_pre_mix, g_post_mix, g_pre_ffn, g_post_ffn,
           w_in, w_qkv_conv, a_log, dt_bias, g_gdn, w_fourier, w_out, w_up, w_dwc, w_down):
    b, l, d = x.shape
    assert w_ada.shape[0] == 1, "single-layer stack"
    assert b < 16 and l % MIX_TILE == 0 and ctx.shape[1] % ROW_BLOCK == 0
    assert w_in.shape[2] == N_GROUPS * GROUP_W + N_GATES and d == 2 * GROUP_W

    cc = jnp.zeros((16, d), F32).at[:b].set(c).at[b].set(c_ctx)
    mod = _ada(cc, w_ada[0], b_ada)
    sh1, sc1, gt1, sh2, sc2, gt2 = [m[:, None, :] for m in jnp.split(mod[:b], 6, axis=-1)]
    sh1c, sc1c = [m[None, :] for m in jnp.split(mod[b], 6, axis=-1)[:2]]

    w_in_t = jnp.swapaxes(w_in, 1, 2)[0]
    wabt = jnp.pad(w_in_t[N_GROUPS * GROUP_W:], ((0, PAIR - N_GATES), (0, 0)))
    zeros8 = jnp.zeros((8,), F32)
    alog = jnp.concatenate([zeros8, a_log[0].reshape(-1)])[:, None]
    dtb = jnp.concatenate([zeros8, dt_bias[0].reshape(-1)])[:, None]

    pg, kt, gat = _inproj(x, ctx, g_pre_mix, sh1, sc1, sh1c, sc1c, w_in_t, wabt, w_qkv_conv[0], alog, dtb)
    on = _gdn(pg, kt, gat, g_gdn, l)

    ch, sh, c_c, s_c = _dft_constants(l, HEAD_D)
    mcs, ch, sh = _fourier_w(c_c, s_c, w_fourier[0], ch, sh)
    x1, h2 = _mix(pg, on, ch, sh, mcs, w_out[0], x, g_post_mix, gt1, g_pre_ffn, sh2, sc2)
    d_ff = w_down.shape[1]
    return _ffn(h2, x1, w_up[0], w_dwc[0].reshape(9, d_ff), w_down[0], g_post_ffn, gt2)
```
